```python
import math
import jax, jax.numpy as jnp
from jax import lax
import numpy as np

D_MODEL = 1024
BATCH = 2
SEQ = 8192
DEPTH = 2

N_MEM = 256
CONV_CH = D_MODEL // 4
CONV_WIDTH = 31
DSA_HEADS = 8
DSA_HEAD_DIM = D_MODEL // 16
DSA_WIDTH = DSA_HEADS * DSA_HEAD_DIM
KV_RANK = D_MODEL // 8
IDX_HEADS = 4
IDX_DIM = 64
DSA_TOPK = 256
Q_BLOCK = 128
MLSTM_HEADS = 4
MLSTM_HEAD_DIM = D_MODEL // 16
MLSTM_WIDTH = MLSTM_HEADS * MLSTM_HEAD_DIM
MLSTM_CONV = 4
MLSTM_CHUNK = 64
MIX_WIDTH = CONV_CH + DSA_WIDTH + MLSTM_WIDTH
REL_BUCKETS = 32
REL_MAX_DIST = 128
XATTN_HEADS = 4
XATTN_HEAD_DIM = D_MODEL // XATTN_HEADS
PEER_HEADS = 8
PEER_KEYS = 128
PEER_EXPERTS = PEER_KEYS * PEER_KEYS
PEER_QDIM = 256
PEER_TOPK = 16
PEER_BLOCK = 128
DEEPNORM_ALPHA = (2.0 * DEPTH) ** 0.25
DEEPNORM_BETA = (8.0 * DEPTH) ** -0.25
LN_EPS = 1e-5

SPLIT_SIZES = (2 * CONV_CH,
               DSA_WIDTH, KV_RANK,
               IDX_HEADS * IDX_DIM, IDX_DIM, IDX_HEADS,
               MLSTM_WIDTH, MLSTM_WIDTH, MLSTM_WIDTH,
               MLSTM_HEADS, MLSTM_HEADS)
PROJ_WIDTH = sum(SPLIT_SIZES)
SPLIT_POINTS = tuple(int(p) for p in np.cumsum(SPLIT_SIZES)[:-1])

kernel_name = 'hybrid_conv_dsa_mlstm_peer_trunk'


def layer_norm(x, g, b):
    xf = x.astype(jnp.float32)
    mu = xf.mean(-1, keepdims=True)
    var = jnp.square(xf - mu).mean(-1, keepdims=True)
    return ((xf - mu) * lax.rsqrt(var + LN_EPS)).astype(x.dtype) * g + b


def rms_norm(x, g):
    xf = x.astype(jnp.float32)
    return (xf * lax.rsqrt(jnp.square(xf).mean(-1, keepdims=True) + LN_EPS)).astype(x.dtype) * g


def head_norm(h, g):
    B, L, H, dh = h.shape
    hf = h.astype(jnp.float32)
    mu = hf.mean(-1, keepdims=True)
    var = jnp.square(hf - mu).mean(-1, keepdims=True)
    hn = ((hf - mu) * lax.rsqrt(var + LN_EPS)).astype(h.dtype)
    return hn.reshape(B, L, H * dh) * g


def causal_depthwise_conv(x, w, b):
    K, C = w.shape
    xp = jnp.pad(x, ((0, 0), (K - 1, 0), (0, 0)))
    out = lax.conv_general_dilated(xp, w.reshape(K, 1, C), window_strides=(1,), padding='VALID',
                                   dimension_numbers=('NWC', 'WIO', 'NWC'), feature_group_count=C)
    return out + b


def rel_bucket(dist):
    max_exact = REL_BUCKETS // 2
    d = jnp.maximum(dist, 0)
    df = jnp.maximum(d, 1).astype(jnp.float32)
    large = max_exact + (jnp.log(df / max_exact) / math.log(REL_MAX_DIST / max_exact)
                         * (REL_BUCKETS - max_exact)).astype(jnp.int32)
    large = jnp.minimum(large, REL_BUCKETS - 1)
    return jnp.where(d < max_exact, d, large)


def conformer_conv_group(a_in, conv_w, conv_b, ln_g, ln_b):
    val, gate = jnp.split(a_in, 2, axis=-1)
    u = val * jax.nn.sigmoid(gate)
    u = causal_depthwise_conv(u, conv_w, conv_b)
    return jax.nn.silu(layer_norm(u, ln_g, ln_b))


def dsa_group(q, c_kv, q_idx, k_idx, w_idx, w_uk, w_uv, rel_bias):
    B, L = q.shape[0], q.shape[1]
    k_sel = min(DSA_TOPK, L // 4)
    n_blocks = L // Q_BLOCK
    q_lat = jnp.einsum('blhd,hdr->blhr', q, w_uk)
    key_pos = jnp.arange(L)
    attn_scale = DSA_HEAD_DIM ** -0.5
    idx_scale = IDX_DIM ** -0.5
    w_scale = IDX_HEADS ** -0.5

    def block(i):
        start = i * Q_BLOCK
        qi = lax.dynamic_slice_in_dim(q_idx, start, Q_BLOCK, axis=1)
        wi = lax.dynamic_slice_in_dim(w_idx, start, Q_BLOCK, axis=1)
        ql = lax.dynamic_slice_in_dim(q_lat, start, Q_BLOCK, axis=1)
        qpos = start + jnp.arange(Q_BLOCK)
        rel = jax.nn.relu(jnp.einsum('bqhd,bsd->bqhs', qi, k_idx) * idx_scale)
        score = jnp.einsum('bqhs,bqh->bqs', rel, wi * w_scale).astype(jnp.float32)
        causal = key_pos[None, :] <= qpos[:, None]
        score = jnp.where(causal[None], score, -jnp.inf)
        _, sel = lax.top_k(score, k_sel)
        valid = sel <= qpos[None, :, None]
        c_sel = jax.vmap(lambda c, ix: c[ix])(c_kv, sel)
        logits = jnp.einsum('bqhr,bqkr->bhqk', ql, c_sel).astype(jnp.float32) * attn_scale
        bias = rel_bias.astype(jnp.float32)[rel_bucket(qpos[None, :, None] - sel)]
        logits = logits + jnp.moveaxis(bias, -1, 1)
        logits = jnp.where(valid[:, None], logits, -jnp.inf)
        p = jax.nn.softmax(logits, axis=-1).astype(c_sel.dtype)
        o_lat = jnp.einsum('bhqk,bqkr->bqhr', p, c_sel)
        return jnp.einsum('bqhr,hrd->bqhd', o_lat, w_uv)

    out = lax.map(block, jnp.arange(n_blocks))
    return jnp.moveaxis(out, 0, 1).reshape(B, L, DSA_WIDTH)


def mlstm_chunkwise(q, k, v, i_pre, f_pre):
    B, H, L, dh = q.shape
    C = MLSTM_CHUNK
    nc = L // C

    def to_chunks(a):
        return jnp.moveaxis(a.reshape(B, H, nc, C, *a.shape[3:]), 2, 0)

    qc, kc, vc = to_chunks(q * dh ** -0.5), to_chunks(k), to_chunks(v)
    ic, fc = to_chunks(i_pre), to_chunks(jax.nn.log_sigmoid(f_pre))
    tri = jnp.tril(jnp.ones((C, C), dtype=bool))

    def step(carry, inp):
        Cm, n, m = carry
        qj, kj, vj, ij, fj = inp
        b = jnp.cumsum(fj, axis=-1)
        Dm = jnp.where(tri, b[..., :, None] - b[..., None, :] + ij[..., None, :], -jnp.inf)
        inter = b + m[..., None]
        m_row = jnp.maximum(inter, Dm.max(-1))
        w_inter = jnp.exp(inter - m_row)
        S = jnp.einsum('bhtd,bhsd->bhts', qj, kj) * jnp.exp(Dm - m_row[..., None])
        num = (jnp.einsum('bhts,bhsd->bhtd', S, vj)
               + w_inter[..., None] * jnp.einsum('bhvk,bhtk->bhtv', Cm, qj))
        den = S.sum(-1) + w_inter * jnp.einsum('bhk,bhtk->bht', n, qj)
        h = num / jnp.maximum(jnp.abs(den), jnp.exp(-m_row))[..., None]
        bL = b[..., -1]
        g = bL[..., None] - b + ij
        m_new = jnp.maximum(bL + m, g.max(-1))
        decay = jnp.exp(bL + m - m_new)
        wg = jnp.exp(g - m_new[..., None])
        Cm = decay[..., None, None] * Cm + jnp.einsum('bhs,bhsv,bhsk->bhvk', wg, vj, kj)
        n = decay[..., None] * n + jnp.einsum('bhs,bhsk->bhk', wg, kj)
        return (Cm, n, m_new), h

    init = (jnp.zeros((B, H, dh, dh), jnp.float32), jnp.zeros((B, H, dh), jnp.float32),
            jnp.zeros((B, H), jnp.float32))
    _, hs = lax.scan(step, init, (qc, kc, vc, ic, fc))
    return jnp.moveaxis(hs, 0, 2).reshape(B, H, L, dh)


def mlstm_group(xc_in, v_in, o_pre, i_pre, f_pre, conv_w, conv_b, w_qm, w_km, b_i, b_f, norm_g):
    B, L, _ = xc_in.shape
    H, dh = MLSTM_HEADS, MLSTM_HEAD_DIM
    xc = jax.nn.silu(causal_depthwise_conv(xc_in, conv_w, conv_b)).reshape(B, L, H, dh)
    q = jnp.einsum('blhd,hde->bhle', xc, w_qm).astype(jnp.float32)
    k = jnp.einsum('blhd,hde->bhle', xc, w_km).astype(jnp.float32)
    v = v_in.reshape(B, L, H, dh).transpose(0, 2, 1, 3).astype(jnp.float32)
    ig = (i_pre + b_i).astype(jnp.float32).transpose(0, 2, 1)
    fg = (f_pre + b_f).astype(jnp.float32).transpose(0, 2, 1)
    h = mlstm_chunkwise(q, k, v, ig, fg).astype(xc_in.dtype)
    h = head_norm(h.transpose(0, 2, 1, 3), norm_g)
    return jax.nn.sigmoid(o_pre) * h


def memory_cross_attention(x, mem, w_q, w_kv, w_o):
    B, L, D = x.shape
    M = mem.shape[1]
    q = (x @ w_q).reshape(B, L, XATTN_HEADS, XATTN_HEAD_DIM)
    k, v = jnp.split(mem @ w_kv, 2, axis=-1)
    k = k.reshape(B, M, XATTN_HEADS, XATTN_HEAD_DIM)
    v = v.reshape(B, M, XATTN_HEADS, XATTN_HEAD_DIM)
    logits = jnp.einsum('blhd,bmhd->bhlm', q, k).astype(jnp.float32) * XATTN_HEAD_DIM ** -0.5
    p = jax.nn.softmax(logits, axis=-1).astype(x.dtype)
    o = jnp.einsum('bhlm,bmhd->blhd', p, v).reshape(B, L, D)
    return o @ w_o


def peer(x, w_pq, sub_k1, sub_k2, peer_u, peer_v):
    B, L, D = x.shape
    T = B * L
    t = x.reshape(T, D)
    q = (t @ w_pq).reshape(T, PEER_HEADS, PEER_QDIM)
    q1, q2 = jnp.split(q, 2, axis=-1)
    s1 = jnp.einsum('thd,nd->thn', q1, sub_k1).astype(jnp.float32)
    s2 = jnp.einsum('thd,nd->thn', q2, sub_k2).astype(jnp.float32)
    v1, i1 = lax.top_k(s1, PEER_TOPK)
    v2, i2 = lax.top_k(s2, PEER_TOPK)
    cand = (v1[..., :, None] + v2[..., None, :]).reshape(T, PEER_HEADS, PEER_TOPK * PEER_TOPK)
    cand_idx = (i1[..., :, None] * PEER_KEYS + i2[..., None, :]).reshape(T, PEER_HEADS, PEER_TOPK * PEER_TOPK)
    top_s, pos = lax.top_k(cand, PEER_TOPK)
    experts = jnp.take_along_axis(cand_idx, pos, axis=-1)
    gates = jax.nn.softmax(top_s, axis=-1).astype(x.dtype)

    def block(i):
        start = i * PEER_BLOCK
        xs = lax.dynamic_slice_in_dim(t, start, PEER_BLOCK, axis=0)
        e = lax.dynamic_slice_in_dim(experts, start, PEER_BLOCK, axis=0)
        g = lax.dynamic_slice_in_dim(gates, start, PEER_BLOCK, axis=0)
        act = jax.nn.gelu(jnp.einsum('thkd,td->thk', peer_u[e], xs), approximate=False)
        return jnp.einsum('thk,thkd->td', g * act, peer_v[e])

    out = lax.map(block, jnp.arange(T // PEER_BLOCK))
    return out.reshape(B, L, D)


def setup_inputs(seed: int = 0) -> dict:
    key = jax.random.key(seed)
    ks = iter(jax.random.split(key, 40))

    def nrm(shape, scale):
        return scale * jax.random.normal(next(ks), shape, jnp.float32)

    def gain(shape):
        return 1.0 + nrm(shape, 0.02)

    Dm, Lr = D_MODEL, DEPTH
    beta = DEEPNORM_BETA
    return {
        'x': nrm((BATCH, SEQ, Dm), 1.0),
        'mem': nrm((BATCH, N_MEM, Dm), 1.0),
        'ln_in_g': gain((Dm,)),
        'ln_in_b': nrm((Dm,), 0.02),
        'rel_bias': nrm((REL_BUCKETS, DSA_HEADS), 0.5),
        'w_in': nrm((Lr, Dm, PROJ_WIDTH), Dm ** -0.5),
        'conv_a_w': nrm((Lr, CONV_WIDTH, CONV_CH), CONV_WIDTH ** -0.5),
        'conv_a_b': nrm((Lr, CONV_CH), 0.02),
        'norm_a_g': gain((Lr, CONV_CH)),
        'norm_a_b': nrm((Lr, CONV_CH), 0.02),
        'kv_norm_g': gain((Lr, KV_RANK)),
        'w_uk': nrm((Lr, DSA_HEADS, DSA_HEAD_DIM, KV_RANK), DSA_HEAD_DIM ** -0.5),
        'w_uv': nrm((Lr, DSA_HEADS, KV_RANK, DSA_HEAD_DIM), KV_RANK ** -0.5),
        'conv_m_w': nrm((Lr, MLSTM_CONV, MLSTM_WIDTH), MLSTM_CONV ** -0.5),
        'conv_m_b': nrm((Lr, MLSTM_WIDTH), 0.02),
        'w_qm': nrm((Lr, MLSTM_HEADS, MLSTM_HEAD_DIM, MLSTM_HEAD_DIM), MLSTM_HEAD_DIM ** -0.5),
        'w_km': nrm((Lr, MLSTM_HEADS, MLSTM_HEAD_DIM, MLSTM_HEAD_DIM), MLSTM_HEAD_DIM ** -0.5),
        'b_i': nrm((Lr, MLSTM_HEADS), 0.1),
        'b_f': jnp.linspace(3.0, 6.0, MLSTM_HEADS)[None, :] + nrm((Lr, MLSTM_HEADS), 0.1),
        'norm_m_g': gain((Lr, MLSTM_WIDTH)),
        'w_out': nrm((Lr, MIX_WIDTH, Dm), beta * MIX_WIDTH ** -0.5),
        'ln1_g': gain((Lr, Dm)),
        'ln1_b': nrm((Lr, Dm), 0.02),
        'w_cq': nrm((Lr, Dm, Dm), Dm ** -0.5),
        'w_ckv': nrm((Lr, Dm, 2 * Dm), Dm ** -0.5),
        'w_co': nrm((Lr, Dm, Dm), beta * Dm ** -0.5),
        'ln2_g': gain((Lr, Dm)),
        'ln2_b': nrm((Lr, Dm), 0.02),
        'w_pq': nrm((Lr, Dm, PEER_HEADS * PEER_QDIM), Dm ** -0.5),
        'sub_k1': nrm((Lr, PEER_KEYS, PEER_QDIM // 2), (PEER_QDIM // 2) ** -0.5),
        'sub_k2': nrm((Lr, PEER_KEYS, PEER_QDIM // 2), (PEER_QDIM // 2) ** -0.5),
        'peer_u': nrm((Lr, PEER_EXPERTS, Dm), Dm ** -0.5),
        'peer_v': nrm((Lr, PEER_EXPERTS, Dm), beta * PEER_HEADS ** -0.5),
        'ln3_g': gain((Lr, Dm)),
        'ln3_b': nrm((Lr, Dm), 0.02),
    }


def reference(x, mem, ln_in_g, ln_in_b, rel_bias, w_in, conv_a_w, conv_a_b, norm_a_g, norm_a_b,
              kv_norm_g, w_uk, w_uv, conv_m_w, conv_m_b, w_qm, w_km, b_i, b_f, norm_m_g, w_out,
              ln1_g, ln1_b, w_cq, w_ckv, w_co, ln2_g, ln2_b, w_pq, sub_k1, sub_k2, peer_u, peer_v,
              ln3_g, ln3_b):
    B, L, D = x.shape
    x = layer_norm(x, ln_in_g, ln_in_b)
    for l in range(DEPTH):
        h = x @ w_in[l]
        (a_in, q_b, c_b, qi_b, ki_b, wi_b, xc_m, v_m, o_m, i_m, f_m) = jnp.split(h, SPLIT_POINTS, axis=-1)
        y_a = conformer_conv_group(a_in, conv_a_w[l], conv_a_b[l], norm_a_g[l], norm_a_b[l])
        y_b = dsa_group(q_b.reshape(B, L, DSA_HEADS, DSA_HEAD_DIM), rms_norm(c_b, kv_norm_g[l]),
                        qi_b.reshape(B, L, IDX_HEADS, IDX_DIM), ki_b, wi_b, w_uk[l], w_uv[l], rel_bias)
        y_c = mlstm_group(xc_m, v_m, o_m, i_m, f_m, conv_m_w[l], conv_m_b[l], w_qm[l], w_km[l],
                          b_i[l], b_f[l], norm_m_g[l])
        y = jnp.concatenate([y_a, y_b, y_c], axis=-1) @ w_out[l]
        x = layer_norm(DEEPNORM_ALPHA * x + y, ln1_g[l], ln1_b[l])
        y = memory_cross_attention(x, mem, w_cq[l], w_ckv[l], w_co[l])
        x = layer_norm(DEEPNORM_ALPHA * x + y, ln2_g[l], ln2_b[l])
        y = peer(x, w_pq[l], sub_k1[l], sub_k2[l], peer_u[l], peer_v[l])
        x = layer_norm(DEEPNORM_ALPHA * x + y, ln3_g[l], ln3_b[l])
    return x
```

```python
import functools
import math

import numpy as np
import jax
import jax.numpy as jnp
from jax import lax
from jax.experimental import pallas as pl
from jax.experimental.pallas import tpu as pltpu

F32 = jnp.float32
BF16 = jnp.bfloat16
I32 = jnp.int32

D_MODEL = 1024
CONV_CH = 256
CONV_WIDTH = 31
DSA_HEADS = 8
DSA_HEAD_DIM = 64
DSA_WIDTH = 512
KV_RANK = 128
IDX_HEADS = 4
IDX_DIM = 64
DSA_TOPK = 256
MLSTM_HEADS = 4
MLSTM_HEAD_DIM = 64
MLSTM_WIDTH = 256
MLSTM_CONV = 4
MLSTM_CHUNK = 64
REL_BUCKETS = 32
REL_MAX_DIST = 128
XATTN_HEADS = 4
XATTN_HEAD_DIM = 256
PEER_HEADS = 8
PEER_KEYS = 128
PEER_QDIM = 256
PEER_TOPK = 16
LN_EPS = 1e-5

LANES = 128
SUBLANES = 8
VMEM_LIMIT = 56 * 1024 * 1024

NEG_INF = float("-inf")
INT_MIN = -(2 ** 31)

MAIN_W = 1920
IDX_W = 384
SM_WI = 64
SM_I = 72
SM_F = 80


def _cparams(sem):
    return pltpu.CompilerParams(dimension_semantics=sem, vmem_limit_bytes=VMEM_LIMIT)


def _dot(a, b):
    return jnp.dot(a, b, preferred_element_type=F32)


def _dot_t(a, b):
    return lax.dot_general(a, b, (((1,), (1,)), ((), ())), preferred_element_type=F32)


def _dot_tl(a, b):
    return lax.dot_general(a, b, (((0,), (0,)), ((), ())), preferred_element_type=F32)


def _split(a):
    hi = a.astype(BF16)
    lo = (a - hi.astype(F32)).astype(BF16)
    return hi, lo


def _dot3(a, b_hi, b_lo, dot=_dot):
    a_hi, a_lo = _split(a)
    return dot(a_hi, b_hi) + dot(a_lo, b_hi) + dot(a_hi, b_lo)


def _layer_norm(x, g, b):
    mu = jnp.mean(x, axis=-1, keepdims=True)
    xc = x - mu
    var = jnp.mean(xc * xc, axis=-1, keepdims=True)
    return xc * lax.rsqrt(var + LN_EPS) * g + b


def _wsplit(w):
    hi = w.astype(BF16)
    lo = (w - hi.astype(F32)).astype(BF16)
    return hi, lo


def _ln_kernel(x_ref, g_ref, b_ref, o_ref):
    o_ref[...] = _layer_norm(x_ref[...], g_ref[...], b_ref[...])


def _entry_ln(x, g, b, tm=512):
    T, D = x.shape
    return pl.pallas_call(
        _ln_kernel,
        grid=(T // tm,),
        in_specs=[pl.BlockSpec((tm, D), lambda i: (i, 0)),
                  pl.BlockSpec((1, D), lambda i: (0, 0)),
                  pl.BlockSpec((1, D), lambda i: (0, 0))],
        out_specs=pl.BlockSpec((tm, D), lambda i: (i, 0)),
        out_shape=jax.ShapeDtypeStruct((T, D), F32),
        compiler_params=_cparams(("parallel",)),
        name="entry_ln",
    )(x, g.reshape(1, D), b.reshape(1, D))


def _mm1_kernel(x_ref, w_ref, o_ref):
    o_ref[...] = _dot(x_ref[...].astype(BF16), w_ref[...])


def _mm3_kernel(x_ref, wh_ref, wl_ref, o_ref):
    o_ref[...] = _dot3(x_ref[...], wh_ref[...], wl_ref[...])


def _matmul(x, w, passes, tm, tn, name):
    T, K = x.shape
    N = w.shape[1]
    x_spec = pl.BlockSpec((tm, K), lambda j, i: (i, 0))
    w_spec = pl.BlockSpec((K, tn), lambda j, i: (0, j))
    if passes == 1:
        kern, ws, w_specs = _mm1_kernel, (w.astype(BF16),), [w_spec]
    else:
        kern, ws, w_specs = _mm3_kernel, _wsplit(w), [w_spec, w_spec]
    return pl.pallas_call(
        kern,
        grid=(N // tn, T // tm),
        in_specs=[x_spec] + w_specs,
        out_specs=pl.BlockSpec((tm, tn), lambda j, i: (i, j)),
        out_shape=jax.ShapeDtypeStruct((T, N), F32),
        compiler_params=_cparams(("parallel", "parallel")),
        name=name,
    )(x, *ws)


CONV_HALO = 32
CONV_ROWS = 64


def _conv_kernel(cur_ref, halo_ref, w_ref, b_ref, g_ref, bb_ref, o_ref, ubuf, *, tl):
    i = pl.program_id(1)
    cur = cur_ref[...]
    ubuf[CONV_HALO:CONV_HALO + tl, :] = cur[:, :CONV_CH] * jax.nn.sigmoid(cur[:, CONV_CH:])
    hal = halo_ref[...]
    uh = hal[:, :CONV_CH] * jax.nn.sigmoid(hal[:, CONV_CH:])
    ubuf[0:CONV_HALO, :] = jnp.where(i > 0, uh, 0.0)
    base = CONV_HALO - (CONV_WIDTH - 1)
    for c in range(tl // CONV_ROWS):
        r0 = c * CONV_ROWS
        acc = jnp.broadcast_to(b_ref[...], (CONV_ROWS, CONV_CH))
        for k in range(CONV_WIDTH):
            acc = acc + w_ref[k:k + 1, :] * ubuf[r0 + base + k:r0 + base + k + CONV_ROWS, :]
        y = _layer_norm(acc, g_ref[...], bb_ref[...])
        o_ref[r0:r0 + CONV_ROWS, :] = y * jax.nn.sigmoid(y)


def _conv_group(h_main, conv_w, conv_b, ln_g, ln_b, B, L, tl=256):
    T = B * L
    nl = L // tl
    hb = tl // CONV_HALO
    return pl.pallas_call(
        functools.partial(_conv_kernel, tl=tl),
        grid=(B, nl),
        in_specs=[
            pl.BlockSpec((tl, 2 * CONV_CH), lambda b, i: (b * nl + i, 0)),
            pl.BlockSpec((CONV_HALO, 2 * CONV_CH),
                         lambda b, i: (jnp.maximum((b * nl + i) * hb - 1, 0), 0)),
            pl.BlockSpec((CONV_WIDTH, CONV_CH), lambda b, i: (0, 0)),
            pl.BlockSpec((1, CONV_CH), lambda b, i: (0, 0)),
            pl.BlockSpec((1, CONV_CH), lambda b, i: (0, 0)),
            pl.BlockSpec((1, CONV_CH), lambda b, i: (0, 0)),
        ],
        out_specs=pl.BlockSpec((tl, CONV_CH), lambda b, i: (b * nl + i, 0)),
        out_shape=jax.ShapeDtypeStruct((T, CONV_CH), F32),
        scratch_shapes=[pltpu.VMEM((CONV_HALO + tl, CONV_CH), F32)],
        compiler_params=_cparams(("parallel", "parallel")),
        name="conv_group",
    )(h_main, h_main, conv_w, conv_b.reshape(1, -1), ln_g.reshape(1, -1), ln_b.reshape(1, -1))


def _dsa_prep_kernel(c_ref, sm_ref, g_ref, ckv_ref, kp_ref):
    c = c_ref[...]
    ms = jnp.mean(c * c, axis=-1, keepdims=True)
    ckv_ref[...] = (c * lax.rsqrt(ms + LN_EPS) * g_ref[...]).astype(BF16)
    k_hi, k_lo = _split(sm_ref[...][:, :IDX_DIM])
    kp_ref[...] = jnp.concatenate([k_hi, k_hi, k_lo, jnp.zeros_like(k_hi)], axis=-1)


def _dsa_prep(h_main, h_idx, kv_g, tm=512):
    T = h_main.shape[0]
    return pl.pallas_call(
        _dsa_prep_kernel,
        grid=(T // tm,),
        in_specs=[pl.BlockSpec((tm, KV_RANK), lambda i: (i, 14)),
                  pl.BlockSpec((tm, LANES), lambda i: (i, 2)),
                  pl.BlockSpec((1, KV_RANK), lambda i: (0, 0))],
        out_specs=[pl.BlockSpec((tm, KV_RANK), lambda i: (i, 0)),
                   pl.BlockSpec((tm, 4 * IDX_DIM), lambda i: (i, 0))],
        out_shape=[jax.ShapeDtypeStruct((T, KV_RANK), BF16),
                   jax.ShapeDtypeStruct((T, 4 * IDX_DIM), BF16)],
        compiler_params=_cparams(("parallel",)),
        name="dsa_prep",
    )(h_main, h_idx, kv_g.reshape(1, -1))


TQ = 128
SCORE_COLS = 512


def _dsa_kernel(qb_ref, qi_ref, sm_ref, kp_ref, ckv_ref, wuk_ref, wuv_ref, bias_ref, o_ref,
                keys_ref, eqc_ref, selb_ref, ql_ref, m_ref, l_ref, acc_ref, *, k_sel, pos_bits):
    qt = pl.program_id(1)
    n_chunks = qt // (SCORE_COLS // TQ) + 1
    row_t = qt * TQ + lax.broadcasted_iota(I32, (TQ, SCORE_COLS), 0)
    lane_c = lax.broadcasted_iota(I32, (TQ, SCORE_COLS), 1)

    qi = qi_ref[...]
    sm = sm_ref[...]
    qp = []
    for h in range(IDX_HEADS):
        q_hi, q_lo = _split(qi[:, h * IDX_DIM:(h + 1) * IDX_DIM])
        qp.append(jnp.concatenate([q_hi, q_lo, q_hi, jnp.zeros_like(q_hi)], axis=-1))
    qp = jnp.concatenate(qp, axis=0)
    w_fold = (IDX_DIM ** -0.5) * (IDX_HEADS ** -0.5)
    ws = [sm[:, SM_WI + h:SM_WI + h + 1] * w_fold for h in range(IDX_HEADS)]

    def score_chunk(c, carry):
        c0 = pl.multiple_of(c * SCORE_COLS, SCORE_COLS)
        d = _dot_t(qp, kp_ref[pl.ds(c0, SCORE_COLS), :])
        s = jnp.zeros((TQ, SCORE_COLS), F32)
        for h in range(IDX_HEADS):
            s = s + jnp.maximum(d[h * TQ:(h + 1) * TQ], 0.0) * ws[h]
        s = jnp.where(c0 + lane_c <= row_t, s + 0.0, NEG_INF)
        bits = lax.bitcast_convert_type(s, I32)
        keys_ref[:, pl.ds(c0, SCORE_COLS)] = bits ^ ((bits >> 31) & 0x7FFFFFFF)
        return carry

    lax.fori_loop(0, n_chunks, score_chunk, 0)

    def lane_total(acc):
        tot = jnp.sum(acc.astype(F32), axis=-1, keepdims=True)
        return jnp.broadcast_to(tot, (TQ, LANES))

    def count(ref, pred):
        def body(c, acc):
            c0 = pl.multiple_of(c * SCORE_COLS, SCORE_COLS)
            for a in range(SCORE_COLS // LANES):
                blk = ref[:, pl.ds(c0 + a * LANES, LANES)]
                acc = acc + jnp.where(pred(blk), 1, 0)
            return acc
        return lane_total(lax.fori_loop(0, n_chunks, body, jnp.zeros((TQ, LANES), I32)))

    kf = float(k_sel)
    zero = jnp.zeros((TQ, LANES), I32)
    c_nonneg = count(keys_ref, lambda blk: blk >= zero)
    theta0 = jnp.where(c_nonneg >= kf, 0, INT_MIN).astype(I32)

    def bit_step(it, theta):
        cand = theta | (jnp.int32(1) << (30 - it))
        cnt = count(keys_ref, lambda blk: blk >= cand)
        return jnp.where(cnt >= kf, cand, theta)

    theta = lax.fori_loop(0, 31, bit_step, theta0)
    need = kf - count(keys_ref, lambda blk: blk > theta)

    def eq_chunk(c, carry):
        c0 = pl.multiple_of(c * SCORE_COLS, SCORE_COLS)
        for a in range(SCORE_COLS // LANES):
            off = c0 + a * LANES
            blk = keys_ref[:, pl.ds(off, LANES)]
            col = off + lax.broadcasted_iota(I32, (TQ, LANES), 1)
            eqc_ref[:, pl.ds(off, LANES)] = jnp.where(blk == theta, col, jnp.int32(2 ** 30))
        return carry

    lax.fori_loop(0, n_chunks, eq_chunk, 0)

    def pos_step(it, q):
        cand = q | (jnp.int32(1) << (pos_bits - 1 - it))
        cnt = count(eqc_ref, lambda blk: blk < cand)
        return jnp.where(cnt < need, cand, q)

    q_pos = lax.fori_loop(0, pos_bits, pos_step, zero)

    def sel_chunk(c, carry):
        c0 = pl.multiple_of(c * SCORE_COLS, SCORE_COLS)
        for a in range(SCORE_COLS // LANES):
            off = c0 + a * LANES
            blk = keys_ref[:, pl.ds(off, LANES)]
            eqc = eqc_ref[:, pl.ds(off, LANES)]
            col = off + lax.broadcasted_iota(I32, (TQ, LANES), 1)
            picked = jnp.where(blk > theta, 1, jnp.where(eqc <= q_pos, 1, 0))
            causal = col <= row_t[:, :LANES]
            selb_ref[:, pl.ds(off, LANES)] = jnp.where(causal, jnp.where(picked > 0, 0.0, NEG_INF),
                                                       NEG_INF)
        return carry

    lax.fori_loop(0, n_chunks, sel_chunk, 0)

    qb = qb_ref[...]
    scale = DSA_HEAD_DIM ** -0.5
    for h in range(DSA_HEADS):
        qh = qb[:, h * DSA_HEAD_DIM:(h + 1) * DSA_HEAD_DIM].astype(BF16)
        ql_ref[h * TQ:(h + 1) * TQ, :] = (_dot(qh, wuk_ref[h]) * scale).astype(BF16)

    m_ref[...] = jnp.full(m_ref.shape, -1e30, F32)
    l_ref[...] = jnp.zeros(l_ref.shape, F32)
    acc_ref[...] = jnp.zeros(acc_ref.shape, F32)

    def attend(j, bias_lo):
        c0 = pl.multiple_of(j * TQ, TQ)
        ck = ckv_ref[pl.ds(c0, TQ), :]
        sb = selb_ref[:, pl.ds(c0, TQ)]
        lg = _dot_t(ql_ref[...], ck)
        for h in range(DSA_HEADS):
            s = lg[h * TQ:(h + 1) * TQ] + sb
            if bias_lo is not None:
                s = s + bias_ref[h, :, bias_lo:bias_lo + TQ]
            m_old = m_ref[h]
            m_new = jnp.maximum(m_old, jnp.broadcast_to(jnp.max(s, axis=-1, keepdims=True),
                                                        (TQ, LANES)))
            alpha = jnp.exp(m_old - m_new)
            p = jnp.exp(s - m_new)
            l_ref[h] = alpha * l_ref[h] + jnp.broadcast_to(jnp.sum(p, axis=-1, keepdims=True),
                                                           (TQ, LANES))
            acc_ref[h] = alpha * acc_ref[h] + _dot(p.astype(BF16), ck)
            m_ref[h] = m_new

    def far(j, carry):
        attend(j, None)
        return carry

    lax.fori_loop(0, jnp.maximum(qt - 1, 0), far, 0)

    @pl.when(qt > 0)
    def _():
        attend(qt - 1, 0)

    attend(qt, TQ)

    out = jnp.zeros((TQ, DSA_WIDTH), F32)
    for h in range(DSA_HEADS):
        o_lat = acc_ref[h] / l_ref[h]
        out = out + _dot(o_lat.astype(BF16), wuv_ref[h])
    o_ref[...] = out


def _rel_bucket_table(n):
    max_exact = REL_BUCKETS // 2
    d = np.arange(n)
    df = np.maximum(d, 1).astype(np.float32)
    large = max_exact + (np.log(df / np.float32(max_exact)) / np.float32(math.log(REL_MAX_DIST / max_exact))
                         * np.float32(REL_BUCKETS - max_exact)).astype(np.int32)
    large = np.minimum(large, REL_BUCKETS - 1)
    return np.where(d < max_exact, d, large)


def _dsa_attention(h_main, h_idx, ckv, kp, w_uk, w_uv, rel_bias, B, L):
    T = B * L
    nq = L // TQ
    k_sel = min(DSA_TOPK, L // 4)
    pos_bits = max(1, int(math.ceil(math.log2(L))))
    dist = TQ + np.arange(TQ)[:, None] - np.arange(2 * TQ)[None, :]
    assert _rel_bucket_table(REL_MAX_DIST * 4)[TQ:].min() == REL_BUCKETS - 1
    bucket = _rel_bucket_table(2 * TQ + 1)[np.maximum(dist, 0)]
    rb = rel_bias.astype(F32)
    bias_near = jnp.transpose(rb[bucket] - rb[REL_BUCKETS - 1], (2, 0, 1))
    wuv_band = jnp.zeros((DSA_HEADS, KV_RANK, DSA_WIDTH), F32)
    for h in range(DSA_HEADS):
        wuv_band = wuv_band.at[h, :, h * DSA_HEAD_DIM:(h + 1) * DSA_HEAD_DIM].set(w_uv[h])
    kern = functools.partial(_dsa_kernel, k_sel=k_sel, pos_bits=pos_bits)
    lpad = ((L + SCORE_COLS - 1) // SCORE_COLS) * SCORE_COLS
    return pl.pallas_call(
        kern,
        grid=(B, nq),
        in_specs=[
            pl.BlockSpec((TQ, DSA_WIDTH), lambda b, i: (b * nq + i, 1)),
            pl.BlockSpec((TQ, IDX_HEADS * IDX_DIM), lambda b, i: (b * nq + i, 0)),
            pl.BlockSpec((TQ, LANES), lambda b, i: (b * nq + i, 2)),
            pl.BlockSpec((L, 4 * IDX_DIM), lambda b, i: (b, 0)),
            pl.BlockSpec((L, KV_RANK), lambda b, i: (b, 0)),
            pl.BlockSpec((DSA_HEADS, DSA_HEAD_DIM, KV_RANK), lambda b, i: (0, 0, 0)),
            pl.BlockSpec((DSA_HEADS, KV_RANK, DSA_WIDTH), lambda b, i: (0, 0, 0)),
            pl.BlockSpec((DSA_HEADS, TQ, 2 * TQ), lambda b, i: (0, 0, 0)),
        ],
        out_specs=pl.BlockSpec((TQ, DSA_WIDTH), lambda b, i: (b * nq + i, 0)),
        out_shape=jax.ShapeDtypeStruct((T, DSA_WIDTH), F32),
        scratch_shapes=[
            pltpu.VMEM((TQ, lpad), I32),
            pltpu.VMEM((TQ, lpad), I32),
            pltpu.VMEM((TQ, lpad), F32),
            pltpu.VMEM((DSA_HEADS * TQ, KV_RANK), BF16),
            pltpu.VMEM((DSA_HEADS, TQ, LANES), F32),
            pltpu.VMEM((DSA_HEADS, TQ, LANES), F32),
            pltpu.VMEM((DSA_HEADS, TQ, KV_RANK), F32),
        ],
        compiler_params=_cparams(("parallel", "arbitrary")),
        name="dsa_attention",
    )(h_main, h_idx, h_idx, kp, ckv, w_uk.astype(BF16), wuv_band.astype(BF16), bias_near)


def _mlstm_prep_kernel(cur_ref, halo_ref, w_ref, b_ref, wqh_ref, wql_ref, wkh_ref, wkl_ref,
                       q_ref, k_ref, xbuf, *, tl):
    i = pl.program_id(1)
    xbuf[SUBLANES:SUBLANES + tl, :] = cur_ref[...]
    xbuf[0:SUBLANES, :] = jnp.where(i > 0, halo_ref[...], 0.0)
    base = SUBLANES - (MLSTM_CONV - 1)
    acc = jnp.broadcast_to(b_ref[...], (tl, MLSTM_WIDTH))
    for k in range(MLSTM_CONV):
        acc = acc + w_ref[k:k + 1, :] * xbuf[base + k:base + k + tl, :]
    xc = acc * jax.nn.sigmoid(acc)
    q_ref[...] = _dot3(xc, wqh_ref[...], wql_ref[...]) * (MLSTM_HEAD_DIM ** -0.5)
    k_ref[...] = _dot3(xc, wkh_ref[...], wkl_ref[...])


def _block_diag(w):
    h, d, e = w.shape
    out = jnp.zeros((h * d, h * e), F32)
    for i in range(h):
        out = out.at[i * d:(i + 1) * d, i * e:(i + 1) * e].set(w[i])
    return out


def _mlstm_prep(h_main, conv_w, conv_b, w_qm, w_km, B, L, tl=512):
    T = B * L
    nl = L // tl
    hb = tl // SUBLANES
    wq = _wsplit(_block_diag(w_qm))
    wk = _wsplit(_block_diag(w_km))
    full = lambda shape: pl.BlockSpec(shape, lambda b, i: (0,) * len(shape))
    return pl.pallas_call(
        functools.partial(_mlstm_prep_kernel, tl=tl),
        grid=(B, nl),
        in_specs=[
            pl.BlockSpec((tl, MLSTM_WIDTH), lambda b, i: (b * nl + i, 4)),
            pl.BlockSpec((SUBLANES, MLSTM_WIDTH),
                         lambda b, i: (jnp.maximum((b * nl + i) * hb - 1, 0), 4)),
            full((MLSTM_CONV, MLSTM_WIDTH)), full((1, MLSTM_WIDTH)),
            full((MLSTM_WIDTH, MLSTM_WIDTH)), full((MLSTM_WIDTH, MLSTM_WIDTH)),
            full((MLSTM_WIDTH, MLSTM_WIDTH)), full((MLSTM_WIDTH, MLSTM_WIDTH)),
        ],
        out_specs=[pl.BlockSpec((tl, MLSTM_WIDTH), lambda b, i: (b * nl + i, 0)),
                   pl.BlockSpec((tl, MLSTM_WIDTH), lambda b, i: (b * nl + i, 0))],
        out_shape=[jax.ShapeDtypeStruct((T, MLSTM_WIDTH), F32),
                   jax.ShapeDtypeStruct((T, MLSTM_WIDTH), F32)],
        scratch_shapes=[pltpu.VMEM((SUBLANES + tl, MLSTM_WIDTH), F32)],
        compiler_params=_cparams(("parallel", "parallel")),
        name="mlstm_prep",
    )(h_main, h_main, conv_w, conv_b.reshape(1, -1), wq[0], wq[1], wk[0], wk[1])


ML_ROWS = 128


def _mlstm_kernel(q_ref, k_ref, v_ref, o_ref, sm_ref, bi_ref, bf_ref, g_ref, out_ref,
                  cm_ref, n_ref, m_ref):
    C = MLSTM_CHUNK
    dh = MLSTM_HEAD_DIM

    @pl.when(pl.program_id(1) == 0)
    def _():
        cm_ref[...] = jnp.zeros(cm_ref.shape, F32)
        n_ref[...] = jnp.zeros(n_ref.shape, F32)
        m_ref[...] = jnp.zeros(m_ref.shape, F32)

    sm_t = sm_ref[...].T
    ig = sm_t[SM_I:SM_I + SUBLANES] + bi_ref[...]
    fg = jax.nn.log_sigmoid(sm_t[SM_F:SM_F + SUBLANES] + bf_ref[...])
    lane = lax.broadcasted_iota(I32, (SUBLANES, ML_ROWS), 1) & (C - 1)
    bcum = fg
    s = 1
    while s < C:
        bcum = bcum + jnp.where(lane >= s, pltpu.roll(bcum, s, axis=1), 0.0)
        s *= 2
    cols = jnp.concatenate([bcum, ig, jnp.zeros((LANES - 2 * SUBLANES, ML_ROWS), F32)], axis=0).T
    tri = (lax.broadcasted_iota(I32, (C, C), 1) <= lax.broadcasted_iota(I32, (C, C), 0))

    q_all, k_all, v_all, o_all = q_ref[...], k_ref[...], v_ref[...], o_ref[...]
    g_all = g_ref[...]
    for c in range(ML_ROWS // C):
        r0 = c * C
        for h in range(MLSTM_HEADS):
            hs = slice(h * dh, (h + 1) * dh)
            qj = q_all[r0:r0 + C, hs]
            kj = k_all[r0:r0 + C, hs]
            vj = v_all[r0:r0 + C, hs]
            b_row = bcum[h:h + 1, r0:r0 + C]
            i_row = ig[h:h + 1, r0:r0 + C]
            b_col = cols[r0:r0 + C, h:h + 1]
            i_col = cols[r0:r0 + C, SUBLANES + h:SUBLANES + h + 1]
            m_prev = m_ref[h:h + 1, 0:1]
            n_prev = n_ref[h:h + 1, :]
            cm_prev = cm_ref[h]

            dm = jnp.where(tri, b_col - b_row + i_row, NEG_INF)
            inter = b_col + m_prev
            m_row = jnp.maximum(inter, jnp.max(dm, axis=-1, keepdims=True))
            w_inter = jnp.exp(inter - m_row)
            qb, kb, vb = qj.astype(BF16), kj.astype(BF16), vj.astype(BF16)
            sw = _dot_t(qb, kb) * jnp.exp(dm - m_row)
            num = _dot(sw.astype(BF16), vb) + w_inter * _dot(qb, cm_prev.astype(BF16))
            den = (jnp.sum(sw, axis=-1, keepdims=True)
                   + w_inter * jnp.sum(qj * n_prev, axis=-1, keepdims=True))
            hh = num / jnp.maximum(jnp.abs(den), jnp.exp(-m_row))

            b_last = b_row[:, C - 1:C]
            g_row = b_last - b_row + i_row
            g_col = b_last - b_col + i_col
            m_new = jnp.maximum(b_last + m_prev, jnp.max(g_row, axis=-1, keepdims=True))
            decay = jnp.exp(b_last + m_prev - m_new)
            kw = kj * jnp.exp(g_col - m_new)
            cm_ref[h] = decay * cm_prev + _dot_tl(kw.astype(BF16), vb)
            n_ref[h:h + 1, :] = decay * n_prev + jnp.sum(kw, axis=0, keepdims=True)
            m_ref[h:h + 1, :] = jnp.broadcast_to(m_new, (1, LANES))

            mu = jnp.mean(hh, axis=-1, keepdims=True)
            hc = hh - mu
            var = jnp.mean(hc * hc, axis=-1, keepdims=True)
            hn = hc * lax.rsqrt(var + LN_EPS) * g_all[:, hs]
            out_ref[r0:r0 + C, hs] = jax.nn.sigmoid(o_all[r0:r0 + C, hs]) * hn


def _mlstm(q, k, h_main, h_idx, b_i, b_f, norm_g, B, L):
    T = B * L
    nl = L // ML_ROWS
    pad8 = lambda v: jnp.pad(v.astype(F32), (0, SUBLANES - MLSTM_HEADS)).reshape(SUBLANES, 1)
    row = lambda col: pl.BlockSpec((ML_ROWS, MLSTM_WIDTH), lambda b, i: (b * nl + i, col))
    return pl.pallas_call(
        _mlstm_kernel,
        grid=(B, nl),
        in_specs=[row(0), row(0), row(5), row(6),
                  pl.BlockSpec((ML_ROWS, LANES), lambda b, i: (b * nl + i, 2)),
                  pl.BlockSpec((SUBLANES, 1), lambda b, i: (0, 0)),
                  pl.BlockSpec((SUBLANES, 1), lambda b, i: (0, 0)),
                  pl.BlockSpec((1, MLSTM_WIDTH), lambda b, i: (0, 0))],
        out_specs=row(0),
        out_shape=jax.ShapeDtypeStruct((T, MLSTM_WIDTH), F32),
        scratch_shapes=[pltpu.VMEM((MLSTM_HEADS, MLSTM_HEAD_DIM, MLSTM_HEAD_DIM), F32),
                        pltpu.VMEM((SUBLANES, MLSTM_HEAD_DIM), F32),
                        pltpu.VMEM((SUBLANES, LANES), F32)],
        compiler_params=_cparams(("parallel", "arbitrary")),
        name="mlstm_scan",
    )(q, k, h_main, h_main, h_idx, pad8(b_i), pad8(b_f), norm_g.reshape(1, -1))


def _mix_out_kernel(ya_ref, yb_ref, yc_ref, x_ref, wa_ref, wb_ref, wc_ref, g_ref, b_ref, o_ref, *,
                    alpha):
    y = (_dot(ya_ref[...].astype(BF16), wa_ref[...])
         + _dot(yb_ref[...].astype(BF16), wb_ref[...])
         + _dot(yc_ref[...].astype(BF16), wc_ref[...]))
    o_ref[...] = _layer_norm(alpha * x_ref[...] + y, g_ref[...], b_ref[...])


def _mix_out(y_a, y_b, y_c, x, w_out, g, b, alpha, tm=512):
    T, D = x.shape
    w = w_out.astype(BF16)
    wa, wb, wc = w[:CONV_CH], w[CONV_CH:CONV_CH + DSA_WIDTH], w[CONV_CH + DSA_WIDTH:]
    rows = lambda width: pl.BlockSpec((tm, width), lambda i: (i, 0))
    full = lambda a: pl.BlockSpec(a.shape, lambda i: (0, 0))
    g2, b2 = g.reshape(1, D), b.reshape(1, D)
    return pl.pallas_call(
        functools.partial(_mix_out_kernel, alpha=alpha),
        grid=(T // tm,),
        in_specs=[rows(CONV_CH), rows(DSA_WIDTH), rows(MLSTM_WIDTH), rows(D),
                  full(wa), full(wb), full(wc), full(g2), full(b2)],
        out_specs=rows(D),
        out_shape=jax.ShapeDtypeStruct((T, D), F32),
        compiler_params=_cparams(("parallel",)),
        name="mix_out",
    )(y_a, y_b, y_c, x, wa, wb, wc, g2, b2)


def _xattn_kernel(x_ref, kv_ref, wq_ref, wo_ref, g_ref, b_ref, o_ref, *, alpha):
    x = x_ref[...]
    q = _dot(x.astype(BF16), wq_ref[...])
    kv = kv_ref[...]
    scale = XATTN_HEAD_DIM ** -0.5
    outs = []
    for h in range(XATTN_HEADS):
        hs = slice(h * XATTN_HEAD_DIM, (h + 1) * XATTN_HEAD_DIM)
        kh = kv[:, hs].astype(BF16)
        vh = kv[:, D_MODEL + h * XATTN_HEAD_DIM:D_MODEL + (h + 1) * XATTN_HEAD_DIM].astype(BF16)
        lg = _dot_t(q[:, hs].astype(BF16), kh) * scale
        lg = lg - jnp.max(lg, axis=-1, keepdims=True)
        p = jnp.exp(lg)
        p = p / jnp.sum(p, axis=-1, keepdims=True)
        outs.append(_dot(p.astype(BF16), vh))
    o = jnp.concatenate(outs, axis=-1)
    y = _dot(o.astype(BF16), wo_ref[...])
    o_ref[...] = _layer_norm(alpha * x + y, g_ref[...], b_ref[...])


def _xattn(x, kv, w_q, w_o, g, b, alpha, B, L, tm=256):
    T, D = x.shape
    nl = L // tm
    M = kv.shape[0] // B
    full = lambda a: pl.BlockSpec(a.shape, lambda bb, i: (0, 0))
    g2, b2 = g.reshape(1, D), b.reshape(1, D)
    wq, wo = w_q.astype(BF16), w_o.astype(BF16)
    return pl.pallas_call(
        functools.partial(_xattn_kernel, alpha=alpha),
        grid=(B, nl),
        in_specs=[pl.BlockSpec((tm, D), lambda bb, i: (bb * nl + i, 0)),
                  pl.BlockSpec((M, 2 * D), lambda bb, i: (bb, 0)),
                  full(wq), full(wo), full(g2), full(b2)],
        out_specs=pl.BlockSpec((tm, D), lambda bb, i: (bb * nl + i, 0)),
        out_shape=jax.ShapeDtypeStruct((T, D), F32),
        compiler_params=_cparams(("parallel", "parallel")),
        name="xattn",
    )(x, kv, wq, wo, g2, b2)


def _peer_score_kernel(x_ref, wh_ref, wl_ref, k1h_ref, k1l_ref, k2h_ref, k2l_ref, st_ref):
    q = _dot3(x_ref[...], wh_ref[...], wl_ref[...])
    half = PEER_QDIM // 2
    for h in range(PEER_HEADS):
        for part, (kh_ref, kl_ref) in enumerate(((k1h_ref, k1l_ref), (k2h_ref, k2l_ref))):
            c0 = h * PEER_QDIM + part * half
            q_hi, q_lo = _split(q[:, c0:c0 + half])
            kh, kl = kh_ref[...], kl_ref[...]
            st_ref[2 * h + part] = _dot_t(kh, q_hi) + _dot_t(kl, q_hi) + _dot_t(kh, q_lo)


def _peer_scores(x, w_pq, sub_k1, sub_k2, tm=256):
    T, D = x.shape
    wh, wl = _wsplit(w_pq)
    k1h, k1l = _wsplit(sub_k1)
    k2h, k2l = _wsplit(sub_k2)
    full = lambda a: pl.BlockSpec(a.shape, lambda i: (0, 0))
    return pl.pallas_call(
        _peer_score_kernel,
        grid=(T // tm,),
        in_specs=[pl.BlockSpec((tm, D), lambda i: (i, 0)), full(wh), full(wl),
                  full(k1h), full(k1l), full(k2h), full(k2l)],
        out_specs=pl.BlockSpec((2 * PEER_HEADS, PEER_KEYS, tm), lambda i: (0, 0, i)),
        out_shape=jax.ShapeDtypeStruct((2 * PEER_HEADS, PEER_KEYS, T), F32),
        compiler_params=_cparams(("parallel",)),
        name="peer_scores",
    )(x, wh, wl, k1h, k1l, k2h, k2l)


def _peer_thr_kernel(st_ref, stats_ref):
    def top_rows(x):
        rows = []
        for _ in range(PEER_TOPK):
            m = jnp.max(x, axis=0, keepdims=True)
            rows.append(m)
            x = jnp.where(x == m, NEG_INF, x)
        return rows

    v1 = top_rows(st_ref[0])
    v2 = top_rows(st_ref[1])
    v2_all = jnp.concatenate(v2, axis=0)
    cand = jnp.concatenate([v1[k] + v2_all for k in range(PEER_TOPK)], axis=0)
    x = cand
    for _ in range(PEER_TOPK - 1):
        m = jnp.max(x, axis=0, keepdims=True)
        x = jnp.where(x == m, NEG_INF, x)
    thr = jnp.max(x, axis=0, keepdims=True)
    top = v1[0] + v2[0]
    z = jnp.sum(jnp.where(cand >= thr, jnp.exp(cand - top), 0.0), axis=0, keepdims=True)
    pad = jnp.zeros((SUBLANES - 4, thr.shape[1]), F32)
    stats_ref[0] = jnp.concatenate([thr, v1[0], v2[0], 1.0 / z, pad], axis=0)


def _peer_thresholds(st, tm=256):
    T = st.shape[2]
    return pl.pallas_call(
        _peer_thr_kernel,
        grid=(PEER_HEADS, T // tm),
        in_specs=[pl.BlockSpec((2, PEER_KEYS, tm), lambda h, i: (h, 0, i))],
        out_specs=pl.BlockSpec((1, SUBLANES, tm), lambda h, i: (h, 0, i)),
        out_shape=jax.ShapeDtypeStruct((PEER_HEADS, SUBLANES, T), F32),
        compiler_params=_cparams(("parallel", "parallel")),
        name="peer_thresholds",
    )(st)


PEER_TL = 512
PEER_ET = 512


def _peer_mix_kernel(x_ref, st_ref, stats_ref, u_ref, vt_ref, g_ref, b_ref, o_ref,
                     xb_ref, s1_ref, e1_ref, e2_ref, acc_ref, *, alpha):
    j = pl.program_id(1)

    @pl.when(j == 0)
    def _():
        xb_ref[...] = x_ref[...].astype(BF16)
        for h in range(PEER_HEADS):
            st = stats_ref[h]
            s1 = st_ref[2 * h]
            s1_ref[h] = s1
            e1_ref[h] = jnp.exp(s1 - st[1:2]) * st[3:4]
            e2_ref[h] = jnp.exp(st_ref[2 * h + 1] - st[2:3])
        acc_ref[...] = jnp.zeros(acc_ref.shape, F32)

    su = _dot_t(u_ref[...], xb_ref[...])
    w_parts = []
    for a in range(PEER_ET // PEER_KEYS):
        i1 = j * (PEER_ET // PEER_KEYS) + a
        gate = jnp.zeros((PEER_KEYS, PEER_TL), F32)
        for h in range(PEER_HEADS):
            s1_row = s1_ref[h, pl.ds(i1, 1), :]
            c_row = e1_ref[h, pl.ds(i1, 1), :]
            thr = stats_ref[h, 0:1, :]
            picked = (st_ref[2 * h + 1] + s1_row) >= thr
            gate = gate + jnp.where(picked, e2_ref[h] * c_row, 0.0)
        sa = su[a * PEER_KEYS:(a + 1) * PEER_KEYS]
        act = 0.5 * sa * (1.0 + lax.erf(sa * (2.0 ** -0.5)))
        w_parts.append((gate * act).astype(BF16))
    w = jnp.concatenate(w_parts, axis=0)
    acc_ref[...] += _dot(vt_ref[...], w)

    @pl.when(j == pl.num_programs(1) - 1)
    def _():
        y = acc_ref[...].T
        o_ref[...] = _layer_norm(alpha * x_ref[...] + y, g_ref[...], b_ref[...])


def _peer_mix(x, st, stats, peer_u, peer_v, g, b, alpha):
    T, D = x.shape
    E = peer_u.shape[0]
    u = peer_u.astype(BF16)
    vt = peer_v.astype(BF16).T
    g2, b2 = g.reshape(1, D), b.reshape(1, D)
    tl, et = PEER_TL, PEER_ET
    return pl.pallas_call(
        functools.partial(_peer_mix_kernel, alpha=alpha),
        grid=(T // tl, E // et),
        in_specs=[pl.BlockSpec((tl, D), lambda i, j: (i, 0)),
                  pl.BlockSpec((2 * PEER_HEADS, PEER_KEYS, tl), lambda i, j: (0, 0, i)),
                  pl.BlockSpec((PEER_HEADS, SUBLANES, tl), lambda i, j: (0, 0, i)),
                  pl.BlockSpec((et, D), lambda i, j: (j, 0)),
                  pl.BlockSpec((D, et), lambda i, j: (0, j)),
                  pl.BlockSpec((1, D), lambda i, j: (0, 0)),
                  pl.BlockSpec((1, D), lambda i, j: (0, 0))],
        out_specs=pl.BlockSpec((tl, D), lambda i, j: (i, 0)),
        out_shape=jax.ShapeDtypeStruct((T, D), F32),
        scratch_shapes=[pltpu.VMEM((tl, D), BF16),
                        pltpu.VMEM((PEER_HEADS, PEER_KEYS, tl), F32),
                        pltpu.VMEM((PEER_HEADS, PEER_KEYS, tl), F32),
                        pltpu.VMEM((PEER_HEADS, PEER_KEYS, tl), F32),
                        pltpu.VMEM((D, tl), F32)],
        compiler_params=_cparams(("parallel", "arbitrary")),
        name="peer_mix",
    )(x, st, stats, u, vt, g2, b2)


def _pack_w_in(w):
    sizes = (512, 512, 128, 256, 64, 4, 256, 256, 256, 4, 4)
    offs = np.concatenate([[0], np.cumsum(sizes)])
    seg = lambda n: w[:, int(offs[n]):int(offs[n + 1])]
    a_in, q_b, c_b, qi, ki, wi, xc, v_m, o_m, i_m, f_m = (seg(n) for n in range(len(sizes)))
    z = lambda n: jnp.zeros((w.shape[0], n), w.dtype)
    main = jnp.concatenate([a_in, q_b, xc, v_m, o_m, c_b], axis=1)
    idx = jnp.concatenate([qi, ki, wi, z(4), i_m, z(4), f_m, z(IDX_W - 256 - SM_F - 4)], axis=1)
    return main, idx


def kernel(x, mem, ln_in_g, ln_in_b, rel_bias, w_in, conv_a_w, conv_a_b, norm_a_g, norm_a_b,
           kv_norm_g, w_uk, w_uv, conv_m_w, conv_m_b, w_qm, w_km, b_i, b_f, norm_m_g, w_out,
           ln1_g, ln1_b, w_cq, w_ckv, w_co, ln2_g, ln2_b, w_pq, sub_k1, sub_k2, peer_u, peer_v,
           ln3_g, ln3_b):
    B, L, D = x.shape
    T = B * L
    depth = w_in.shape[0]
    alpha = (2.0 * depth) ** 0.25
    xs = _entry_ln(x.reshape(T, D), ln_in_g, ln_in_b)
    mem2 = mem.reshape(-1, D)
    for l in range(depth):
        w_main, w_idx = _pack_w_in(w_in[l])
        h_main = _matmul(xs, w_main, 1, 512, 640, "proj_main")
        h_idx = _matmul(xs, w_idx, 3, 512, IDX_W, "proj_idx")
        y_a = _conv_group(h_main, conv_a_w[l], conv_a_b[l], norm_a_g[l], norm_a_b[l], B, L)
        ckv, kp = _dsa_prep(h_main, h_idx, kv_norm_g[l])
        y_b = _dsa_attention(h_main, h_idx, ckv, kp, w_uk[l], w_uv[l], rel_bias, B, L)
        q_m, k_m = _mlstm_prep(h_main, conv_m_w[l], conv_m_b[l], w_qm[l], w_km[l], B, L)
        y_c = _mlstm(q_m, k_m, h_main, h_idx, b_i[l], b_f[l], norm_m_g[l], B, L)
        xs = _mix_out(y_a, y_b, y_c, xs, w_out[l], ln1_g[l], ln1_b[l], alpha)

        kv = _matmul(mem2, w_ckv[l], 1, mem2.shape[0], 512, "xattn_kv")
        xs = _xattn(xs, kv, w_cq[l], w_co[l], ln2_g[l], ln2_b[l], alpha, B, L)

        st = _peer_scores(xs, w_pq[l], sub_k1[l], sub_k2[l])
        stats = _peer_thresholds(st)
        xs = _peer_mix(xs, st, stats, peer_u[l], peer_v[l], ln3_g[l], ln3_b[l], alpha)
    return xs.reshape(B, L, D)
```

```python
import functools
import math

import numpy as np
import jax
import jax.numpy as jnp
from jax import lax
from jax.experimental import pallas as pl
from jax.experimental.pallas import tpu as pltpu

F32 = jnp.float32
BF16 = jnp.bfloat16
I32 = jnp.int32

D_MODEL = 1024
CONV_CH = 256
CONV_WIDTH = 31
DSA_HEADS = 8
DSA_HEAD_DIM = 64
DSA_WIDTH = 512
KV_RANK = 128
IDX_HEADS = 4
IDX_DIM = 64
DSA_TOPK = 256
MLSTM_HEADS = 4
MLSTM_HEAD_DIM = 64
MLSTM_WIDTH = 256
MLSTM_CONV = 4
MLSTM_CHUNK = 64
REL_BUCKETS = 32
REL_MAX_DIST = 128
XATTN_HEADS = 4
XATTN_HEAD_DIM = 256
PEER_HEADS = 8
PEER_KEYS = 128
PEER_QDIM = 256
PEER_TOPK = 16
LN_EPS = 1e-5

LANES = 128
SUBLANES = 8
VMEM_LIMIT = 56 * 1024 * 1024

NEG_INF = float("-inf")
INT_MIN = -(2 ** 31)

MAIN_W = 1920
IDX_W = 384
SM_WI = 64
SM_I = 72
SM_F = 80


def _cparams(sem):
    return pltpu.CompilerParams(dimension_semantics=sem, vmem_limit_bytes=VMEM_LIMIT)


def _dot(a, b):
    return jnp.dot(a, b, preferred_element_type=F32)


def _dot_t(a, b):
    return lax.dot_general(a, b, (((1,), (1,)), ((), ())), preferred_element_type=F32)


def _dot_tl(a, b):
    return lax.dot_general(a, b, (((0,), (0,)), ((), ())), preferred_element_type=F32)


def _split(a):
    hi = a.astype(BF16)
    lo = (a - hi.astype(F32)).astype(BF16)
    return hi, lo


def _dot3(a, b_hi, b_lo, dot=_dot):
    a_hi, a_lo = _split(a)
    return dot(a_hi, b_hi) + dot(a_lo, b_hi) + dot(a_hi, b_lo)


def _layer_norm(x, g, b):
    mu = jnp.mean(x, axis=-1, keepdims=True)
    xc = x - mu
    var = jnp.mean(xc * xc, axis=-1, keepdims=True)
    return xc * lax.rsqrt(var + LN_EPS) * g + b


def _wsplit(w):
    hi = w.astype(BF16)
    lo = (w - hi.astype(F32)).astype(BF16)
    return hi, lo


def _ln_kernel(x_ref, g_ref, b_ref, o_ref):
    o_ref[...] = _layer_norm(x_ref[...], g_ref[...], b_ref[...])


def _entry_ln(x, g, b, tm=512):
    T, D = x.shape
    return pl.pallas_call(
        _ln_kernel,
        grid=(T // tm,),
        in_specs=[pl.BlockSpec((tm, D), lambda i: (i, 0)),
                  pl.BlockSpec((1, D), lambda i: (0, 0)),
                  pl.BlockSpec((1, D), lambda i: (0, 0))],
        out_specs=pl.BlockSpec((tm, D), lambda i: (i, 0)),
        out_shape=jax.ShapeDtypeStruct((T, D), F32),
        compiler_params=_cparams(("parallel",)),
        name="entry_ln",
    )(x, g.reshape(1, D), b.reshape(1, D))


def _mm1_kernel(x_ref, w_ref, o_ref):
    o_ref[...] = _dot(x_ref[...].astype(BF16), w_ref[...])


def _mm3_kernel(x_ref, wh_ref, wl_ref, o_ref):
    o_ref[...] = _dot3(x_ref[...], wh_ref[...], wl_ref[...])


def _matmul(x, w, passes, tm, tn, name):
    T, K = x.shape
    N = w.shape[1]
    x_spec = pl.BlockSpec((tm, K), lambda j, i: (i, 0))
    w_spec = pl.BlockSpec((K, tn), lambda j, i: (0, j))
    if passes == 1:
        kern, ws, w_specs = _mm1_kernel, (w.astype(BF16),), [w_spec]
    else:
        kern, ws, w_specs = _mm3_kernel, _wsplit(w), [w_spec, w_spec]
    return pl.pallas_call(
        kern,
        grid=(N // tn, T // tm),
        in_specs=[x_spec] + w_specs,
        out_specs=pl.BlockSpec((tm, tn), lambda j, i: (i, j)),
        out_shape=jax.ShapeDtypeStruct((T, N), F32),
        compiler_params=_cparams(("parallel", "parallel")),
        name=name,
    )(x, *ws)


CONV_HALO = 32
CONV_ROWS = 64


def _conv_kernel(cur_ref, halo_ref, w_ref, b_ref, g_ref, bb_ref, o_ref, ubuf, *, tl):
    i = pl.program_id(1)
    cur = cur_ref[...]
    ubuf[CONV_HALO:CONV_HALO + tl, :] = cur[:, :CONV_CH] * jax.nn.sigmoid(cur[:, CONV_CH:])
    hal = halo_ref[...]
    uh = hal[:, :CONV_CH] * jax.nn.sigmoid(hal[:, CONV_CH:])
    ubuf[0:CONV_HALO, :] = jnp.where(i > 0, uh, 0.0)
    base = CONV_HALO - (CONV_WIDTH - 1)
    for c in range(tl // CONV_ROWS):
        r0 = c * CONV_ROWS
        acc = jnp.broadcast_to(b_ref[...], (CONV_ROWS, CONV_CH))
        for k in range(CONV_WIDTH):
            acc = acc + w_ref[k:k + 1, :] * ubuf[r0 + base + k:r0 + base + k + CONV_ROWS, :]
        y = _layer_norm(acc, g_ref[...], bb_ref[...])
        o_ref[r0:r0 + CONV_ROWS, :] = y * jax.nn.sigmoid(y)


def _conv_group(h_main, conv_w, conv_b, ln_g, ln_b, B, L, tl=256):
    T = B * L
    nl = L // tl
    hb = tl // CONV_HALO
    return pl.pallas_call(
        functools.partial(_conv_kernel, tl=tl),
        grid=(B, nl),
        in_specs=[
            pl.BlockSpec((tl, 2 * CONV_CH), lambda b, i: (b * nl + i, 0)),
            pl.BlockSpec((CONV_HALO, 2 * CONV_CH),
                         lambda b, i: (jnp.maximum((b * nl + i) * hb - 1, 0), 0)),
            pl.BlockSpec((CONV_WIDTH, CONV_CH), lambda b, i: (0, 0)),
            pl.BlockSpec((1, CONV_CH), lambda b, i: (0, 0)),
            pl.BlockSpec((1, CONV_CH), lambda b, i: (0, 0)),
            pl.BlockSpec((1, CONV_CH), lambda b, i: (0, 0)),
        ],
        out_specs=pl.BlockSpec((tl, CONV_CH), lambda b, i: (b * nl + i, 0)),
        out_shape=jax.ShapeDtypeStruct((T, CONV_CH), F32),
        scratch_shapes=[pltpu.VMEM((CONV_HALO + tl, CONV_CH), F32)],
        compiler_params=_cparams(("parallel", "parallel")),
        name="conv_group",
    )(h_main, h_main, conv_w, conv_b.reshape(1, -1), ln_g.reshape(1, -1), ln_b.reshape(1, -1))


def _dsa_prep_kernel(c_ref, sm_ref, g_ref, ckv_ref, kp_ref):
    c = c_ref[...]
    ms = jnp.mean(c * c, axis=-1, keepdims=True)
    ckv_ref[...] = (c * lax.rsqrt(ms + LN_EPS) * g_ref[...]).astype(BF16)
    k_hi, k_lo = _split(sm_ref[...][:, :IDX_DIM])
    kp_ref[...] = jnp.concatenate([k_hi, k_hi, k_lo, jnp.zeros_like(k_hi)], axis=-1)


def _dsa_prep(h_main, h_idx, kv_g, tm=512):
    T = h_main.shape[0]
    return pl.pallas_call(
        _dsa_prep_kernel,
        grid=(T // tm,),
        in_specs=[pl.BlockSpec((tm, KV_RANK), lambda i: (i, 14)),
                  pl.BlockSpec((tm, LANES), lambda i: (i, 2)),
                  pl.BlockSpec((1, KV_RANK), lambda i: (0, 0))],
        out_specs=[pl.BlockSpec((tm, KV_RANK), lambda i: (i, 0)),
                   pl.BlockSpec((tm, 4 * IDX_DIM), lambda i: (i, 0))],
        out_shape=[jax.ShapeDtypeStruct((T, KV_RANK), BF16),
                   jax.ShapeDtypeStruct((T, 4 * IDX_DIM), BF16)],
        compiler_params=_cparams(("parallel",)),
        name="dsa_prep",
    )(h_main, h_idx, kv_g.reshape(1, -1))


TQ = 128
SCORE_COLS = 512


I16 = jnp.int16
I16_MIN = -(2 ** 15)


def _dsa_kernel(qb_ref, qi_ref, sm_ref, kp_ref, ckv_ref, wuk_ref, wuv_ref, bias_ref, tri_ref, o_ref,
                keys_ref, hi_ref, lo_ref, selb_ref, ql_ref, m_ref, l_ref, acc_ref, *, k_sel):
    qt = pl.program_id(1)
    nb = SCORE_COLS // LANES
    c_diag = qt // nb
    n_chunks = c_diag + 1
    row_t = qt * TQ + lax.broadcasted_iota(I32, (TQ, LANES), 0)
    lane_c = lax.broadcasted_iota(I32, (TQ, LANES), 1)

    qi = qi_ref[...]
    sm = sm_ref[...]
    qp = []
    for h in range(IDX_HEADS):
        q_hi, q_lo = _split(qi[:, h * IDX_DIM:(h + 1) * IDX_DIM])
        qp.append(jnp.concatenate([q_hi, q_lo, q_hi, jnp.zeros_like(q_hi)], axis=-1))
    qp = jnp.concatenate(qp, axis=0)
    w_fold = (IDX_DIM ** -0.5) * (IDX_HEADS ** -0.5)
    ws = [jnp.broadcast_to(sm[:, SM_WI + h:SM_WI + h + 1] * w_fold, (TQ, LANES))
          for h in range(IDX_HEADS)]

    def score_chunk(c, carry):
        c0 = pl.multiple_of(c * SCORE_COLS, SCORE_COLS)
        d = _dot_t(qp, kp_ref[pl.ds(c0, SCORE_COLS), :])
        for a in range(nb):
            off = c0 + a * LANES
            s = jnp.zeros((TQ, LANES), F32)
            for h in range(IDX_HEADS):
                s = s + jnp.maximum(d[h * TQ:(h + 1) * TQ, a * LANES:(a + 1) * LANES], 0.0) * ws[h]
            s = jnp.where(off + lane_c <= row_t, s + 0.0, NEG_INF)
            bits = lax.bitcast_convert_type(s, I32)
            key = bits ^ ((bits >> 31) & 0x7FFFFFFF)
            keys_ref[:, pl.ds(off, LANES)] = key
            hi_ref[:, pl.ds(off, LANES)] = (key >> 16).astype(I16)
            lo_ref[:, pl.ds(off, LANES)] = ((key & 0xFFFF) + I16_MIN).astype(I16)
        return carry

    lax.fori_loop(0, n_chunks, score_chunk, 0)

    def count16(ref, pred):
        def body(c, acc):
            c0 = pl.multiple_of(c * SCORE_COLS, SCORE_COLS)
            for a in range(nb):
                blk = ref[:, pl.ds(c0 + a * LANES, LANES)]
                acc = acc + jnp.where(pred(blk), jnp.int16(1), jnp.int16(0))
            return acc
        acc = lax.fori_loop(0, n_chunks, body, jnp.zeros((TQ, LANES), I16))
        tot = jnp.sum(acc.astype(I32).astype(F32), axis=-1, keepdims=True)
        return jnp.broadcast_to(tot, (TQ, LANES))

    def search16(ref, k_need):
        c_nonneg = count16(ref, lambda blk: blk >= jnp.int16(0))
        th0 = jnp.where(c_nonneg >= k_need, 0, I16_MIN).astype(I32)

        def bit_step(it, th):
            cand = th | (jnp.int32(1) << (14 - it))
            cand16 = cand.astype(I16)
            cnt = count16(ref, lambda blk: blk >= cand16)
            return jnp.where(cnt >= k_need, cand, th)

        return lax.fori_loop(0, 15, bit_step, th0)

    k_full = jnp.full((TQ, LANES), float(k_sel), F32)
    th_hi = search16(hi_ref, k_full)
    th_hi16 = th_hi.astype(I16)
    k_lo = k_full - count16(hi_ref, lambda blk: blk > th_hi16)

    def bucket_chunk(c, carry):
        c0 = pl.multiple_of(c * SCORE_COLS, SCORE_COLS)
        for a in range(nb):
            sl = pl.ds(c0 + a * LANES, LANES)
            lo_ref[:, sl] = jnp.where(hi_ref[:, sl] == th_hi16, lo_ref[:, sl], jnp.int16(I16_MIN))
        return carry

    lax.fori_loop(0, n_chunks, bucket_chunk, 0)
    th_lo = search16(lo_ref, k_lo)
    th_lo16 = th_lo.astype(I16)
    need = k_lo - count16(lo_ref, lambda blk: blk > th_lo16)
    theta = (th_hi << 16) | ((th_lo - I16_MIN) & 0xFFFF)

    def sel_chunk(c, carry):
        c0 = pl.multiple_of(c * SCORE_COLS, SCORE_COLS)
        keys = [keys_ref[:, pl.ds(c0 + a * LANES, LANES)] for a in range(nb)]
        tie = jnp.concatenate([jnp.where(k == theta, 1.0, 0.0) for k in keys], axis=1).astype(BF16)
        pc = _dot(tie, tri_ref[...])
        for a in range(nb):
            off = c0 + a * LANES
            rank = pc[:, a * LANES:(a + 1) * LANES] + carry
            tied_in = jnp.where(keys[a] == theta, jnp.where(rank <= need, 0.0, NEG_INF), NEG_INF)
            sb = jnp.where(keys[a] > theta, 0.0, tied_in)
            selb_ref[:, pl.ds(off, LANES)] = jnp.where(off + lane_c <= row_t, sb, NEG_INF)
        return carry + pc[:, SCORE_COLS:SCORE_COLS + LANES]

    lax.fori_loop(0, n_chunks, sel_chunk, jnp.zeros((TQ, LANES), F32))

    qb = qb_ref[...]
    scale = DSA_HEAD_DIM ** -0.5
    for h in range(DSA_HEADS):
        qh = qb[:, h * DSA_HEAD_DIM:(h + 1) * DSA_HEAD_DIM].astype(BF16)
        ql_ref[h * TQ:(h + 1) * TQ, :] = (_dot(qh, wuk_ref[h]) * scale).astype(BF16)

    m_ref[...] = jnp.full(m_ref.shape, -1e30, F32)
    l_ref[...] = jnp.zeros(l_ref.shape, F32)
    acc_ref[...] = jnp.zeros(acc_ref.shape, F32)

    def attend(c, near):
        c0 = pl.multiple_of(c * SCORE_COLS, SCORE_COLS)
        ck = ckv_ref[pl.ds(c0, SCORE_COLS), :]
        lg = _dot_t(ql_ref[...], ck)
        for h in range(DSA_HEADS):
            parts = []
            for a in range(nb):
                s = (lg[h * TQ:(h + 1) * TQ, a * LANES:(a + 1) * LANES]
                     + selb_ref[:, pl.ds(c0 + a * LANES, LANES)])
                if near:
                    seg = jnp.clip(c * nb + a - qt + 2, 0, 3)
                    s = s + bias_ref[h, :, pl.ds(pl.multiple_of(seg * LANES, LANES), LANES)]
                parts.append(s)
            blk_max = jnp.maximum(jnp.maximum(parts[0], parts[1]), jnp.maximum(parts[2], parts[3]))
            m_old = m_ref[h]
            m_new = jnp.maximum(m_old, jnp.broadcast_to(jnp.max(blk_max, axis=-1, keepdims=True),
                                                        (TQ, LANES)))
            alpha = jnp.exp(m_old - m_new)
            ps = [jnp.exp(s - m_new) for s in parts]
            l_ref[h] = alpha * l_ref[h] + ((ps[0] + ps[1]) + (ps[2] + ps[3]))
            pb = jnp.concatenate([p.astype(BF16) for p in ps], axis=1)
            acc_ref[h] = alpha * acc_ref[h] + _dot(pb, ck)
            m_ref[h] = m_new

    def far(c, carry):
        attend(c, False)
        return carry

    lax.fori_loop(0, jnp.maximum(c_diag - 1, 0), far, 0)

    @pl.when(c_diag > 0)
    def _():
        attend(c_diag - 1, True)

    attend(c_diag, True)

    out = jnp.zeros((TQ, DSA_WIDTH), F32)
    for h in range(DSA_HEADS):
        o_lat = acc_ref[h] / jnp.sum(l_ref[h], axis=-1, keepdims=True)
        out = out + _dot(o_lat.astype(BF16), wuv_ref[h])
    o_ref[...] = out


def _rel_bucket_table(n):
    max_exact = REL_BUCKETS // 2
    d = np.arange(n)
    df = np.maximum(d, 1).astype(np.float32)
    large = max_exact + (np.log(df / np.float32(max_exact)) / np.float32(math.log(REL_MAX_DIST / max_exact))
                         * np.float32(REL_BUCKETS - max_exact)).astype(np.int32)
    large = np.minimum(large, REL_BUCKETS - 1)
    return np.where(d < max_exact, d, large)


def _dsa_attention(h_main, h_idx, ckv, kp, w_uk, w_uv, rel_bias, B, L):
    T = B * L
    nq = L // TQ
    k_sel = min(DSA_TOPK, L // 4)
    assert SCORE_COLS == 4 * LANES and SCORE_COLS >= k_sel
    dist = TQ + np.arange(TQ)[:, None] - np.arange(2 * TQ)[None, :]
    assert _rel_bucket_table(REL_MAX_DIST * 4)[TQ:].min() == REL_BUCKETS - 1
    bucket = _rel_bucket_table(2 * TQ + 1)[np.maximum(dist, 0)]
    rb = rel_bias.astype(F32)
    bias_near = jnp.transpose(rb[bucket] - rb[REL_BUCKETS - 1], (2, 0, 1))
    bias_near = jnp.pad(bias_near, ((0, 0), (0, 0), (TQ, TQ)))
    u = np.arange(SCORE_COLS)[:, None]
    v = np.arange(SCORE_COLS + LANES)[None, :]
    tri = jnp.asarray((u <= v) | (v >= SCORE_COLS), BF16)
    wuv_band = jnp.zeros((DSA_HEADS, KV_RANK, DSA_WIDTH), F32)
    for h in range(DSA_HEADS):
        wuv_band = wuv_band.at[h, :, h * DSA_HEAD_DIM:(h + 1) * DSA_HEAD_DIM].set(w_uv[h])
    kern = functools.partial(_dsa_kernel, k_sel=k_sel)
    lpad = ((L + SCORE_COLS - 1) // SCORE_COLS) * SCORE_COLS
    return pl.pallas_call(
        kern,
        grid=(B, nq),
        in_specs=[
            pl.BlockSpec((TQ, DSA_WIDTH), lambda b, i: (b * nq + i, 1)),
            pl.BlockSpec((TQ, IDX_HEADS * IDX_DIM), lambda b, i: (b * nq + i, 0)),
            pl.BlockSpec((TQ, LANES), lambda b, i: (b * nq + i, 2)),
            pl.BlockSpec((L, 4 * IDX_DIM), lambda b, i: (b, 0)),
            pl.BlockSpec((L, KV_RANK), lambda b, i: (b, 0)),
            pl.BlockSpec((DSA_HEADS, DSA_HEAD_DIM, KV_RANK), lambda b, i: (0, 0, 0)),
            pl.BlockSpec((DSA_HEADS, KV_RANK, DSA_WIDTH), lambda b, i: (0, 0, 0)),
            pl.BlockSpec((DSA_HEADS, TQ, 4 * TQ), lambda b, i: (0, 0, 0)),
            pl.BlockSpec((SCORE_COLS, SCORE_COLS + LANES), lambda b, i: (0, 0)),
        ],
        out_specs=pl.BlockSpec((TQ, DSA_WIDTH), lambda b, i: (b * nq + i, 0)),
        out_shape=jax.ShapeDtypeStruct((T, DSA_WIDTH), F32),
        scratch_shapes=[
            pltpu.VMEM((TQ, lpad), I32),
            pltpu.VMEM((TQ, lpad), I16),
            pltpu.VMEM((TQ, lpad), I16),
            pltpu.VMEM((TQ, lpad), F32),
            pltpu.VMEM((DSA_HEADS * TQ, KV_RANK), BF16),
            pltpu.VMEM((DSA_HEADS, TQ, LANES), F32),
            pltpu.VMEM((DSA_HEADS, TQ, LANES), F32),
            pltpu.VMEM((DSA_HEADS, TQ, KV_RANK), F32),
        ],
        compiler_params=_cparams(("parallel", "arbitrary")),
        name="dsa_attention",
    )(h_main, h_idx, h_idx, kp, ckv, w_uk.astype(BF16), wuv_band.astype(BF16), bias_near, tri)


def _mlstm_prep_kernel(cur_ref, halo_ref, w_ref, b_ref, wqh_ref, wql_ref, wkh_ref, wkl_ref,
                       q_ref, k_ref, xbuf, *, tl):
    i = pl.program_id(1)
    xbuf[SUBLANES:SUBLANES + tl, :] = cur_ref[...]
    xbuf[0:SUBLANES, :] = jnp.where(i > 0, halo_ref[...], 0.0)
    base = SUBLANES - (MLSTM_CONV - 1)
    acc = jnp.broadcast_to(b_ref[...], (tl, MLSTM_WIDTH))
    for k in range(MLSTM_CONV):
        acc = acc + w_ref[k:k + 1, :] * xbuf[base + k:base + k + tl, :]
    xc = acc * jax.nn.sigmoid(acc)
    q_ref[...] = _dot3(xc, wqh_ref[...], wql_ref[...]) * (MLSTM_HEAD_DIM ** -0.5)
    k_ref[...] = _dot3(xc, wkh_ref[...], wkl_ref[...])


def _block_diag(w):
    h, d, e = w.shape
    out = jnp.zeros((h * d, h * e), F32)
    for i in range(h):
        out = out.at[i * d:(i + 1) * d, i * e:(i + 1) * e].set(w[i])
    return out


def _mlstm_prep(h_main, conv_w, conv_b, w_qm, w_km, B, L, tl=512):
    T = B * L
    nl = L // tl
    hb = tl // SUBLANES
    wq = _wsplit(_block_diag(w_qm))
    wk = _wsplit(_block_diag(w_km))
    full = lambda shape: pl.BlockSpec(shape, lambda b, i: (0,) * len(shape))
    return pl.pallas_call(
        functools.partial(_mlstm_prep_kernel, tl=tl),
        grid=(B, nl),
        in_specs=[
            pl.BlockSpec((tl, MLSTM_WIDTH), lambda b, i: (b * nl + i, 4)),
            pl.BlockSpec((SUBLANES, MLSTM_WIDTH),
                         lambda b, i: (jnp.maximum((b * nl + i) * hb - 1, 0), 4)),
            full((MLSTM_CONV, MLSTM_WIDTH)), full((1, MLSTM_WIDTH)),
            full((MLSTM_WIDTH, MLSTM_WIDTH)), full((MLSTM_WIDTH, MLSTM_WIDTH)),
            full((MLSTM_WIDTH, MLSTM_WIDTH)), full((MLSTM_WIDTH, MLSTM_WIDTH)),
        ],
        out_specs=[pl.BlockSpec((tl, MLSTM_WIDTH), lambda b, i: (b * nl + i, 0)),
                   pl.BlockSpec((tl, MLSTM_WIDTH), lambda b, i: (b * nl + i, 0))],
        out_shape=[jax.ShapeDtypeStruct((T, MLSTM_WIDTH), F32),
                   jax.ShapeDtypeStruct((T, MLSTM_WIDTH), F32)],
        scratch_shapes=[pltpu.VMEM((SUBLANES + tl, MLSTM_WIDTH), F32)],
        compiler_params=_cparams(("parallel", "parallel")),
        name="mlstm_prep",
    )(h_main, h_main, conv_w, conv_b.reshape(1, -1), wq[0], wq[1], wk[0], wk[1])


ML_ROWS = 128


def _mlstm_kernel(q_ref, k_ref, v_ref, o_ref, sm_ref, bi_ref, bf_ref, g_ref, out_ref,
                  cm_ref, n_ref, m_ref):
    C = MLSTM_CHUNK
    dh = MLSTM_HEAD_DIM

    @pl.when(pl.program_id(1) == 0)
    def _():
        cm_ref[...] = jnp.zeros(cm_ref.shape, F32)
        n_ref[...] = jnp.zeros(n_ref.shape, F32)
        m_ref[...] = jnp.zeros(m_ref.shape, F32)

    sm_t = sm_ref[...].T
    ig = sm_t[SM_I:SM_I + SUBLANES] + bi_ref[...]
    fg = jax.nn.log_sigmoid(sm_t[SM_F:SM_F + SUBLANES] + bf_ref[...])
    lane = lax.broadcasted_iota(I32, (SUBLANES, ML_ROWS), 1) & (C - 1)
    bcum = fg
    s = 1
    while s < C:
        bcum = bcum + jnp.where(lane >= s, pltpu.roll(bcum, s, axis=1), 0.0)
        s *= 2
    cols = jnp.concatenate([bcum, ig, jnp.zeros((LANES - 2 * SUBLANES, ML_ROWS), F32)], axis=0).T
    tri = (lax.broadcasted_iota(I32, (C, C), 1) <= lax.broadcasted_iota(I32, (C, C), 0))

    q_all, k_all, v_all, o_all = q_ref[...], k_ref[...], v_ref[...], o_ref[...]
    g_all = g_ref[...]
    for c in range(ML_ROWS // C):
        r0 = c * C
        for h in range(MLSTM_HEADS):
            hs = slice(h * dh, (h + 1) * dh)
            qj = q_all[r0:r0 + C, hs]
            kj = k_all[r0:r0 + C, hs]
            vj = v_all[r0:r0 + C, hs]
            b_row = bcum[h:h + 1, r0:r0 + C]
            i_row = ig[h:h + 1, r0:r0 + C]
            b_col = cols[r0:r0 + C, h:h + 1]
            i_col = cols[r0:r0 + C, SUBLANES + h:SUBLANES + h + 1]
            m_prev = m_ref[h:h + 1, 0:1]
            n_prev = n_ref[h:h + 1, :]
            cm_prev = cm_ref[h]

            dm = jnp.where(tri, b_col - b_row + i_row, NEG_INF)
            inter = b_col + m_prev
            m_row = jnp.maximum(inter, jnp.max(dm, axis=-1, keepdims=True))
            w_inter = jnp.exp(inter - m_row)
            qb, kb, vb = qj.astype(BF16), kj.astype(BF16), vj.astype(BF16)
            sw = _dot_t(qb, kb) * jnp.exp(dm - m_row)
            num = _dot(sw.astype(BF16), vb) + w_inter * _dot(qb, cm_prev.astype(BF16))
            den = (jnp.sum(sw, axis=-1, keepdims=True)
                   + w_inter * jnp.sum(qj * n_prev, axis=-1, keepdims=True))
            hh = num / jnp.maximum(jnp.abs(den), jnp.exp(-m_row))

            b_last = b_row[:, C - 1:C]
            g_row = b_last - b_row + i_row
            g_col = b_last - b_col + i_col
            m_new = jnp.maximum(b_last + m_prev, jnp.max(g_row, axis=-1, keepdims=True))
            decay = jnp.exp(b_last + m_prev - m_new)
            kw = kj * jnp.exp(g_col - m_new)
            cm_ref[h] = decay * cm_prev + _dot_tl(kw.astype(BF16), vb)
            n_ref[h:h + 1, :] = decay * n_prev + jnp.sum(kw, axis=0, keepdims=True)
            m_ref[h:h + 1, :] = jnp.broadcast_to(m_new, (1, LANES))

            mu = jnp.mean(hh, axis=-1, keepdims=True)
            hc = hh - mu
            var = jnp.mean(hc * hc, axis=-1, keepdims=True)
            hn = hc * lax.rsqrt(var + LN_EPS) * g_all[:, hs]
            out_ref[r0:r0 + C, hs] = jax.nn.sigmoid(o_all[r0:r0 + C, hs]) * hn


def _mlstm(q, k, h_main, h_idx, b_i, b_f, norm_g, B, L):
    T = B * L
    nl = L // ML_ROWS
    pad8 = lambda v: jnp.pad(v.astype(F32), (0, SUBLANES - MLSTM_HEADS)).reshape(SUBLANES, 1)
    row = lambda col: pl.BlockSpec((ML_ROWS, MLSTM_WIDTH), lambda b, i: (b * nl + i, col))
    return pl.pallas_call(
        _mlstm_kernel,
        grid=(B, nl),
        in_specs=[row(0), row(0), row(5), row(6),
                  pl.BlockSpec((ML_ROWS, LANES), lambda b, i: (b * nl + i, 2)),
                  pl.BlockSpec((SUBLANES, 1), lambda b, i: (0, 0)),
                  pl.BlockSpec((SUBLANES, 1), lambda b, i: (0, 0)),
                  pl.BlockSpec((1, MLSTM_WIDTH), lambda b, i: (0, 0))],
        out_specs=row(0),
        out_shape=jax.ShapeDtypeStruct((T, MLSTM_WIDTH), F32),
        scratch_shapes=[pltpu.VMEM((MLSTM_HEADS, MLSTM_HEAD_DIM, MLSTM_HEAD_DIM), F32),
                        pltpu.VMEM((SUBLANES, MLSTM_HEAD_DIM), F32),
                        pltpu.VMEM((SUBLANES, LANES), F32)],
        compiler_params=_cparams(("parallel", "arbitrary")),
        name="mlstm_scan",
    )(q, k, h_main, h_main, h_idx, pad8(b_i), pad8(b_f), norm_g.reshape(1, -1))


def _mix_out_kernel(ya_ref, yb_ref, yc_ref, x_ref, wa_ref, wb_ref, wc_ref, g_ref, b_ref, o_ref, *,
                    alpha):
    y = (_dot(ya_ref[...].astype(BF16), wa_ref[...])
         + _dot(yb_ref[...].astype(BF16), wb_ref[...])
         + _dot(yc_ref[...].astype(BF16), wc_ref[...]))
    o_ref[...] = _layer_norm(alpha * x_ref[...] + y, g_ref[...], b_ref[...])


def _mix_out(y_a, y_b, y_c, x, w_out, g, b, alpha, tm=512):
    T, D = x.shape
    w = w_out.astype(BF16)
    wa, wb, wc = w[:CONV_CH], w[CONV_CH:CONV_CH + DSA_WIDTH], w[CONV_CH + DSA_WIDTH:]
    rows = lambda width: pl.BlockSpec((tm, width), lambda i: (i, 0))
    full = lambda a: pl.BlockSpec(a.shape, lambda i: (0, 0))
    g2, b2 = g.reshape(1, D), b.reshape(1, D)
    return pl.pallas_call(
        functools.partial(_mix_out_kernel, alpha=alpha),
        grid=(T // tm,),
        in_specs=[rows(CONV_CH), rows(DSA_WIDTH), rows(MLSTM_WIDTH), rows(D),
                  full(wa), full(wb), full(wc), full(g2), full(b2)],
        out_specs=rows(D),
        out_shape=jax.ShapeDtypeStruct((T, D), F32),
        compiler_params=_cparams(("parallel",)),
        name="mix_out",
    )(y_a, y_b, y_c, x, wa, wb, wc, g2, b2)


def _xattn_kernel(x_ref, kv_ref, wq_ref, wo_ref, g_ref, b_ref, o_ref, *, alpha):
    x = x_ref[...]
    q = _dot(x.astype(BF16), wq_ref[...])
    kv = kv_ref[...]
    scale = XATTN_HEAD_DIM ** -0.5
    outs = []
    for h in range(XATTN_HEADS):
        hs = slice(h * XATTN_HEAD_DIM, (h + 1) * XATTN_HEAD_DIM)
        kh = kv[:, hs].astype(BF16)
        vh = kv[:, D_MODEL + h * XATTN_HEAD_DIM:D_MODEL + (h + 1) * XATTN_HEAD_DIM].astype(BF16)
        lg = _dot_t(q[:, hs].astype(BF16), kh) * scale
        lg = lg - jnp.max(lg, axis=-1, keepdims=True)
        p = jnp.exp(lg)
        p = p / jnp.sum(p, axis=-1, keepdims=True)
        outs.append(_dot(p.astype(BF16), vh))
    o = jnp.concatenate(outs, axis=-1)
    y = _dot(o.astype(BF16), wo_ref[...])
    o_ref[...] = _layer_norm(alpha * x + y, g_ref[...], b_ref[...])


def _xattn(x, kv, w_q, w_o, g, b, alpha, B, L, tm=256):
    T, D = x.shape
    nl = L // tm
    M = kv.shape[0] // B
    full = lambda a: pl.BlockSpec(a.shape, lambda bb, i: (0, 0))
    g2, b2 = g.reshape(1, D), b.reshape(1, D)
    wq, wo = w_q.astype(BF16), w_o.astype(BF16)
    return pl.pallas_call(
        functools.partial(_xattn_kernel, alpha=alpha),
        grid=(B, nl),
        in_specs=[pl.BlockSpec((tm, D), lambda bb, i: (bb * nl + i, 0)),
                  pl.BlockSpec((M, 2 * D), lambda bb, i: (bb, 0)),
                  full(wq), full(wo), full(g2), full(b2)],
        out_specs=pl.BlockSpec((tm, D), lambda bb, i: (bb * nl + i, 0)),
        out_shape=jax.ShapeDtypeStruct((T, D), F32),
        compiler_params=_cparams(("parallel", "parallel")),
        name="xattn",
    )(x, kv, wq, wo, g2, b2)


def _peer_score_kernel(x_ref, wh_ref, wl_ref, k1h_ref, k1l_ref, k2h_ref, k2l_ref, st_ref):
    q = _dot3(x_ref[...], wh_ref[...], wl_ref[...])
    half = PEER_QDIM // 2
    for h in range(PEER_HEADS):
        for part, (kh_ref, kl_ref) in enumerate(((k1h_ref, k1l_ref), (k2h_ref, k2l_ref))):
            c0 = h * PEER_QDIM + part * half
            q_hi, q_lo = _split(q[:, c0:c0 + half])
            kh, kl = kh_ref[...], kl_ref[...]
            st_ref[2 * h + part] = _dot_t(kh, q_hi) + _dot_t(kl, q_hi) + _dot_t(kh, q_lo)


def _peer_scores(x, w_pq, sub_k1, sub_k2, tm=256):
    T, D = x.shape
    wh, wl = _wsplit(w_pq)
    k1h, k1l = _wsplit(sub_k1)
    k2h, k2l = _wsplit(sub_k2)
    full = lambda a: pl.BlockSpec(a.shape, lambda i: (0, 0))
    return pl.pallas_call(
        _peer_score_kernel,
        grid=(T // tm,),
        in_specs=[pl.BlockSpec((tm, D), lambda i: (i, 0)), full(wh), full(wl),
                  full(k1h), full(k1l), full(k2h), full(k2l)],
        out_specs=pl.BlockSpec((2 * PEER_HEADS, PEER_KEYS, tm), lambda i: (0, 0, i)),
        out_shape=jax.ShapeDtypeStruct((2 * PEER_HEADS, PEER_KEYS, T), F32),
        compiler_params=_cparams(("parallel",)),
        name="peer_scores",
    )(x, wh, wl, k1h, k1l, k2h, k2l)


PEER_NTOP = PEER_TOPK + 1
PEER_PAIR_ROWS = tuple(PEER_NTOP // (k + 1) for k in range(PEER_NTOP))
PEER_CAND_ROWS = -(-sum(PEER_PAIR_ROWS) // SUBLANES) * SUBLANES


def _peer_thr_kernel(st_ref, stats_ref, v2_ref, cand_ref):
    def top_rows(x):
        rows = []
        for _ in range(PEER_NTOP):
            m = jnp.max(x, axis=0, keepdims=True)
            rows.append(m)
            x = jnp.where(x == m, NEG_INF, x)
        return rows

    v1 = top_rows(st_ref[0])
    v2 = top_rows(st_ref[1])
    for k in range(PEER_NTOP):
        v2_ref[k:k + 1, :] = v2[k]
    r = 0
    for k, n in enumerate(PEER_PAIR_ROWS):
        cand_ref[r:r + n, :] = v1[k] + v2_ref[0:n, :]
        r += n
    cand_ref[r:PEER_CAND_ROWS, :] = jnp.full((PEER_CAND_ROWS - r, cand_ref.shape[1]), NEG_INF, F32)
    cand = cand_ref[...]
    x = cand
    for _ in range(PEER_TOPK - 1):
        m = jnp.max(x, axis=0, keepdims=True)
        x = jnp.where(x == m, NEG_INF, x)
    thr = jnp.max(x, axis=0, keepdims=True)
    nxt = jnp.max(jnp.where(x == thr, NEG_INF, x), axis=0, keepdims=True)
    top = v1[0] + v2[0]
    z = jnp.sum(jnp.where(cand >= thr, jnp.exp(cand - top), 0.0), axis=0, keepdims=True)
    cut = jnp.where(nxt > NEG_INF, 0.5 * thr + 0.5 * nxt, thr)
    pad = jnp.zeros((SUBLANES - 4, thr.shape[1]), F32)
    stats_ref[0] = jnp.concatenate([cut, v1[0], v2[0], 1.0 / z, pad], axis=0)


def _peer_thresholds(st, tm=256):
    T = st.shape[2]
    return pl.pallas_call(
        _peer_thr_kernel,
        grid=(PEER_HEADS, T // tm),
        in_specs=[pl.BlockSpec((2, PEER_KEYS, tm), lambda h, i: (h, 0, i))],
        out_specs=pl.BlockSpec((1, SUBLANES, tm), lambda h, i: (h, 0, i)),
        out_shape=jax.ShapeDtypeStruct((PEER_HEADS, SUBLANES, T), F32),
        scratch_shapes=[pltpu.VMEM((3 * SUBLANES, tm), F32),
                        pltpu.VMEM((PEER_CAND_ROWS, tm), F32)],
        compiler_params=_cparams(("parallel", "parallel")),
        name="peer_thresholds",
    )(st)


PEER_TL = 512
PEER_ET = 512


def _peer_mix_kernel(x_ref, st_ref, stats_ref, u_ref, vt_ref, g_ref, b_ref, o_ref,
                     xb_ref, d1_ref, e1_ref, e2_ref, su_ref, w_ref, acc_ref, *, alpha):
    j = pl.program_id(1)

    @pl.when(j == 0)
    def _():
        xb_ref[...] = x_ref[...].astype(BF16)
        for h in range(PEER_HEADS):
            st = stats_ref[h]
            s1 = st_ref[2 * h]
            d1_ref[h] = st[0:1] - s1
            e1_ref[h] = jnp.exp(s1 - st[1:2]) * st[3:4]
            e2_ref[h] = jnp.exp(st_ref[2 * h + 1] - st[2:3])
        acc_ref[...] = jnp.zeros(acc_ref.shape, F32)

    su_ref[...] = _dot_t(u_ref[...], xb_ref[...])
    for a in range(PEER_ET // PEER_KEYS):
        i1 = j * (PEER_ET // PEER_KEYS) + a
        rs = slice(a * PEER_KEYS, (a + 1) * PEER_KEYS)
        d_rows = [d1_ref[h, pl.ds(i1, 1), :] for h in range(PEER_HEADS)]
        c_rows = [e1_ref[h, pl.ds(i1, 1), :] for h in range(PEER_HEADS)]
        for lb in range(PEER_TL // LANES):
            ls = slice(lb * LANES, (lb + 1) * LANES)
            gate = jnp.zeros((PEER_KEYS, LANES), F32)
            for h in range(PEER_HEADS):
                picked = st_ref[2 * h + 1, :, ls] >= d_rows[h][:, ls]
                gate = gate + jnp.where(picked, e2_ref[h, :, ls] * c_rows[h][:, ls], 0.0)
            sa = su_ref[rs, ls]
            act = 0.5 * sa * (1.0 + lax.erf(sa * (2.0 ** -0.5)))
            w_ref[rs, ls] = (gate * act).astype(BF16)
    acc_ref[...] += _dot(vt_ref[...], w_ref[...])

    @pl.when(j == pl.num_programs(1) - 1)
    def _():
        y = acc_ref[...].T
        o_ref[...] = _layer_norm(alpha * x_ref[...] + y, g_ref[...], b_ref[...])


def _peer_mix(x, st, stats, peer_u, peer_v, g, b, alpha):
    T, D = x.shape
    E = peer_u.shape[0]
    u = peer_u.astype(BF16)
    vt = peer_v.astype(BF16).T
    g2, b2 = g.reshape(1, D), b.reshape(1, D)
    tl, et = PEER_TL, PEER_ET
    return pl.pallas_call(
        functools.partial(_peer_mix_kernel, alpha=alpha),
        grid=(T // tl, E // et),
        in_specs=[pl.BlockSpec((tl, D), lambda i, j: (i, 0)),
                  pl.BlockSpec((2 * PEER_HEADS, PEER_KEYS, tl), lambda i, j: (0, 0, i)),
                  pl.BlockSpec((PEER_HEADS, SUBLANES, tl), lambda i, j: (0, 0, i)),
                  pl.BlockSpec((et, D), lambda i, j: (j, 0)),
                  pl.BlockSpec((D, et), lambda i, j: (0, j)),
                  pl.BlockSpec((1, D), lambda i, j: (0, 0)),
                  pl.BlockSpec((1, D), lambda i, j: (0, 0))],
        out_specs=pl.BlockSpec((tl, D), lambda i, j: (i, 0)),
        out_shape=jax.ShapeDtypeStruct((T, D), F32),
        scratch_shapes=[pltpu.VMEM((tl, D), BF16),
                        pltpu.VMEM((PEER_HEADS, PEER_KEYS, tl), F32),
                        pltpu.VMEM((PEER_HEADS, PEER_KEYS, tl), F32),
                        pltpu.VMEM((PEER_HEADS, PEER_KEYS, tl), F32),
                        pltpu.VMEM((et, tl), F32),
                        pltpu.VMEM((et, tl), BF16),
                        pltpu.VMEM((D, tl), F32)],
        compiler_params=_cparams(("parallel", "arbitrary")),
        name="peer_mix",
    )(x, st, stats, u, vt, g2, b2)


def _pack_w_in(w):
    sizes = (512, 512, 128, 256, 64, 4, 256, 256, 256, 4, 4)
    offs = np.concatenate([[0], np.cumsum(sizes)])
    seg = lambda n: w[:, int(offs[n]):int(offs[n + 1])]
    a_in, q_b, c_b, qi, ki, wi, xc, v_m, o_m, i_m, f_m = (seg(n) for n in range(len(sizes)))
    z = lambda n: jnp.zeros((w.shape[0], n), w.dtype)
    main = jnp.concatenate([a_in, q_b, xc, v_m, o_m, c_b], axis=1)
    idx = jnp.concatenate([qi, ki, wi, z(4), i_m, z(4), f_m, z(IDX_W - 256 - SM_F - 4)], axis=1)
    return main, idx


def kernel(x, mem, ln_in_g, ln_in_b, rel_bias, w_in, conv_a_w, conv_a_b, norm_a_g, norm_a_b,
           kv_norm_g, w_uk, w_uv, conv_m_w, conv_m_b, w_qm, w_km, b_i, b_f, norm_m_g, w_out,
           ln1_g, ln1_b, w_cq, w_ckv, w_co, ln2_g, ln2_b, w_pq, sub_k1, sub_k2, peer_u, peer_v,
           ln3_g, ln3_b):
    B, L, D = x.shape
    T = B * L
    depth = w_in.shape[0]
    alpha = (2.0 * depth) ** 0.25
    xs = _entry_ln(x.reshape(T, D), ln_in_g, ln_in_b)
    mem2 = mem.reshape(-1, D)
    for l in range(depth):
        w_main, w_idx = _pack_w_in(w_in[l])
        h_main = _matmul(xs, w_main, 1, 512, 640, "proj_main")
        h_idx = _matmul(xs, w_idx, 3, 512, IDX_W, "proj_idx")
        y_a = _conv_group(h_main, conv_a_w[l], conv_a_b[l], norm_a_g[l], norm_a_b[l], B, L)
        ckv, kp = _dsa_prep(h_main, h_idx, kv_norm_g[l])
        y_b = _dsa_attention(h_main, h_idx, ckv, kp, w_uk[l], w_uv[l], rel_bias, B, L)
        q_m, k_m = _mlstm_prep(h_main, conv_m_w[l], conv_m_b[l], w_qm[l], w_km[l], B, L)
        y_c = _mlstm(q_m, k_m, h_main, h_idx, b_i[l], b_f[l], norm_m_g[l], B, L)
        xs = _mix_out(y_a, y_b, y_c, xs, w_out[l], ln1_g[l], ln1_b[l], alpha)

        kv = _matmul(mem2, w_ckv[l], 1, mem2.shape[0], 512, "xattn_kv")
        xs = _xattn(xs, kv, w_cq[l], w_co[l], ln2_g[l], ln2_b[l], alpha, B, L)

        st = _peer_scores(xs, w_pq[l], sub_k1[l], sub_k2[l])
        stats = _peer_thresholds(st)
        xs = _peer_mix(xs, st, stats, peer_u[l], peer_v[l], ln3_g[l], ln3_b[l], alpha)
    return xs.reshape(B, L, D)
```

```python
import functools
import math

import numpy as np
import jax
import jax.numpy as jnp
from jax import lax
from jax.experimental import pallas as pl
from jax.experimental.pallas import tpu as pltpu

F32 = jnp.float32
BF16 = jnp.bfloat16
I32 = jnp.int32

D_MODEL = 1024
CONV_CH = 256
CONV_WIDTH = 31
DSA_HEADS = 8
DSA_HEAD_DIM = 64
DSA_WIDTH = 512
KV_RANK = 128
IDX_HEADS = 4
IDX_DIM = 64
DSA_TOPK = 256
MLSTM_HEADS = 4
MLSTM_HEAD_DIM = 64
MLSTM_WIDTH = 256
MLSTM_CONV = 4
MLSTM_CHUNK = 64
REL_BUCKETS = 32
REL_MAX_DIST = 128
XATTN_HEADS = 4
XATTN_HEAD_DIM = 256
PEER_HEADS = 8
PEER_KEYS = 128
PEER_QDIM = 256
PEER_TOPK = 16
LN_EPS = 1e-5

LANES = 128
SUBLANES = 8
VMEM_LIMIT = 56 * 1024 * 1024

NEG_INF = float("-inf")
LOG2_E = 1.4426950408889634

MAIN_W = 1920
IDX_W = 384
SM_WI = 64
SM_I = 72
SM_F = 80


def _cparams(sem):
    return pltpu.CompilerParams(dimension_semantics=sem, vmem_limit_bytes=VMEM_LIMIT)


def _dot(a, b):
    return jnp.dot(a, b, preferred_element_type=F32)


def _dot_t(a, b):
    return lax.dot_general(a, b, (((1,), (1,)), ((), ())), preferred_element_type=F32)


def _dot_tl(a, b):
    return lax.dot_general(a, b, (((0,), (0,)), ((), ())), preferred_element_type=F32)


def _split(a):
    hi = a.astype(BF16)
    lo = (a - hi.astype(F32)).astype(BF16)
    return hi, lo


def _dot3(a, b_hi, b_lo, dot=_dot):
    a_hi, a_lo = _split(a)
    return dot(a_hi, b_hi) + dot(a_lo, b_hi) + dot(a_hi, b_lo)


def _layer_norm(x, g, b):
    mu = jnp.mean(x, axis=-1, keepdims=True)
    xc = x - mu
    var = jnp.mean(xc * xc, axis=-1, keepdims=True)
    return xc * lax.rsqrt(var + LN_EPS) * g + b


def _wsplit(w):
    hi = w.astype(BF16)
    lo = (w - hi.astype(F32)).astype(BF16)
    return hi, lo


def _ln_kernel(x_ref, g_ref, b_ref, o_ref):
    o_ref[...] = _layer_norm(x_ref[...], g_ref[...], b_ref[...])


def _entry_ln(x, g, b, tm=512):
    T, D = x.shape
    return pl.pallas_call(
        _ln_kernel,
        grid=(T // tm,),
        in_specs=[pl.BlockSpec((tm, D), lambda i: (i, 0)),
                  pl.BlockSpec((1, D), lambda i: (0, 0)),
                  pl.BlockSpec((1, D), lambda i: (0, 0))],
        out_specs=pl.BlockSpec((tm, D), lambda i: (i, 0)),
        out_shape=jax.ShapeDtypeStruct((T, D), F32),
        compiler_params=_cparams(("parallel",)),
        name="entry_ln",
    )(x, g.reshape(1, D), b.reshape(1, D))


def _mm1_kernel(x_ref, w_ref, o_ref):
    o_ref[...] = _dot(x_ref[...].astype(BF16), w_ref[...])


def _mm3_kernel(x_ref, wh_ref, wl_ref, o_ref):
    o_ref[...] = _dot3(x_ref[...], wh_ref[...], wl_ref[...])


def _matmul(x, w, passes, tm, tn, name):
    T, K = x.shape
    N = w.shape[1]
    x_spec = pl.BlockSpec((tm, K), lambda j, i: (i, 0))
    w_spec = pl.BlockSpec((K, tn), lambda j, i: (0, j))
    if passes == 1:
        kern, ws, w_specs = _mm1_kernel, (w.astype(BF16),), [w_spec]
    else:
        kern, ws, w_specs = _mm3_kernel, _wsplit(w), [w_spec, w_spec]
    return pl.pallas_call(
        kern,
        grid=(N // tn, T // tm),
        in_specs=[x_spec] + w_specs,
        out_specs=pl.BlockSpec((tm, tn), lambda j, i: (i, j)),
        out_shape=jax.ShapeDtypeStruct((T, N), F32),
        compiler_params=_cparams(("parallel", "parallel")),
        name=name,
    )(x, *ws)


CONV_HALO = 32
CONV_ROWS = 64


def _conv_kernel(cur_ref, halo_ref, w_ref, b_ref, g_ref, bb_ref, o_ref, ubuf, *, tl):
    i = pl.program_id(1)
    cur = cur_ref[...]
    ubuf[CONV_HALO:CONV_HALO + tl, :] = cur[:, :CONV_CH] * jax.nn.sigmoid(cur[:, CONV_CH:])
    hal = halo_ref[...]
    uh = hal[:, :CONV_CH] * jax.nn.sigmoid(hal[:, CONV_CH:])
    ubuf[0:CONV_HALO, :] = jnp.where(i > 0, uh, 0.0)
    base = CONV_HALO - (CONV_WIDTH - 1)
    for c in range(tl // CONV_ROWS):
        r0 = c * CONV_ROWS
        acc = jnp.broadcast_to(b_ref[...], (CONV_ROWS, CONV_CH))
        for k in range(CONV_WIDTH):
            acc = acc + w_ref[k:k + 1, :] * ubuf[r0 + base + k:r0 + base + k + CONV_ROWS, :]
        y = _layer_norm(acc, g_ref[...], bb_ref[...])
        o_ref[r0:r0 + CONV_ROWS, :] = y * jax.nn.sigmoid(y)


def _conv_group(h_main, conv_w, conv_b, ln_g, ln_b, B, L, tl=256):
    T = B * L
    nl = L // tl
    hb = tl // CONV_HALO
    return pl.pallas_call(
        functools.partial(_conv_kernel, tl=tl),
        grid=(B, nl),
        in_specs=[
            pl.BlockSpec((tl, 2 * CONV_CH), lambda b, i: (b * nl + i, 0)),
            pl.BlockSpec((CONV_HALO, 2 * CONV_CH),
                         lambda b, i: (jnp.maximum((b * nl + i) * hb - 1, 0), 0)),
            pl.BlockSpec((CONV_WIDTH, CONV_CH), lambda b, i: (0, 0)),
            pl.BlockSpec((1, CONV_CH), lambda b, i: (0, 0)),
            pl.BlockSpec((1, CONV_CH), lambda b, i: (0, 0)),
            pl.BlockSpec((1, CONV_CH), lambda b, i: (0, 0)),
        ],
        out_specs=pl.BlockSpec((tl, CONV_CH), lambda b, i: (b * nl + i, 0)),
        out_shape=jax.ShapeDtypeStruct((T, CONV_CH), F32),
        scratch_shapes=[pltpu.VMEM((CONV_HALO + tl, CONV_CH), F32)],
        compiler_params=_cparams(("parallel", "parallel")),
        name="conv_group",
    )(h_main, h_main, conv_w, conv_b.reshape(1, -1), ln_g.reshape(1, -1), ln_b.reshape(1, -1))


def _dsa_prep_kernel(c_ref, sm_ref, g_ref, ckv_ref, kp_ref):
    c = c_ref[...]
    ms = jnp.mean(c * c, axis=-1, keepdims=True)
    ckv = (c * lax.rsqrt(ms + LN_EPS) * g_ref[...]).astype(BF16)
    ckv_ref[...] = jnp.concatenate([ckv, jnp.ones_like(ckv)], axis=-1)
    k_hi, k_lo = _split(sm_ref[...][:, :IDX_DIM])
    kp_ref[...] = jnp.concatenate([k_hi, k_hi, k_lo, jnp.zeros_like(k_hi)], axis=-1)


def _dsa_prep(h_main, h_idx, kv_g, tm=512):
    T = h_main.shape[0]
    return pl.pallas_call(
        _dsa_prep_kernel,
        grid=(T // tm,),
        in_specs=[pl.BlockSpec((tm, KV_RANK), lambda i: (i, 14)),
                  pl.BlockSpec((tm, LANES), lambda i: (i, 2)),
                  pl.BlockSpec((1, KV_RANK), lambda i: (0, 0))],
        out_specs=[pl.BlockSpec((tm, 2 * KV_RANK), lambda i: (i, 0)),
                   pl.BlockSpec((tm, 4 * IDX_DIM), lambda i: (i, 0))],
        out_shape=[jax.ShapeDtypeStruct((T, 2 * KV_RANK), BF16),
                   jax.ShapeDtypeStruct((T, 4 * IDX_DIM), BF16)],
        compiler_params=_cparams(("parallel",)),
        name="dsa_prep",
    )(h_main, h_idx, kv_g.reshape(1, -1))


TQ = 128
SCORE_COLS = 512
PV_GROUP = 1


I16 = jnp.int16
I16_MIN = -(2 ** 15)


def _dsa_kernel(qb_ref, qi_ref, sm_ref, kp_ref, ckv_ref, wuk_ref, wuv_ref, bias_ref, tri_ref, o_ref,
                keys_ref, hi_ref, lo_ref, selb_ref, ql_ref, s_ref, p_ref, m_ref, al_ref, acc_ref, *,
                k_sel):
    qt = pl.program_id(1)
    nb = SCORE_COLS // LANES
    c_diag = qt // nb
    n_chunks = c_diag + 1
    n_pairs = (n_chunks + 1) // 2
    row_t = qt * TQ + lax.broadcasted_iota(I32, (TQ, LANES), 0)
    lane_c = lax.broadcasted_iota(I32, (TQ, LANES), 1)

    qi = qi_ref[...]
    sm = sm_ref[...]
    qp = []
    for h in range(IDX_HEADS):
        q_hi, q_lo = _split(qi[:, h * IDX_DIM:(h + 1) * IDX_DIM])
        qp.append(jnp.concatenate([q_hi, q_lo, q_hi, jnp.zeros_like(q_hi)], axis=-1))
    qp = jnp.concatenate(qp, axis=0)
    w_fold = (IDX_DIM ** -0.5) * (IDX_HEADS ** -0.5)
    ws = [jnp.broadcast_to(sm[:, SM_WI + h:SM_WI + h + 1] * w_fold, (TQ, LANES))
          for h in range(IDX_HEADS)]

    def score_pair(cp, carry):
        for half in range(2):
            c0 = pl.multiple_of((2 * cp + half) * SCORE_COLS, SCORE_COLS)
            d = _dot_t(qp, kp_ref[pl.ds(c0, SCORE_COLS), :])
            for a in range(nb):
                off = c0 + a * LANES
                s = jnp.zeros((TQ, LANES), F32)
                for h in range(IDX_HEADS):
                    s = s + jnp.maximum(d[h * TQ:(h + 1) * TQ, a * LANES:(a + 1) * LANES], 0.0) * ws[h]
                s = jnp.where(off + lane_c <= row_t, s + 0.0, NEG_INF)
                bits = lax.bitcast_convert_type(s, I32)
                key = bits ^ ((bits >> 31) & 0x7FFFFFFF)
                keys_ref[:, pl.ds(off, LANES)] = key
                hi_ref[:, pl.ds(off, LANES)] = (key >> 16).astype(I16)
                lo_ref[:, pl.ds(off, LANES)] = ((key & 0xFFFF) + I16_MIN).astype(I16)
        return carry

    lax.fori_loop(0, n_pairs, score_pair, 0)

    def count16(ref, pred):
        def body(c, acc):
            c0 = pl.multiple_of(c * 2 * SCORE_COLS, 2 * SCORE_COLS)
            for a in range(2 * nb):
                blk = ref[:, pl.ds(c0 + a * LANES, LANES)]
                acc = acc + jnp.where(pred(blk), jnp.int16(1), jnp.int16(0))
            return acc
        acc = lax.fori_loop(0, n_pairs, body, jnp.zeros((TQ, LANES), I16))
        tot = jnp.sum(acc.astype(I32).astype(F32), axis=-1, keepdims=True)
        return jnp.broadcast_to(tot, (TQ, LANES))

    def search16(ref, k_need):
        c_nonneg = count16(ref, lambda blk: blk >= jnp.int16(0))
        th0 = jnp.where(c_nonneg >= k_need, 0, I16_MIN).astype(I32)

        def bit_step(it, th):
            cand = th | (jnp.int32(1) << (14 - it))
            cand16 = cand.astype(I16)
            cnt = count16(ref, lambda blk: blk >= cand16)
            return jnp.where(cnt >= k_need, cand, th)

        return lax.fori_loop(0, 15, bit_step, th0)

    k_full = jnp.full((TQ, LANES), float(k_sel), F32)
    th_hi = search16(hi_ref, k_full)
    th_hi16 = th_hi.astype(I16)
    k_lo = k_full - count16(hi_ref, lambda blk: blk > th_hi16)

    def bucket_chunk(c, carry):
        c0 = pl.multiple_of(c * SCORE_COLS, SCORE_COLS)
        for a in range(nb):
            sl = pl.ds(c0 + a * LANES, LANES)
            lo_ref[:, sl] = jnp.where(hi_ref[:, sl] == th_hi16, lo_ref[:, sl], jnp.int16(I16_MIN))
        return carry

    lax.fori_loop(0, 2 * n_pairs, bucket_chunk, 0)
    th_lo = search16(lo_ref, k_lo)
    th_lo16 = th_lo.astype(I16)
    need = k_lo - count16(lo_ref, lambda blk: blk > th_lo16)
    theta = (th_hi << 16) | ((th_lo - I16_MIN) & 0xFFFF)

    def sel_pair(cp, carry):
        pcs = []
        for half in range(2):
            c0 = pl.multiple_of((2 * cp + half) * SCORE_COLS, SCORE_COLS)
            tie = jnp.concatenate(
                [jnp.where(keys_ref[:, pl.ds(c0 + a * LANES, LANES)] == theta, 1.0, 0.0)
                 for a in range(nb)], axis=1).astype(BF16)
            pcs.append(_dot(tie, tri_ref[...]))
        for half in range(2):
            c0 = pl.multiple_of((2 * cp + half) * SCORE_COLS, SCORE_COLS)
            for a in range(nb):
                off = c0 + a * LANES
                key = keys_ref[:, pl.ds(off, LANES)]
                rank = pcs[half][:, a * LANES:(a + 1) * LANES] + carry
                tied_in = jnp.where(key == theta, jnp.where(rank <= need, 0.0, NEG_INF), NEG_INF)
                sb = jnp.where(key > theta, 0.0, tied_in)
                selb_ref[:, pl.ds(off, LANES)] = jnp.where(off + lane_c <= row_t, sb, NEG_INF)
            carry = carry + pcs[half][:, SCORE_COLS:SCORE_COLS + LANES]
        return carry

    lax.fori_loop(0, n_pairs, sel_pair, jnp.zeros((TQ, LANES), F32))

    qb = qb_ref[...]
    scale = (DSA_HEAD_DIM ** -0.5) * LOG2_E
    for h in range(DSA_HEADS):
        qh = qb[:, h * DSA_HEAD_DIM:(h + 1) * DSA_HEAD_DIM].astype(BF16)
        ql_ref[h * TQ:(h + 1) * TQ, :] = (_dot(qh, wuk_ref[h]) * scale).astype(BF16)

    m_ref[...] = jnp.full(m_ref.shape, -1e30, F32)
    acc_ref[...] = jnp.zeros(acc_ref.shape, F32)

    def attend(c, near):
        c0 = pl.multiple_of(c * SCORE_COLS, SCORE_COLS)
        ckx = ckv_ref[pl.ds(c0, SCORE_COLS), :]
        ck = ckx[:, :KV_RANK]
        lg = _dot_t(ql_ref[...], ck)
        for h in range(DSA_HEADS):
            blk_max = None
            for a in range(nb):
                s = (lg[h * TQ:(h + 1) * TQ, a * LANES:(a + 1) * LANES]
                     + selb_ref[:, pl.ds(c0 + a * LANES, LANES)])
                if near:
                    seg = jnp.clip(c * nb + a - qt + 2, 0, 3)
                    s = s + bias_ref[h, :, pl.ds(pl.multiple_of(seg * LANES, LANES), LANES)]
                s_ref[h, :, a * LANES:(a + 1) * LANES] = s
                blk_max = s if blk_max is None else jnp.maximum(blk_max, s)
            m_old = m_ref[h]
            m_new = jnp.maximum(m_old, jnp.broadcast_to(jnp.max(blk_max, axis=-1, keepdims=True),
                                                        (TQ, LANES)))
            al_ref[h] = jnp.exp2(m_old - m_new)
            m_ref[h] = m_new
        for g0 in range(0, DSA_HEADS, PV_GROUP):
            for h in range(g0, g0 + PV_GROUP):
                m_new = m_ref[h]
                for a in range(nb):
                    sl = slice(a * LANES, (a + 1) * LANES)
                    p_ref[h * TQ:(h + 1) * TQ, sl] = jnp.exp2(s_ref[h, :, sl] - m_new).astype(BF16)
            pv = _dot(p_ref[g0 * TQ:(g0 + PV_GROUP) * TQ, :], ckx)
            for h in range(g0, g0 + PV_GROUP):
                alpha = al_ref[h]
                acc_ref[h] = (jnp.concatenate([alpha, alpha], axis=1) * acc_ref[h]
                              + pv[(h - g0) * TQ:(h - g0 + 1) * TQ])

    def far(c, carry):
        attend(c, False)
        return carry

    lax.fori_loop(0, jnp.maximum(c_diag - 1, 0), far, 0)

    @pl.when(c_diag > 0)
    def _():
        attend(c_diag - 1, True)

    attend(c_diag, True)

    out = jnp.zeros((TQ, DSA_WIDTH), F32)
    for h in range(DSA_HEADS):
        o_lat = acc_ref[h, :, :KV_RANK] / acc_ref[h, :, KV_RANK:]
        out = out + _dot(o_lat.astype(BF16), wuv_ref[h])
    o_ref[...] = out


def _rel_bucket_table(n):
    max_exact = REL_BUCKETS // 2
    d = np.arange(n)
    df = np.maximum(d, 1).astype(np.float32)
    large = max_exact + (np.log(df / np.float32(max_exact)) / np.float32(math.log(REL_MAX_DIST / max_exact))
                         * np.float32(REL_BUCKETS - max_exact)).astype(np.int32)
    large = np.minimum(large, REL_BUCKETS - 1)
    return np.where(d < max_exact, d, large)


def _dsa_attention(h_main, h_idx, ckv, kp, w_uk, w_uv, rel_bias, B, L):
    T = B * L
    nq = L // TQ
    k_sel = min(DSA_TOPK, L // 4)
    assert SCORE_COLS == 4 * LANES and SCORE_COLS >= k_sel
    dist = TQ + np.arange(TQ)[:, None] - np.arange(2 * TQ)[None, :]
    assert _rel_bucket_table(REL_MAX_DIST * 4)[TQ:].min() == REL_BUCKETS - 1
    bucket = _rel_bucket_table(2 * TQ + 1)[np.maximum(dist, 0)]
    rb = rel_bias.astype(F32)
    bias_near = jnp.transpose(rb[bucket] - rb[REL_BUCKETS - 1], (2, 0, 1))
    bias_near = jnp.pad(bias_near * LOG2_E, ((0, 0), (0, 0), (TQ, TQ)))
    assert L % (2 * SCORE_COLS) == 0
    u = np.arange(SCORE_COLS)[:, None]
    v = np.arange(SCORE_COLS + LANES)[None, :]
    tri = jnp.asarray((u <= v) | (v >= SCORE_COLS), BF16)
    wuv_band = jnp.zeros((DSA_HEADS, KV_RANK, DSA_WIDTH), F32)
    for h in range(DSA_HEADS):
        wuv_band = wuv_band.at[h, :, h * DSA_HEAD_DIM:(h + 1) * DSA_HEAD_DIM].set(w_uv[h])
    kern = functools.partial(_dsa_kernel, k_sel=k_sel)
    lpad = -(-L // (2 * SCORE_COLS)) * (2 * SCORE_COLS)
    return pl.pallas_call(
        kern,
        grid=(B, nq),
        in_specs=[
            pl.BlockSpec((TQ, DSA_WIDTH), lambda b, i: (b * nq + i, 1)),
            pl.BlockSpec((TQ, IDX_HEADS * IDX_DIM), lambda b, i: (b * nq + i, 0)),
            pl.BlockSpec((TQ, LANES), lambda b, i: (b * nq + i, 2)),
            pl.BlockSpec((L, 4 * IDX_DIM), lambda b, i: (b, 0)),
            pl.BlockSpec((L, 2 * KV_RANK), lambda b, i: (b, 0)),
            pl.BlockSpec((DSA_HEADS, DSA_HEAD_DIM, KV_RANK), lambda b, i: (0, 0, 0)),
            pl.BlockSpec((DSA_HEADS, KV_RANK, DSA_WIDTH), lambda b, i: (0, 0, 0)),
            pl.BlockSpec((DSA_HEADS, TQ, 4 * TQ), lambda b, i: (0, 0, 0)),
            pl.BlockSpec((SCORE_COLS, SCORE_COLS + LANES), lambda b, i: (0, 0)),
        ],
        out_specs=pl.BlockSpec((TQ, DSA_WIDTH), lambda b, i: (b * nq + i, 0)),
        out_shape=jax.ShapeDtypeStruct((T, DSA_WIDTH), F32),
        scratch_shapes=[
            pltpu.VMEM((TQ, lpad), I32),
            pltpu.VMEM((TQ, lpad), I16),
            pltpu.VMEM((TQ, lpad), I16),
            pltpu.VMEM((TQ, lpad), F32),
            pltpu.VMEM((DSA_HEADS * TQ, KV_RANK), BF16),
            pltpu.VMEM((DSA_HEADS, TQ, SCORE_COLS), F32),
            pltpu.VMEM((DSA_HEADS * TQ, SCORE_COLS), BF16),
            pltpu.VMEM((DSA_HEADS, TQ, LANES), F32),
            pltpu.VMEM((DSA_HEADS, TQ, LANES), F32),
            pltpu.VMEM((DSA_HEADS, TQ, 2 * KV_RANK), F32),
        ],
        compiler_params=_cparams(("parallel", "arbitrary")),
        name="dsa_attention",
    )(h_main, h_idx, h_idx, kp, ckv, w_uk.astype(BF16), wuv_band.astype(BF16), bias_near, tri)


def _mlstm_prep_kernel(cur_ref, halo_ref, w_ref, b_ref, wqh_ref, wql_ref, wkh_ref, wkl_ref,
                       q_ref, k_ref, xbuf, *, tl):
    i = pl.program_id(1)
    xbuf[SUBLANES:SUBLANES + tl, :] = cur_ref[...]
    xbuf[0:SUBLANES, :] = jnp.where(i > 0, halo_ref[...], 0.0)
    base = SUBLANES - (MLSTM_CONV - 1)
    acc = jnp.broadcast_to(b_ref[...], (tl, MLSTM_WIDTH))
    for k in range(MLSTM_CONV):
        acc = acc + w_ref[k:k + 1, :] * xbuf[base + k:base + k + tl, :]
    xc = acc * jax.nn.sigmoid(acc)
    q_ref[...] = _dot3(xc, wqh_ref[...], wql_ref[...]) * (MLSTM_HEAD_DIM ** -0.5)
    k_ref[...] = _dot3(xc, wkh_ref[...], wkl_ref[...])


def _block_diag(w):
    h, d, e = w.shape
    out = jnp.zeros((h * d, h * e), F32)
    for i in range(h):
        out = out.at[i * d:(i + 1) * d, i * e:(i + 1) * e].set(w[i])
    return out


def _mlstm_prep(h_main, conv_w, conv_b, w_qm, w_km, B, L, tl=512):
    T = B * L
    nl = L // tl
    hb = tl // SUBLANES
    wq = _wsplit(_block_diag(w_qm))
    wk = _wsplit(_block_diag(w_km))
    full = lambda shape: pl.BlockSpec(shape, lambda b, i: (0,) * len(shape))
    return pl.pallas_call(
        functools.partial(_mlstm_prep_kernel, tl=tl),
        grid=(B, nl),
        in_specs=[
            pl.BlockSpec((tl, MLSTM_WIDTH), lambda b, i: (b * nl + i, 4)),
            pl.BlockSpec((SUBLANES, MLSTM_WIDTH),
                         lambda b, i: (jnp.maximum((b * nl + i) * hb - 1, 0), 4)),
            full((MLSTM_CONV, MLSTM_WIDTH)), full((1, MLSTM_WIDTH)),
            full((MLSTM_WIDTH, MLSTM_WIDTH)), full((MLSTM_WIDTH, MLSTM_WIDTH)),
            full((MLSTM_WIDTH, MLSTM_WIDTH)), full((MLSTM_WIDTH, MLSTM_WIDTH)),
        ],
        out_specs=[pl.BlockSpec((tl, MLSTM_WIDTH), lambda b, i: (b * nl + i, 0)),
                   pl.BlockSpec((tl, MLSTM_WIDTH), lambda b, i: (b * nl + i, 0))],
        out_shape=[jax.ShapeDtypeStruct((T, MLSTM_WIDTH), F32),
                   jax.ShapeDtypeStruct((T, MLSTM_WIDTH), F32)],
        scratch_shapes=[pltpu.VMEM((SUBLANES + tl, MLSTM_WIDTH), F32)],
        compiler_params=_cparams(("parallel", "parallel")),
        name="mlstm_prep",
    )(h_main, h_main, conv_w, conv_b.reshape(1, -1), wq[0], wq[1], wk[0], wk[1])


ML_ROWS = 128


def _mlstm_kernel(q_ref, k_ref, v_ref, o_ref, sm_ref, bi_ref, bf_ref, g_ref, out_ref,
                  cm_ref, n_ref, m_ref):
    C = MLSTM_CHUNK
    dh = MLSTM_HEAD_DIM

    @pl.when(pl.program_id(1) == 0)
    def _():
        cm_ref[...] = jnp.zeros(cm_ref.shape, F32)
        n_ref[...] = jnp.zeros(n_ref.shape, F32)
        m_ref[...] = jnp.zeros(m_ref.shape, F32)

    sm_t = sm_ref[...].T
    ig = sm_t[SM_I:SM_I + SUBLANES] + bi_ref[...]
    fg = jax.nn.log_sigmoid(sm_t[SM_F:SM_F + SUBLANES] + bf_ref[...])
    lane = lax.broadcasted_iota(I32, (SUBLANES, ML_ROWS), 1) & (C - 1)
    bcum = fg
    s = 1
    while s < C:
        bcum = bcum + jnp.where(lane >= s, pltpu.roll(bcum, s, axis=1), 0.0)
        s *= 2
    cols = jnp.concatenate([bcum, ig, jnp.zeros((LANES - 2 * SUBLANES, ML_ROWS), F32)], axis=0).T
    tri = (lax.broadcasted_iota(I32, (C, C), 1) <= lax.broadcasted_iota(I32, (C, C), 0))

    q_all, k_all, v_all, o_all = q_ref[...], k_ref[...], v_ref[...], o_ref[...]
    g_all = g_ref[...]
    for c in range(ML_ROWS // C):
        r0 = c * C
        for h in range(MLSTM_HEADS):
            hs = slice(h * dh, (h + 1) * dh)
            qj = q_all[r0:r0 + C, hs]
            kj = k_all[r0:r0 + C, hs]
            vj = v_all[r0:r0 + C, hs]
            b_row = bcum[h:h + 1, r0:r0 + C]
            i_row = ig[h:h + 1, r0:r0 + C]
            b_col = cols[r0:r0 + C, h:h + 1]
            i_col = cols[r0:r0 + C, SUBLANES + h:SUBLANES + h + 1]
            m_prev = m_ref[h:h + 1, 0:1]
            n_prev = n_ref[h:h + 1, :]
            cm_prev = cm_ref[h]

            dm = jnp.where(tri, b_col - b_row + i_row, NEG_INF)
            inter = b_col + m_prev
            m_row = jnp.maximum(inter, jnp.max(dm, axis=-1, keepdims=True))
            w_inter = jnp.exp(inter - m_row)
            qb, kb, vb = qj.astype(BF16), kj.astype(BF16), vj.astype(BF16)
            sw = _dot_t(qb, kb) * jnp.exp(dm - m_row)
            num = _dot(sw.astype(BF16), vb) + w_inter * _dot(qb, cm_prev.astype(BF16))
            den = (jnp.sum(sw, axis=-1, keepdims=True)
                   + w_inter * jnp.sum(qj * n_prev, axis=-1, keepdims=True))
            hh = num / jnp.maximum(jnp.abs(den), jnp.exp(-m_row))

            b_last = b_row[:, C - 1:C]
            g_row = b_last - b_row + i_row
            g_col = b_last - b_col + i_col
            m_new = jnp.maximum(b_last + m_prev, jnp.max(g_row, axis=-1, keepdims=True))
            decay = jnp.exp(b_last + m_prev - m_new)
            kw = kj * jnp.exp(g_col - m_new)
            cm_ref[h] = decay * cm_prev + _dot_tl(kw.astype(BF16), vb)
            n_ref[h:h + 1, :] = decay * n_prev + jnp.sum(kw, axis=0, keepdims=True)
            m_ref[h:h + 1, :] = jnp.broadcast_to(m_new, (1, LANES))

            mu = jnp.mean(hh, axis=-1, keepdims=True)
            hc = hh - mu
            var = jnp.mean(hc * hc, axis=-1, keepdims=True)
            hn = hc * lax.rsqrt(var + LN_EPS) * g_all[:, hs]
            out_ref[r0:r0 + C, hs] = jax.nn.sigmoid(o_all[r0:r0 + C, hs]) * hn


def _mlstm(q, k, h_main, h_idx, b_i, b_f, norm_g, B, L):
    T = B * L
    nl = L // ML_ROWS
    pad8 = lambda v: jnp.pad(v.astype(F32), (0, SUBLANES - MLSTM_HEADS)).reshape(SUBLANES, 1)
    row = lambda col: pl.BlockSpec((ML_ROWS, MLSTM_WIDTH), lambda b, i: (b * nl + i, col))
    return pl.pallas_call(
        _mlstm_kernel,
        grid=(B, nl),
        in_specs=[row(0), row(0), row(5), row(6),
                  pl.BlockSpec((ML_ROWS, LANES), lambda b, i: (b * nl + i, 2)),
                  pl.BlockSpec((SUBLANES, 1), lambda b, i: (0, 0)),
                  pl.BlockSpec((SUBLANES, 1), lambda b, i: (0, 0)),
                  pl.BlockSpec((1, MLSTM_WIDTH), lambda b, i: (0, 0))],
        out_specs=row(0),
        out_shape=jax.ShapeDtypeStruct((T, MLSTM_WIDTH), F32),
        scratch_shapes=[pltpu.VMEM((MLSTM_HEADS, MLSTM_HEAD_DIM, MLSTM_HEAD_DIM), F32),
                        pltpu.VMEM((SUBLANES, MLSTM_HEAD_DIM), F32),
                        pltpu.VMEM((SUBLANES, LANES), F32)],
        compiler_params=_cparams(("parallel", "arbitrary")),
        name="mlstm_scan",
    )(q, k, h_main, h_main, h_idx, pad8(b_i), pad8(b_f), norm_g.reshape(1, -1))


def _mix_out_kernel(ya_ref, yb_ref, yc_ref, x_ref, wa_ref, wb_ref, wc_ref, g_ref, b_ref, o_ref, *,
                    alpha):
    y = (_dot(ya_ref[...].astype(BF16), wa_ref[...])
         + _dot(yb_ref[...].astype(BF16), wb_ref[...])
         + _dot(yc_ref[...].astype(BF16), wc_ref[...]))
    o_ref[...] = _layer_norm(alpha * x_ref[...] + y, g_ref[...], b_ref[...])


def _mix_out(y_a, y_b, y_c, x, w_out, g, b, alpha, tm=512):
    T, D = x.shape
    w = w_out.astype(BF16)
    wa, wb, wc = w[:CONV_CH], w[CONV_CH:CONV_CH + DSA_WIDTH], w[CONV_CH + DSA_WIDTH:]
    rows = lambda width: pl.BlockSpec((tm, width), lambda i: (i, 0))
    full = lambda a: pl.BlockSpec(a.shape, lambda i: (0, 0))
    g2, b2 = g.reshape(1, D), b.reshape(1, D)
    return pl.pallas_call(
        functools.partial(_mix_out_kernel, alpha=alpha),
        grid=(T // tm,),
        in_specs=[rows(CONV_CH), rows(DSA_WIDTH), rows(MLSTM_WIDTH), rows(D),
                  full(wa), full(wb), full(wc), full(g2), full(b2)],
        out_specs=rows(D),
        out_shape=jax.ShapeDtypeStruct((T, D), F32),
        compiler_params=_cparams(("parallel",)),
        name="mix_out",
    )(y_a, y_b, y_c, x, wa, wb, wc, g2, b2)


def _xattn_kernel(x_ref, kv_ref, wq_ref, wo_ref, g_ref, b_ref, o_ref, *, alpha):
    x = x_ref[...]
    q = _dot(x.astype(BF16), wq_ref[...])
    kv = kv_ref[...]
    scale = XATTN_HEAD_DIM ** -0.5
    outs = []
    for h in range(XATTN_HEADS):
        hs = slice(h * XATTN_HEAD_DIM, (h + 1) * XATTN_HEAD_DIM)
        kh = kv[:, hs].astype(BF16)
        vh = kv[:, D_MODEL + h * XATTN_HEAD_DIM:D_MODEL + (h + 1) * XATTN_HEAD_DIM].astype(BF16)
        lg = _dot_t(q[:, hs].astype(BF16), kh) * scale
        lg = lg - jnp.max(lg, axis=-1, keepdims=True)
        p = jnp.exp(lg)
        p = p / jnp.sum(p, axis=-1, keepdims=True)
        outs.append(_dot(p.astype(BF16), vh))
    o = jnp.concatenate(outs, axis=-1)
    y = _dot(o.astype(BF16), wo_ref[...])
    o_ref[...] = _layer_norm(alpha * x + y, g_ref[...], b_ref[...])


def _xattn(x, kv, w_q, w_o, g, b, alpha, B, L, tm=256):
    T, D = x.shape
    nl = L // tm
    M = kv.shape[0] // B
    full = lambda a: pl.BlockSpec(a.shape, lambda bb, i: (0, 0))
    g2, b2 = g.reshape(1, D), b.reshape(1, D)
    wq, wo = w_q.astype(BF16), w_o.astype(BF16)
    return pl.pallas_call(
        functools.partial(_xattn_kernel, alpha=alpha),
        grid=(B, nl),
        in_specs=[pl.BlockSpec((tm, D), lambda bb, i: (bb * nl + i, 0)),
                  pl.BlockSpec((M, 2 * D), lambda bb, i: (bb, 0)),
                  full(wq), full(wo), full(g2), full(b2)],
        out_specs=pl.BlockSpec((tm, D), lambda bb, i: (bb * nl + i, 0)),
        out_shape=jax.ShapeDtypeStruct((T, D), F32),
        compiler_params=_cparams(("parallel", "parallel")),
        name="xattn",
    )(x, kv, wq, wo, g2, b2)


def _peer_score_kernel(x_ref, wh_ref, wl_ref, k1h_ref, k1l_ref, k2h_ref, k2l_ref, st_ref):
    q = _dot3(x_ref[...], wh_ref[...], wl_ref[...])
    half = PEER_QDIM // 2
    for h in range(PEER_HEADS):
        for part, (kh_ref, kl_ref) in enumerate(((k1h_ref, k1l_ref), (k2h_ref, k2l_ref))):
            c0 = h * PEER_QDIM + part * half
            q_hi, q_lo = _split(q[:, c0:c0 + half])
            kh, kl = kh_ref[...], kl_ref[...]
            st_ref[2 * h + part] = _dot_t(kh, q_hi) + _dot_t(kl, q_hi) + _dot_t(kh, q_lo)


def _peer_scores(x, w_pq, sub_k1, sub_k2, tm=256):
    T, D = x.shape
    wh, wl = _wsplit(w_pq)
    k1h, k1l = _wsplit(sub_k1)
    k2h, k2l = _wsplit(sub_k2)
    full = lambda a: pl.BlockSpec(a.shape, lambda i: (0, 0))
    return pl.pallas_call(
        _peer_score_kernel,
        grid=(T // tm,),
        in_specs=[pl.BlockSpec((tm, D), lambda i: (i, 0)), full(wh), full(wl),
                  full(k1h), full(k1l), full(k2h), full(k2l)],
        out_specs=pl.BlockSpec((2 * PEER_HEADS, PEER_KEYS, tm), lambda i: (0, 0, i)),
        out_shape=jax.ShapeDtypeStruct((2 * PEER_HEADS, PEER_KEYS, T), F32),
        compiler_params=_cparams(("parallel",)),
        name="peer_scores",
    )(x, wh, wl, k1h, k1l, k2h, k2l)


PEER_NTOP = PEER_TOPK + 1
PEER_PAIR_ROWS = tuple(PEER_NTOP // (k + 1) for k in range(PEER_NTOP))
PEER_CAND_ROWS = -(-sum(PEER_PAIR_ROWS) // SUBLANES) * SUBLANES


def _peer_thr_kernel(st_ref, stats_ref, v2_ref, cand_ref):
    def top_rows(x):
        rows = []
        for _ in range(PEER_NTOP):
            m = jnp.max(x, axis=0, keepdims=True)
            rows.append(m)
            x = jnp.where(x == m, NEG_INF, x)
        return rows

    v1 = top_rows(st_ref[0])
    v2 = top_rows(st_ref[1])
    for k in range(PEER_NTOP):
        v2_ref[k:k + 1, :] = v2[k]
    r = 0
    for k, n in enumerate(PEER_PAIR_ROWS):
        cand_ref[r:r + n, :] = v1[k] + v2_ref[0:n, :]
        r += n
    cand_ref[r:PEER_CAND_ROWS, :] = jnp.full((PEER_CAND_ROWS - r, cand_ref.shape[1]), NEG_INF, F32)
    cand = cand_ref[...]
    x = cand
    for _ in range(PEER_TOPK - 1):
        m = jnp.max(x, axis=0, keepdims=True)
        x = jnp.where(x == m, NEG_INF, x)
    thr = jnp.max(x, axis=0, keepdims=True)
    nxt = jnp.max(jnp.where(x == thr, NEG_INF, x), axis=0, keepdims=True)
    top = v1[0] + v2[0]
    z = jnp.sum(jnp.where(cand >= thr, jnp.exp(cand - top), 0.0), axis=0, keepdims=True)
    cut = jnp.where(nxt > NEG_INF, 0.5 * thr + 0.5 * nxt, thr)
    pad = jnp.zeros((SUBLANES - 4, thr.shape[1]), F32)
    stats_ref[0] = jnp.concatenate([cut, v1[0], v2[0], 1.0 / z, pad], axis=0)


def _peer_thresholds(st, tm=256):
    T = st.shape[2]
    return pl.pallas_call(
        _peer_thr_kernel,
        grid=(PEER_HEADS, T // tm),
        in_specs=[pl.BlockSpec((2, PEER_KEYS, tm), lambda h, i: (h, 0, i))],
        out_specs=pl.BlockSpec((1, SUBLANES, tm), lambda h, i: (h, 0, i)),
        out_shape=jax.ShapeDtypeStruct((PEER_HEADS, SUBLANES, T), F32),
        scratch_shapes=[pltpu.VMEM((3 * SUBLANES, tm), F32),
                        pltpu.VMEM((PEER_CAND_ROWS, tm), F32)],
        compiler_params=_cparams(("parallel", "parallel")),
        name="peer_thresholds",
    )(st)


PEER_TL = 512
PEER_ET = 512
PEER_SUB = 64

def _peer_mix_kernel(x_ref, st_ref, stats_ref, u_ref, vt_ref, g_ref, b_ref, o_ref,
                     xb_ref, d1_ref, e1_ref, e2_ref, gate_ref, w_ref, acc_ref, *, alpha):
    j = pl.program_id(1)
    n_tiles = pl.num_programs(1) - 1
    sub_tiles = PEER_ET // PEER_KEYS

    @pl.when(j == 0)
    def _():
        xb_ref[...] = x_ref[...].astype(BF16)
        for h in range(PEER_HEADS):
            st = stats_ref[h]
            s1 = st_ref[2 * h]
            d1_ref[h] = st[0:1] - s1
            e1_ref[h] = jnp.exp(s1 - st[1:2]) * st[3:4]
            e2_ref[h] = jnp.exp(st_ref[2 * h + 1] - st[2:3])
        acc_ref[...] = jnp.zeros(acc_ref.shape, F32)
        w_ref[...] = jnp.zeros(w_ref.shape, BF16)

    slot = j % 2

    def gate_block(a, lb):
        i1 = jnp.minimum(j, n_tiles - 1) * sub_tiles + a
        ls = slice(lb * LANES, (lb + 1) * LANES)
        d_rows = [d1_ref[h, pl.ds(i1, 1), :][:, ls] for h in range(PEER_HEADS)]
        c_rows = [e1_ref[h, pl.ds(i1, 1), :][:, ls] for h in range(PEER_HEADS)]
        for k0 in range(0, PEER_KEYS, PEER_SUB):
            ks = slice(k0, k0 + PEER_SUB)
            gate = None
            for h in range(PEER_HEADS):
                picked = st_ref[2 * h + 1, ks, ls] >= d_rows[h]
                term = jnp.where(picked, e2_ref[h, ks, ls] * c_rows[h], 0.0)
                gate = term if gate is None else gate + term
            gate_ref[a * PEER_KEYS + k0:a * PEER_KEYS + k0 + PEER_SUB, ls] = gate

    for a in range(sub_tiles):
        for lb in range(PEER_TL // LANES):
            gate_block(a, lb)
    acc_ref[...] += _dot(vt_ref[...], w_ref[1 - slot])
    su = _dot_t(u_ref[...], xb_ref[...])
    act = 0.5 * su * (1.0 + lax.erf(su * (2.0 ** -0.5)))
    w_ref[slot] = (gate_ref[...] * act).astype(BF16)

    @pl.when(j == n_tiles)
    def _():
        y = acc_ref[...].T
        o_ref[...] = _layer_norm(alpha * x_ref[...] + y, g_ref[...], b_ref[...])


def _peer_mix(x, st, stats, peer_u, peer_v, g, b, alpha):
    T, D = x.shape
    E = peer_u.shape[0]
    u = peer_u.astype(BF16)
    vt = peer_v.astype(BF16).T
    g2, b2 = g.reshape(1, D), b.reshape(1, D)
    tl, et = PEER_TL, PEER_ET
    n_e = E // et
    return pl.pallas_call(
        functools.partial(_peer_mix_kernel, alpha=alpha),
        grid=(T // tl, n_e + 1),
        in_specs=[pl.BlockSpec((tl, D), lambda i, j: (i, 0)),
                  pl.BlockSpec((2 * PEER_HEADS, PEER_KEYS, tl), lambda i, j: (0, 0, i)),
                  pl.BlockSpec((PEER_HEADS, SUBLANES, tl), lambda i, j: (0, 0, i)),
                  pl.BlockSpec((et, D), lambda i, j: (jnp.minimum(j, n_e - 1), 0)),
                  pl.BlockSpec((D, et), lambda i, j: (0, jnp.maximum(j - 1, 0))),
                  pl.BlockSpec((1, D), lambda i, j: (0, 0)),
                  pl.BlockSpec((1, D), lambda i, j: (0, 0))],
        out_specs=pl.BlockSpec((tl, D), lambda i, j: (i, 0)),
        out_shape=jax.ShapeDtypeStruct((T, D), F32),
        scratch_shapes=[pltpu.VMEM((tl, D), BF16),
                        pltpu.VMEM((PEER_HEADS, PEER_KEYS, tl), F32),
                        pltpu.VMEM((PEER_HEADS, PEER_KEYS, tl), F32),
                        pltpu.VMEM((PEER_HEADS, PEER_KEYS, tl), F32),
                        pltpu.VMEM((et, tl), F32),
                        pltpu.VMEM((2, et, tl), BF16),
                        pltpu.VMEM((D, tl), F32)],
        compiler_params=_cparams(("parallel", "arbitrary")),
        name="peer_mix",
    )(x, st, stats, u, vt, g2, b2)


def _pack_w_in(w):
    sizes = (512, 512, 128, 256, 64, 4, 256, 256, 256, 4, 4)
    offs = np.concatenate([[0], np.cumsum(sizes)])
    seg = lambda n: w[:, int(offs[n]):int(offs[n + 1])]
    a_in, q_b, c_b, qi, ki, wi, xc, v_m, o_m, i_m, f_m = (seg(n) for n in range(len(sizes)))
    z = lambda n: jnp.zeros((w.shape[0], n), w.dtype)
    main = jnp.concatenate([a_in, q_b, xc, v_m, o_m, c_b], axis=1)
    idx = jnp.concatenate([qi, ki, wi, z(4), i_m, z(4), f_m, z(IDX_W - 256 - SM_F - 4)], axis=1)
    return main, idx


def kernel(x, mem, ln_in_g, ln_in_b, rel_bias, w_in, conv_a_w, conv_a_b, norm_a_g, norm_a_b,
           kv_norm_g, w_uk, w_uv, conv_m_w, conv_m_b, w_qm, w_km, b_i, b_f, norm_m_g, w_out,
           ln1_g, ln1_b, w_cq, w_ckv, w_co, ln2_g, ln2_b, w_pq, sub_k1, sub_k2, peer_u, peer_v,
           ln3_g, ln3_b):
    B, L, D = x.shape
    T = B * L
    depth = w_in.shape[0]
    alpha = (2.0 * depth) ** 0.25
    xs = _entry_ln(x.reshape(T, D), ln_in_g, ln_in_b)
    mem2 = mem.reshape(-1, D)
    for l in range(depth):
        w_main, w_idx = _pack_w_in(w_in[l])
        h_main = _matmul(xs, w_main, 1, 512, 640, "proj_main")
        h_idx = _matmul(xs, w_idx, 3, 512, IDX_W, "proj_idx")
        y_a = _conv_group(h_main, conv_a_w[l], conv_a_b[l], norm_a_g[l], norm_a_b[l], B, L)
        ckv, kp = _dsa_prep(h_main, h_idx, kv_norm_g[l])
        y_b = _dsa_attention(h_main, h_idx, ckv, kp, w_uk[l], w_uv[l], rel_bias, B, L)
        q_m, k_m = _mlstm_prep(h_main, conv_m_w[l], conv_m_b[l], w_qm[l], w_km[l], B, L)
        y_c = _mlstm(q_m, k_m, h_main, h_idx, b_i[l], b_f[l], norm_m_g[l], B, L)
        xs = _mix_out(y_a, y_b, y_c, xs, w_out[l], ln1_g[l], ln1_b[l], alpha)

        kv = _matmul(mem2, w_ckv[l], 1, mem2.shape[0], 512, "xattn_kv")
        xs = _xattn(xs, kv, w_cq[l], w_co[l], ln2_g[l], ln2_b[l], alpha, B, L)

        st = _peer_scores(xs, w_pq[l], sub_k1[l], sub_k2[l])
        stats = _peer_thresholds(st)
        xs = _peer_mix(xs, st, stats, peer_u[l], peer_v[l], ln3_g[l], ln3_b[l], alpha)
    return xs.reshape(B, L, D)
```

```python
import functools
import math

import numpy as np
import jax
import jax.numpy as jnp
from jax import lax
from jax.experimental import pallas as pl
from jax.experimental.pallas import tpu as pltpu

F32 = jnp.float32
BF16 = jnp.bfloat16
I32 = jnp.int32

D_MODEL = 1024
CONV_CH = 256
CONV_WIDTH = 31
DSA_HEADS = 8
DSA_HEAD_DIM = 64
DSA_WIDTH = 512
KV_RANK = 128
IDX_HEADS = 4
IDX_DIM = 64
DSA_TOPK = 256
MLSTM_HEADS = 4
MLSTM_HEAD_DIM = 64
MLSTM_WIDTH = 256
MLSTM_CONV = 4
MLSTM_CHUNK = 64
REL_BUCKETS = 32
REL_MAX_DIST = 128
XATTN_HEADS = 4
XATTN_HEAD_DIM = 256
PEER_HEADS = 8
PEER_KEYS = 128
PEER_QDIM = 256
PEER_TOPK = 16
LN_EPS = 1e-5

LANES = 128
SUBLANES = 8
VMEM_LIMIT = 56 * 1024 * 1024

NEG_INF = float("-inf")
LOG2_E = 1.4426950408889634

MAIN_W = 1920
IDX_W = 384
SM_WI = 64
SM_I = 72
SM_F = 80


def _cparams(sem):
    return pltpu.CompilerParams(dimension_semantics=sem, vmem_limit_bytes=VMEM_LIMIT)


def _dot(a, b):
    return jnp.dot(a, b, preferred_element_type=F32)


def _dot_t(a, b):
    return lax.dot_general(a, b, (((1,), (1,)), ((), ())), preferred_element_type=F32)


def _dot_tl(a, b):
    return lax.dot_general(a, b, (((0,), (0,)), ((), ())), preferred_element_type=F32)


def _split(a):
    hi = a.astype(BF16)
    lo = (a - hi.astype(F32)).astype(BF16)
    return hi, lo


def _dot3(a, b_hi, b_lo, dot=_dot):
    a_hi, a_lo = _split(a)
    return dot(a_hi, b_hi) + dot(a_lo, b_hi) + dot(a_hi, b_lo)


def _layer_norm(x, g, b):
    mu = jnp.mean(x, axis=-1, keepdims=True)
    xc = x - mu
    var = jnp.mean(xc * xc, axis=-1, keepdims=True)
    return xc * lax.rsqrt(var + LN_EPS) * g + b


def _wsplit(w):
    hi = w.astype(BF16)
    lo = (w - hi.astype(F32)).astype(BF16)
    return hi, lo


def _ln_kernel(x_ref, g_ref, b_ref, o_ref):
    o_ref[...] = _layer_norm(x_ref[...], g_ref[...], b_ref[...])


def _entry_ln(x, g, b, tm=512):
    T, D = x.shape
    return pl.pallas_call(
        _ln_kernel,
        grid=(T // tm,),
        in_specs=[pl.BlockSpec((tm, D), lambda i: (i, 0)),
                  pl.BlockSpec((1, D), lambda i: (0, 0)),
                  pl.BlockSpec((1, D), lambda i: (0, 0))],
        out_specs=pl.BlockSpec((tm, D), lambda i: (i, 0)),
        out_shape=jax.ShapeDtypeStruct((T, D), F32),
        compiler_params=_cparams(("parallel",)),
        name="entry_ln",
    )(x, g.reshape(1, D), b.reshape(1, D))


def _mm1_kernel(x_ref, w_ref, o_ref):
    o_ref[...] = _dot(x_ref[...].astype(BF16), w_ref[...])


def _mm3_kernel(x_ref, wh_ref, wl_ref, o_ref):
    o_ref[...] = _dot3(x_ref[...], wh_ref[...], wl_ref[...])


def _matmul(x, w, passes, tm, tn, name):
    T, K = x.shape
    N = w.shape[1]
    x_spec = pl.BlockSpec((tm, K), lambda j, i: (i, 0))
    w_spec = pl.BlockSpec((K, tn), lambda j, i: (0, j))
    if passes == 1:
        kern, ws, w_specs = _mm1_kernel, (w.astype(BF16),), [w_spec]
    else:
        kern, ws, w_specs = _mm3_kernel, _wsplit(w), [w_spec, w_spec]
    return pl.pallas_call(
        kern,
        grid=(N // tn, T // tm),
        in_specs=[x_spec] + w_specs,
        out_specs=pl.BlockSpec((tm, tn), lambda j, i: (i, j)),
        out_shape=jax.ShapeDtypeStruct((T, N), F32),
        compiler_params=_cparams(("parallel", "parallel")),
        name=name,
    )(x, *ws)


CONV_HALO = 32
CONV_ROWS = 64


def _conv_kernel(cur_ref, halo_ref, w_ref, b_ref, g_ref, bb_ref, o_ref, ubuf, *, tl):
    i = pl.program_id(1)
    cur = cur_ref[...]
    ubuf[CONV_HALO:CONV_HALO + tl, :] = cur[:, :CONV_CH] * jax.nn.sigmoid(cur[:, CONV_CH:])
    hal = halo_ref[...]
    uh = hal[:, :CONV_CH] * jax.nn.sigmoid(hal[:, CONV_CH:])
    ubuf[0:CONV_HALO, :] = jnp.where(i > 0, uh, 0.0)
    base = CONV_HALO - (CONV_WIDTH - 1)
    for c in range(tl // CONV_ROWS):
        r0 = c * CONV_ROWS
        acc = jnp.broadcast_to(b_ref[...], (CONV_ROWS, CONV_CH))
        for k in range(CONV_WIDTH):
            acc = acc + w_ref[k:k + 1, :] * ubuf[r0 + base + k:r0 + base + k + CONV_ROWS, :]
        y = _layer_norm(acc, g_ref[...], bb_ref[...])
        o_ref[r0:r0 + CONV_ROWS, :] = y * jax.nn.sigmoid(y)


def _conv_group(h_main, conv_w, conv_b, ln_g, ln_b, B, L, tl=256):
    T = B * L
    nl = L // tl
    hb = tl // CONV_HALO
    return pl.pallas_call(
        functools.partial(_conv_kernel, tl=tl),
        grid=(B, nl),
        in_specs=[
            pl.BlockSpec((tl, 2 * CONV_CH), lambda b, i: (b * nl + i, 0)),
            pl.BlockSpec((CONV_HALO, 2 * CONV_CH),
                         lambda b, i: (jnp.maximum((b * nl + i) * hb - 1, 0), 0)),
            pl.BlockSpec((CONV_WIDTH, CONV_CH), lambda b, i: (0, 0)),
            pl.BlockSpec((1, CONV_CH), lambda b, i: (0, 0)),
            pl.BlockSpec((1, CONV_CH), lambda b, i: (0, 0)),
            pl.BlockSpec((1, CONV_CH), lambda b, i: (0, 0)),
        ],
        out_specs=pl.BlockSpec((tl, CONV_CH), lambda b, i: (b * nl + i, 0)),
        out_shape=jax.ShapeDtypeStruct((T, CONV_CH), F32),
        scratch_shapes=[pltpu.VMEM((CONV_HALO + tl, CONV_CH), F32)],
        compiler_params=_cparams(("parallel", "parallel")),
        name="conv_group",
    )(h_main, h_main, conv_w, conv_b.reshape(1, -1), ln_g.reshape(1, -1), ln_b.reshape(1, -1))


def _dsa_prep_kernel(c_ref, sm_ref, g_ref, ckv_ref, kp_ref):
    c = c_ref[...]
    ms = jnp.mean(c * c, axis=-1, keepdims=True)
    ckv = (c * lax.rsqrt(ms + LN_EPS) * g_ref[...]).astype(BF16)
    ckv_ref[...] = jnp.concatenate([ckv, jnp.ones_like(ckv)], axis=-1)
    k_hi, k_lo = _split(sm_ref[...][:, :IDX_DIM])
    kp_ref[...] = jnp.concatenate([k_hi, k_hi, k_lo, jnp.zeros_like(k_hi)], axis=-1)


def _dsa_prep(h_main, h_idx, kv_g, tm=512):
    T = h_main.shape[0]
    return pl.pallas_call(
        _dsa_prep_kernel,
        grid=(T // tm,),
        in_specs=[pl.BlockSpec((tm, KV_RANK), lambda i: (i, 14)),
                  pl.BlockSpec((tm, LANES), lambda i: (i, 2)),
                  pl.BlockSpec((1, KV_RANK), lambda i: (0, 0))],
        out_specs=[pl.BlockSpec((tm, 2 * KV_RANK), lambda i: (i, 0)),
                   pl.BlockSpec((tm, 4 * IDX_DIM), lambda i: (i, 0))],
        out_shape=[jax.ShapeDtypeStruct((T, 2 * KV_RANK), BF16),
                   jax.ShapeDtypeStruct((T, 4 * IDX_DIM), BF16)],
        compiler_params=_cparams(("parallel",)),
        name="dsa_prep",
    )(h_main, h_idx, kv_g.reshape(1, -1))


TQ = 128
SCORE_COLS = 512
PV_GROUP = 1


I16 = jnp.int16
I16_MIN = -(2 ** 15)


def _dsa_kernel(qb_ref, qi_ref, sm_ref, kp_ref, ckv_ref, wuk_ref, wuv_ref, bias_ref, tri_ref, o_ref,
                keys_ref, hi_ref, lo_ref, selb_ref, ql_ref, s_ref, p_ref, m_ref, al_ref, acc_ref, *,
                k_sel):
    qt = pl.program_id(1)
    nb = SCORE_COLS // LANES
    c_diag = qt // nb
    n_chunks = c_diag + 1
    n_pairs = (n_chunks + 1) // 2
    row_t = qt * TQ + lax.broadcasted_iota(I32, (TQ, LANES), 0)
    lane_c = lax.broadcasted_iota(I32, (TQ, LANES), 1)

    qi = qi_ref[...]
    sm = sm_ref[...]
    qp = []
    for h in range(IDX_HEADS):
        q_hi, q_lo = _split(qi[:, h * IDX_DIM:(h + 1) * IDX_DIM])
        qp.append(jnp.concatenate([q_hi, q_lo, q_hi, jnp.zeros_like(q_hi)], axis=-1))
    qp = jnp.concatenate(qp, axis=0)
    w_fold = (IDX_DIM ** -0.5) * (IDX_HEADS ** -0.5)
    ws = [jnp.broadcast_to(sm[:, SM_WI + h:SM_WI + h + 1] * w_fold, (TQ, LANES))
          for h in range(IDX_HEADS)]

    def score_pair(cp, carry):
        for half in range(2):
            c0 = pl.multiple_of((2 * cp + half) * SCORE_COLS, SCORE_COLS)
            d = _dot_t(qp, kp_ref[pl.ds(c0, SCORE_COLS), :])
            for a in range(nb):
                off = c0 + a * LANES
                s = jnp.zeros((TQ, LANES), F32)
                for h in range(IDX_HEADS):
                    s = s + jnp.maximum(d[h * TQ:(h + 1) * TQ, a * LANES:(a + 1) * LANES], 0.0) * ws[h]
                s = jnp.where(off + lane_c <= row_t, s + 0.0, NEG_INF)
                bits = lax.bitcast_convert_type(s, I32)
                key = bits ^ ((bits >> 31) & 0x7FFFFFFF)
                blk = (2 * cp + half) * nb + a
                keys_ref[blk] = key
                hi_ref[blk] = (key >> 16).astype(I16)
                lo_ref[blk] = ((key & 0xFFFF) + I16_MIN).astype(I16)
        return carry

    lax.fori_loop(0, n_pairs, score_pair, 0)

    def count16(ref, pred):
        def body(c, acc):
            for a in range(2 * nb):
                acc = acc + jnp.where(pred(ref[c * 2 * nb + a]), jnp.int16(1), jnp.int16(0))
            return acc
        acc = lax.fori_loop(0, n_pairs, body, jnp.zeros((TQ, LANES), I16))
        tot = jnp.sum(acc.astype(I32).astype(F32), axis=-1, keepdims=True)
        return jnp.broadcast_to(tot, (TQ, LANES))

    def search16(ref, k_need):
        c_nonneg = count16(ref, lambda blk: blk >= jnp.int16(0))
        th0 = jnp.where(c_nonneg >= k_need, 0, I16_MIN).astype(I32)

        def bit_step(it, th):
            cand = th | (jnp.int32(1) << (14 - it))
            cand16 = cand.astype(I16)
            cnt = count16(ref, lambda blk: blk >= cand16)
            return jnp.where(cnt >= k_need, cand, th)

        return lax.fori_loop(0, 15, bit_step, th0)

    k_full = jnp.full((TQ, LANES), float(k_sel), F32)
    th_hi = search16(hi_ref, k_full)
    th_hi16 = th_hi.astype(I16)
    k_lo = k_full - count16(hi_ref, lambda blk: blk > th_hi16)

    def bucket_chunk(c, carry):
        for a in range(nb):
            blk = c * nb + a
            lo_ref[blk] = jnp.where(hi_ref[blk] == th_hi16, lo_ref[blk], jnp.int16(I16_MIN))
        return carry

    lax.fori_loop(0, 2 * n_pairs, bucket_chunk, 0)
    th_lo = search16(lo_ref, k_lo)
    th_lo16 = th_lo.astype(I16)
    need = k_lo - count16(lo_ref, lambda blk: blk > th_lo16)
    theta = (th_hi << 16) | ((th_lo - I16_MIN) & 0xFFFF)

    def sel_pair(cp, carry):
        pcs = []
        for half in range(2):
            tie = jnp.concatenate(
                [jnp.where(keys_ref[(2 * cp + half) * nb + a] == theta, 1.0, 0.0)
                 for a in range(nb)], axis=1).astype(BF16)
            pcs.append(_dot(tie, tri_ref[...]))
        for half in range(2):
            for a in range(nb):
                blk = (2 * cp + half) * nb + a
                key = keys_ref[blk]
                rank = pcs[half][:, a * LANES:(a + 1) * LANES] + carry
                tied_in = jnp.where(key == theta, jnp.where(rank <= need, 0.0, NEG_INF), NEG_INF)
                sb = jnp.where(key > theta, 0.0, tied_in)
                selb_ref[blk] = jnp.where(blk * LANES + lane_c <= row_t, sb, NEG_INF)
            carry = carry + pcs[half][:, SCORE_COLS:SCORE_COLS + LANES]
        return carry

    lax.fori_loop(0, n_pairs, sel_pair, jnp.zeros((TQ, LANES), F32))

    qb = qb_ref[...]
    scale = (DSA_HEAD_DIM ** -0.5) * LOG2_E
    for h in range(DSA_HEADS):
        qh = qb[:, h * DSA_HEAD_DIM:(h + 1) * DSA_HEAD_DIM].astype(BF16)
        ql_ref[h * TQ:(h + 1) * TQ, :] = (_dot(qh, wuk_ref[h]) * scale).astype(BF16)

    m_ref[...] = jnp.full(m_ref.shape, -1e30, F32)
    acc_ref[...] = jnp.zeros(acc_ref.shape, F32)

    def attend(c, near):
        c0 = pl.multiple_of(c * SCORE_COLS, SCORE_COLS)
        ckx = ckv_ref[pl.ds(c0, SCORE_COLS), :]
        ck = ckx[:, :KV_RANK]
        lg = _dot_t(ql_ref[...], ck)
        for h in range(DSA_HEADS):
            blk_max = None
            for a in range(nb):
                s = lg[h * TQ:(h + 1) * TQ, a * LANES:(a + 1) * LANES] + selb_ref[c * nb + a]
                if near:
                    s = s + bias_ref[h, jnp.clip(c * nb + a - qt + 2, 0, 3)]
                s_ref[h, a] = s
                blk_max = s if blk_max is None else jnp.maximum(blk_max, s)
            m_old = m_ref[h]
            m_new = jnp.maximum(m_old, jnp.broadcast_to(jnp.max(blk_max, axis=-1, keepdims=True),
                                                        (TQ, LANES)))
            al_ref[h] = jnp.exp2(m_old - m_new)
            m_ref[h] = m_new
        for g0 in range(0, DSA_HEADS, PV_GROUP):
            for h in range(g0, g0 + PV_GROUP):
                m_new = m_ref[h]
                for a in range(nb):
                    sl = slice(a * LANES, (a + 1) * LANES)
                    p_ref[h * TQ:(h + 1) * TQ, sl] = jnp.exp2(s_ref[h, a] - m_new).astype(BF16)
            pv = _dot(p_ref[g0 * TQ:(g0 + PV_GROUP) * TQ, :], ckx)
            for h in range(g0, g0 + PV_GROUP):
                alpha = al_ref[h]
                acc_ref[h] = (jnp.concatenate([alpha, alpha], axis=1) * acc_ref[h]
                              + pv[(h - g0) * TQ:(h - g0 + 1) * TQ])

    def far(c, carry):
        attend(c, False)
        return carry

    lax.fori_loop(0, jnp.maximum(c_diag - 1, 0), far, 0)

    @pl.when(c_diag > 0)
    def _():
        attend(c_diag - 1, True)

    attend(c_diag, True)

    out = jnp.zeros((TQ, DSA_WIDTH), F32)
    for h in range(DSA_HEADS):
        o_lat = acc_ref[h, :, :KV_RANK] / acc_ref[h, :, KV_RANK:]
        out = out + _dot(o_lat.astype(BF16), wuv_ref[h])
    o_ref[...] = out


def _rel_bucket_table(n):
    max_exact = REL_BUCKETS // 2
    d = np.arange(n)
    df = np.maximum(d, 1).astype(np.float32)
    large = max_exact + (np.log(df / np.float32(max_exact)) / np.float32(math.log(REL_MAX_DIST / max_exact))
                         * np.float32(REL_BUCKETS - max_exact)).astype(np.int32)
    large = np.minimum(large, REL_BUCKETS - 1)
    return np.where(d < max_exact, d, large)


def _dsa_attention(h_main, h_idx, ckv, kp, w_uk, w_uv, rel_bias, B, L):
    T = B * L
    nq = L // TQ
    k_sel = min(DSA_TOPK, L // 4)
    assert SCORE_COLS == 4 * LANES and SCORE_COLS >= k_sel
    dist = TQ + np.arange(TQ)[:, None] - np.arange(2 * TQ)[None, :]
    assert _rel_bucket_table(REL_MAX_DIST * 4)[TQ:].min() == REL_BUCKETS - 1
    bucket = _rel_bucket_table(2 * TQ + 1)[np.maximum(dist, 0)]
    rb = rel_bias.astype(F32)
    bias_near = jnp.transpose(rb[bucket] - rb[REL_BUCKETS - 1], (2, 0, 1))
    bias_near = jnp.pad(bias_near * LOG2_E, ((0, 0), (0, 0), (TQ, TQ)))
    bias_near = bias_near.reshape(DSA_HEADS, TQ, 4, TQ).transpose(0, 2, 1, 3)
    assert L % (2 * SCORE_COLS) == 0
    u = np.arange(SCORE_COLS)[:, None]
    v = np.arange(SCORE_COLS + LANES)[None, :]
    tri = jnp.asarray((u <= v) | (v >= SCORE_COLS), BF16)
    wuv_band = jnp.zeros((DSA_HEADS, KV_RANK, DSA_WIDTH), F32)
    for h in range(DSA_HEADS):
        wuv_band = wuv_band.at[h, :, h * DSA_HEAD_DIM:(h + 1) * DSA_HEAD_DIM].set(w_uv[h])
    kern = functools.partial(_dsa_kernel, k_sel=k_sel)
    lpad = -(-L // (2 * SCORE_COLS)) * (2 * SCORE_COLS)
    return pl.pallas_call(
        kern,
        grid=(B, nq),
        in_specs=[
            pl.BlockSpec((TQ, DSA_WIDTH), lambda b, i: (b * nq + i, 1)),
            pl.BlockSpec((TQ, IDX_HEADS * IDX_DIM), lambda b, i: (b * nq + i, 0)),
            pl.BlockSpec((TQ, LANES), lambda b, i: (b * nq + i, 2)),
            pl.BlockSpec((L, 4 * IDX_DIM), lambda b, i: (b, 0)),
            pl.BlockSpec((L, 2 * KV_RANK), lambda b, i: (b, 0)),
            pl.BlockSpec((DSA_HEADS, DSA_HEAD_DIM, KV_RANK), lambda b, i: (0, 0, 0)),
            pl.BlockSpec((DSA_HEADS, KV_RANK, DSA_WIDTH), lambda b, i: (0, 0, 0)),
            pl.BlockSpec((DSA_HEADS, 4, TQ, TQ), lambda b, i: (0, 0, 0, 0)),
            pl.BlockSpec((SCORE_COLS, SCORE_COLS + LANES), lambda b, i: (0, 0)),
        ],
        out_specs=pl.BlockSpec((TQ, DSA_WIDTH), lambda b, i: (b * nq + i, 0)),
        out_shape=jax.ShapeDtypeStruct((T, DSA_WIDTH), F32),
        scratch_shapes=[
            pltpu.VMEM((lpad // LANES, TQ, LANES), I32),
            pltpu.VMEM((lpad // LANES, TQ, LANES), I16),
            pltpu.VMEM((lpad // LANES, TQ, LANES), I16),
            pltpu.VMEM((lpad // LANES, TQ, LANES), F32),
            pltpu.VMEM((DSA_HEADS * TQ, KV_RANK), BF16),
            pltpu.VMEM((DSA_HEADS, SCORE_COLS // LANES, TQ, LANES), F32),
            pltpu.VMEM((DSA_HEADS * TQ, SCORE_COLS), BF16),
            pltpu.VMEM((DSA_HEADS, TQ, LANES), F32),
            pltpu.VMEM((DSA_HEADS, TQ, LANES), F32),
            pltpu.VMEM((DSA_HEADS, TQ, 2 * KV_RANK), F32),
        ],
        compiler_params=_cparams(("parallel", "arbitrary")),
        name="dsa_attention",
    )(h_main, h_idx, h_idx, kp, ckv, w_uk.astype(BF16), wuv_band.astype(BF16), bias_near, tri)


def _mlstm_prep_kernel(cur_ref, halo_ref, w_ref, b_ref, wqh_ref, wql_ref, wkh_ref, wkl_ref,
                       q_ref, k_ref, xbuf, *, tl):
    i = pl.program_id(1)
    xbuf[SUBLANES:SUBLANES + tl, :] = cur_ref[...]
    xbuf[0:SUBLANES, :] = jnp.where(i > 0, halo_ref[...], 0.0)
    base = SUBLANES - (MLSTM_CONV - 1)
    acc = jnp.broadcast_to(b_ref[...], (tl, MLSTM_WIDTH))
    for k in range(MLSTM_CONV):
        acc = acc + w_ref[k:k + 1, :] * xbuf[base + k:base + k + tl, :]
    xc = acc * jax.nn.sigmoid(acc)
    q_ref[...] = _dot3(xc, wqh_ref[...], wql_ref[...]) * (MLSTM_HEAD_DIM ** -0.5)
    k_ref[...] = _dot3(xc, wkh_ref[...], wkl_ref[...])


def _block_diag(w):
    h, d, e = w.shape
    out = jnp.zeros((h * d, h * e), F32)
    for i in range(h):
        out = out.at[i * d:(i + 1) * d, i * e:(i + 1) * e].set(w[i])
    return out


def _mlstm_prep(h_main, conv_w, conv_b, w_qm, w_km, B, L, tl=512):
    T = B * L
    nl = L // tl
    hb = tl // SUBLANES
    wq = _wsplit(_block_diag(w_qm))
    wk = _wsplit(_block_diag(w_km))
    full = lambda shape: pl.BlockSpec(shape, lambda b, i: (0,) * len(shape))
    return pl.pallas_call(
        functools.partial(_mlstm_prep_kernel, tl=tl),
        grid=(B, nl),
        in_specs=[
            pl.BlockSpec((tl, MLSTM_WIDTH), lambda b, i: (b * nl + i, 4)),
            pl.BlockSpec((SUBLANES, MLSTM_WIDTH),
                         lambda b, i: (jnp.maximum((b * nl + i) * hb - 1, 0), 4)),
            full((MLSTM_CONV, MLSTM_WIDTH)), full((1, MLSTM_WIDTH)),
            full((MLSTM_WIDTH, MLSTM_WIDTH)), full((MLSTM_WIDTH, MLSTM_WIDTH)),
            full((MLSTM_WIDTH, MLSTM_WIDTH)), full((MLSTM_WIDTH, MLSTM_WIDTH)),
        ],
        out_specs=[pl.BlockSpec((tl, MLSTM_WIDTH), lambda b, i: (b * nl + i, 0)),
                   pl.BlockSpec((tl, MLSTM_WIDTH), lambda b, i: (b * nl + i, 0))],
        out_shape=[jax.ShapeDtypeStruct((T, MLSTM_WIDTH), F32),
                   jax.ShapeDtypeStruct((T, MLSTM_WIDTH), F32)],
        scratch_shapes=[pltpu.VMEM((SUBLANES + tl, MLSTM_WIDTH), F32)],
        compiler_params=_cparams(("parallel", "parallel")),
        name="mlstm_prep",
    )(h_main, h_main, conv_w, conv_b.reshape(1, -1), wq[0], wq[1], wk[0], wk[1])


ML_ROWS = 128


def _mlstm_kernel(q_ref, k_ref, v_ref, o_ref, sm_ref, bi_ref, bf_ref, g_ref, out_ref,
                  cm_ref, n_ref, m_ref):
    C = MLSTM_CHUNK
    dh = MLSTM_HEAD_DIM

    @pl.when(pl.program_id(1) == 0)
    def _():
        cm_ref[...] = jnp.zeros(cm_ref.shape, F32)
        n_ref[...] = jnp.zeros(n_ref.shape, F32)
        m_ref[...] = jnp.zeros(m_ref.shape, F32)

    sm_t = sm_ref[...].T
    ig = sm_t[SM_I:SM_I + SUBLANES] + bi_ref[...]
    fg = jax.nn.log_sigmoid(sm_t[SM_F:SM_F + SUBLANES] + bf_ref[...])
    lane = lax.broadcasted_iota(I32, (SUBLANES, ML_ROWS), 1) & (C - 1)
    bcum = fg
    s = 1
    while s < C:
        bcum = bcum + jnp.where(lane >= s, pltpu.roll(bcum, s, axis=1), 0.0)
        s *= 2
    cols = jnp.concatenate([bcum, ig, jnp.zeros((LANES - 2 * SUBLANES, ML_ROWS), F32)], axis=0).T
    tri = (lax.broadcasted_iota(I32, (C, C), 1) <= lax.broadcasted_iota(I32, (C, C), 0))

    q_all, k_all, v_all, o_all = q_ref[...], k_ref[...], v_ref[...], o_ref[...]
    g_all = g_ref[...]
    for c in range(ML_ROWS // C):
        r0 = c * C
        for h in range(MLSTM_HEADS):
            hs = slice(h * dh, (h + 1) * dh)
            qj = q_all[r0:r0 + C, hs]
            kj = k_all[r0:r0 + C, hs]
            vj = v_all[r0:r0 + C, hs]
            b_row = bcum[h:h + 1, r0:r0 + C]
            i_row = ig[h:h + 1, r0:r0 + C]
            b_col = cols[r0:r0 + C, h:h + 1]
            i_col = cols[r0:r0 + C, SUBLANES + h:SUBLANES + h + 1]
            m_prev = m_ref[h:h + 1, 0:1]
            n_prev = n_ref[h:h + 1, :]
            cm_prev = cm_ref[h]

            dm = jnp.where(tri, b_col - b_row + i_row, NEG_INF)
            inter = b_col + m_prev
            m_row = jnp.maximum(inter, jnp.max(dm, axis=-1, keepdims=True))
            w_inter = jnp.exp(inter - m_row)
            qb, kb, vb = qj.astype(BF16), kj.astype(BF16), vj.astype(BF16)
            sw = _dot_t(qb, kb) * jnp.exp(dm - m_row)
            num = _dot(sw.astype(BF16), vb) + w_inter * _dot(qb, cm_prev.astype(BF16))
            den = (jnp.sum(sw, axis=-1, keepdims=True)
                   + w_inter * jnp.sum(qj * n_prev, axis=-1, keepdims=True))
            hh = num / jnp.maximum(jnp.abs(den), jnp.exp(-m_row))

            b_last = b_row[:, C - 1:C]
            g_row = b_last - b_row + i_row
            g_col = b_last - b_col + i_col
            m_new = jnp.maximum(b_last + m_prev, jnp.max(g_row, axis=-1, keepdims=True))
            decay = jnp.exp(b_last + m_prev - m_new)
            kw = kj * jnp.exp(g_col - m_new)
            cm_ref[h] = decay * cm_prev + _dot_tl(kw.astype(BF16), vb)
            n_ref[h:h + 1, :] = decay * n_prev + jnp.sum(kw, axis=0, keepdims=True)
            m_ref[h:h + 1, :] = jnp.broadcast_to(m_new, (1, LANES))

            mu = jnp.mean(hh, axis=-1, keepdims=True)
            hc = hh - mu
            var = jnp.mean(hc * hc, axis=-1, keepdims=True)
            hn = hc * lax.rsqrt(var + LN_EPS) * g_all[:, hs]
            out_ref[r0:r0 + C, hs] = jax.nn.sigmoid(o_all[r0:r0 + C, hs]) * hn


def _mlstm(q, k, h_main, h_idx, b_i, b_f, norm_g, B, L):
    T = B * L
    nl = L // ML_ROWS
    pad8 = lambda v: jnp.pad(v.astype(F32), (0, SUBLANES - MLSTM_HEADS)).reshape(SUBLANES, 1)
    row = lambda col: pl.BlockSpec((ML_ROWS, MLSTM_WIDTH), lambda b, i: (b * nl + i, col))
    return pl.pallas_call(
        _mlstm_kernel,
        grid=(B, nl),
        in_specs=[row(0), row(0), row(5), row(6),
                  pl.BlockSpec((ML_ROWS, LANES), lambda b, i: (b * nl + i, 2)),
                  pl.BlockSpec((SUBLANES, 1), lambda b, i: (0, 0)),
                  pl.BlockSpec((SUBLANES, 1), lambda b, i: (0, 0)),
                  pl.BlockSpec((1, MLSTM_WIDTH), lambda b, i: (0, 0))],
        out_specs=row(0),
        out_shape=jax.ShapeDtypeStruct((T, MLSTM_WIDTH), F32),
        scratch_shapes=[pltpu.VMEM((MLSTM_HEADS, MLSTM_HEAD_DIM, MLSTM_HEAD_DIM), F32),
                        pltpu.VMEM((SUBLANES, MLSTM_HEAD_DIM), F32),
                        pltpu.VMEM((SUBLANES, LANES), F32)],
        compiler_params=_cparams(("parallel", "arbitrary")),
        name="mlstm_scan",
    )(q, k, h_main, h_main, h_idx, pad8(b_i), pad8(b_f), norm_g.reshape(1, -1))


def _mix_out_kernel(ya_ref, yb_ref, yc_ref, x_ref, wa_ref, wb_ref, wc_ref, g_ref, b_ref, o_ref, *,
                    alpha):
    y = (_dot(ya_ref[...].astype(BF16), wa_ref[...])
         + _dot(yb_ref[...].astype(BF16), wb_ref[...])
         + _dot(yc_ref[...].astype(BF16), wc_ref[...]))
    o_ref[...] = _layer_norm(alpha * x_ref[...] + y, g_ref[...], b_ref[...])


def _mix_out(y_a, y_b, y_c, x, w_out, g, b, alpha, tm=512):
    T, D = x.shape
    w = w_out.astype(BF16)
    wa, wb, wc = w[:CONV_CH], w[CONV_CH:CONV_CH + DSA_WIDTH], w[CONV_CH + DSA_WIDTH:]
    rows = lambda width: pl.BlockSpec((tm, width), lambda i: (i, 0))
    full = lambda a: pl.BlockSpec(a.shape, lambda i: (0, 0))
    g2, b2 = g.reshape(1, D), b.reshape(1, D)
    return pl.pallas_call(
        functools.partial(_mix_out_kernel, alpha=alpha),
        grid=(T // tm,),
        in_specs=[rows(CONV_CH), rows(DSA_WIDTH), rows(MLSTM_WIDTH), rows(D),
                  full(wa), full(wb), full(wc), full(g2), full(b2)],
        out_specs=rows(D),
        out_shape=jax.ShapeDtypeStruct((T, D), F32),
        compiler_params=_cparams(("parallel",)),
        name="mix_out",
    )(y_a, y_b, y_c, x, wa, wb, wc, g2, b2)


def _xattn_kernel(x_ref, kv_ref, wq_ref, wo_ref, g_ref, b_ref, o_ref, *, alpha):
    x = x_ref[...]
    q = _dot(x.astype(BF16), wq_ref[...])
    kv = kv_ref[...]
    scale = XATTN_HEAD_DIM ** -0.5
    outs = []
    for h in range(XATTN_HEADS):
        hs = slice(h * XATTN_HEAD_DIM, (h + 1) * XATTN_HEAD_DIM)
        kh = kv[:, hs].astype(BF16)
        vh = kv[:, D_MODEL + h * XATTN_HEAD_DIM:D_MODEL + (h + 1) * XATTN_HEAD_DIM].astype(BF16)
        lg = _dot_t(q[:, hs].astype(BF16), kh) * scale
        lg = lg - jnp.max(lg, axis=-1, keepdims=True)
        p = jnp.exp(lg)
        p = p / jnp.sum(p, axis=-1, keepdims=True)
        outs.append(_dot(p.astype(BF16), vh))
    o = jnp.concatenate(outs, axis=-1)
    y = _dot(o.astype(BF16), wo_ref[...])
    o_ref[...] = _layer_norm(alpha * x + y, g_ref[...], b_ref[...])


def _xattn(x, kv, w_q, w_o, g, b, alpha, B, L, tm=256):
    T, D = x.shape
    nl = L // tm
    M = kv.shape[0] // B
    full = lambda a: pl.BlockSpec(a.shape, lambda bb, i: (0, 0))
    g2, b2 = g.reshape(1, D), b.reshape(1, D)
    wq, wo = w_q.astype(BF16), w_o.astype(BF16)
    return pl.pallas_call(
        functools.partial(_xattn_kernel, alpha=alpha),
        grid=(B, nl),
        in_specs=[pl.BlockSpec((tm, D), lambda bb, i: (bb * nl + i, 0)),
                  pl.BlockSpec((M, 2 * D), lambda bb, i: (bb, 0)),
                  full(wq), full(wo), full(g2), full(b2)],
        out_specs=pl.BlockSpec((tm, D), lambda bb, i: (bb * nl + i, 0)),
        out_shape=jax.ShapeDtypeStruct((T, D), F32),
        compiler_params=_cparams(("parallel", "parallel")),
        name="xattn",
    )(x, kv, wq, wo, g2, b2)


def _peer_score_kernel(x_ref, wh_ref, wl_ref, k1h_ref, k1l_ref, k2h_ref, k2l_ref, st_ref):
    q = _dot3(x_ref[...], wh_ref[...], wl_ref[...])
    half = PEER_QDIM // 2
    for h in range(PEER_HEADS):
        for part, (kh_ref, kl_ref) in enumerate(((k1h_ref, k1l_ref), (k2h_ref, k2l_ref))):
            c0 = h * PEER_QDIM + part * half
            q_hi, q_lo = _split(q[:, c0:c0 + half])
            kh, kl = kh_ref[...], kl_ref[...]
            st_ref[2 * h + part] = _dot_t(kh, q_hi) + _dot_t(kl, q_hi) + _dot_t(kh, q_lo)


def _peer_scores(x, w_pq, sub_k1, sub_k2, tm=256):
    T, D = x.shape
    wh, wl = _wsplit(w_pq)
    k1h, k1l = _wsplit(sub_k1)
    k2h, k2l = _wsplit(sub_k2)
    full = lambda a: pl.BlockSpec(a.shape, lambda i: (0, 0))
    return pl.pallas_call(
        _peer_score_kernel,
        grid=(T // tm,),
        in_specs=[pl.BlockSpec((tm, D), lambda i: (i, 0)), full(wh), full(wl),
                  full(k1h), full(k1l), full(k2h), full(k2l)],
        out_specs=pl.BlockSpec((2 * PEER_HEADS, PEER_KEYS, tm), lambda i: (0, 0, i)),
        out_shape=jax.ShapeDtypeStruct((2 * PEER_HEADS, PEER_KEYS, T), F32),
        compiler_params=_cparams(("parallel",)),
        name="peer_scores",
    )(x, wh, wl, k1h, k1l, k2h, k2l)


PEER_NTOP = PEER_TOPK + 1
PEER_PAIR_ROWS = tuple(PEER_NTOP // (k + 1) for k in range(PEER_NTOP))
PEER_CAND_ROWS = -(-sum(PEER_PAIR_ROWS) // SUBLANES) * SUBLANES


def _peer_thr_kernel(st_ref, stats_ref, v2_ref, cand_ref):
    def top_rows(x):
        rows = []
        for _ in range(PEER_NTOP):
            m = jnp.max(x, axis=0, keepdims=True)
            rows.append(m)
            x = jnp.where(x == m, NEG_INF, x)
        return rows

    v1 = top_rows(st_ref[0])
    v2 = top_rows(st_ref[1])
    for k in range(PEER_NTOP):
        v2_ref[k:k + 1, :] = v2[k]
    r = 0
    for k, n in enumerate(PEER_PAIR_ROWS):
        cand_ref[r:r + n, :] = v1[k] + v2_ref[0:n, :]
        r += n
    cand_ref[r:PEER_CAND_ROWS, :] = jnp.full((PEER_CAND_ROWS - r, cand_ref.shape[1]), NEG_INF, F32)
    cand = cand_ref[...]
    x = cand
    for _ in range(PEER_TOPK - 1):
        m = jnp.max(x, axis=0, keepdims=True)
        x = jnp.where(x == m, NEG_INF, x)
    thr = jnp.max(x, axis=0, keepdims=True)
    nxt = jnp.max(jnp.where(x == thr, NEG_INF, x), axis=0, keepdims=True)
    top = v1[0] + v2[0]
    z = jnp.sum(jnp.where(cand >= thr, jnp.exp(cand - top), 0.0), axis=0, keepdims=True)
    cut = jnp.where(nxt > NEG_INF, 0.5 * thr + 0.5 * nxt, thr)
    pad = jnp.zeros((SUBLANES - 4, thr.shape[1]), F32)
    stats_ref[0] = jnp.concatenate([cut, v1[0], v2[0], 1.0 / z, pad], axis=0)


def _peer_thresholds(st, tm=256):
    T = st.shape[2]
    return pl.pallas_call(
        _peer_thr_kernel,
        grid=(PEER_HEADS, T // tm),
        in_specs=[pl.BlockSpec((2, PEER_KEYS, tm), lambda h, i: (h, 0, i))],
        out_specs=pl.BlockSpec((1, SUBLANES, tm), lambda h, i: (h, 0, i)),
        out_shape=jax.ShapeDtypeStruct((PEER_HEADS, SUBLANES, T), F32),
        scratch_shapes=[pltpu.VMEM((3 * SUBLANES, tm), F32),
                        pltpu.VMEM((PEER_CAND_ROWS, tm), F32)],
        compiler_params=_cparams(("parallel", "parallel")),
        name="peer_thresholds",
    )(st)


PEER_TL = 512
PEER_ET = 512
PEER_SUB = 64

def _peer_mix_kernel(x_ref, st_ref, stats_ref, u_ref, vt_ref, g_ref, b_ref, o_ref,
                     xb_ref, d1_ref, e1_ref, e2_ref, gate_ref, w_ref, acc_ref, *, alpha):
    j = pl.program_id(1)
    n_tiles = pl.num_programs(1) - 1
    sub_tiles = PEER_ET // PEER_KEYS

    @pl.when(j == 0)
    def _():
        xb_ref[...] = x_ref[...].astype(BF16)
        for h in range(PEER_HEADS):
            st = stats_ref[h]
            s1 = st_ref[2 * h]
            d1_ref[h] = st[0:1] - s1
            e1_ref[h] = jnp.exp(s1 - st[1:2]) * st[3:4]
            e2_ref[h] = jnp.exp(st_ref[2 * h + 1] - st[2:3])
        acc_ref[...] = jnp.zeros(acc_ref.shape, F32)
        w_ref[...] = jnp.zeros(w_ref.shape, BF16)

    slot = j % 2

    def gate_block(a, lb):
        i1 = jnp.minimum(j, n_tiles - 1) * sub_tiles + a
        ls = slice(lb * LANES, (lb + 1) * LANES)
        d_rows = [d1_ref[h, pl.ds(i1, 1), :][:, ls] for h in range(PEER_HEADS)]
        c_rows = [e1_ref[h, pl.ds(i1, 1), :][:, ls] for h in range(PEER_HEADS)]
        for k0 in range(0, PEER_KEYS, PEER_SUB):
            ks = slice(k0, k0 + PEER_SUB)
            gate = None
            for h in range(PEER_HEADS):
                picked = st_ref[2 * h + 1, ks, ls] >= d_rows[h]
                term = jnp.where(picked, e2_ref[h, ks, ls] * c_rows[h], 0.0)
                gate = term if gate is None else gate + term
            gate_ref[a * PEER_KEYS + k0:a * PEER_KEYS + k0 + PEER_SUB, ls] = gate

    for a in range(sub_tiles):
        for lb in range(PEER_TL // LANES):
            gate_block(a, lb)
    acc_ref[...] += _dot(vt_ref[...], w_ref[1 - slot])
    su = _dot_t(u_ref[...], xb_ref[...])
    act = 0.5 * su * (1.0 + lax.erf(su * (2.0 ** -0.5)))
    w_ref[slot] = (gate_ref[...] * act).astype(BF16)

    @pl.when(j == n_tiles)
    def _():
        y = acc_ref[...].T
        o_ref[...] = _layer_norm(alpha * x_ref[...] + y, g_ref[...], b_ref[...])


def _peer_mix(x, st, stats, peer_u, peer_v, g, b, alpha):
    T, D = x.shape
    E = peer_u.shape[0]
    u = peer_u.astype(BF16)
    vt = peer_v.astype(BF16).T
    g2, b2 = g.reshape(1, D), b.reshape(1, D)
    tl, et = PEER_TL, PEER_ET
    n_e = E // et
    return pl.pallas_call(
        functools.partial(_peer_mix_kernel, alpha=alpha),
        grid=(T // tl, n_e + 1),
        in_specs=[pl.BlockSpec((tl, D), lambda i, j: (i, 0)),
                  pl.BlockSpec((2 * PEER_HEADS, PEER_KEYS, tl), lambda i, j: (0, 0, i)),
                  pl.BlockSpec((PEER_HEADS, SUBLANES, tl), lambda i, j: (0, 0, i)),
                  pl.BlockSpec((et, D), lambda i, j: (jnp.minimum(j, n_e - 1), 0)),
                  pl.BlockSpec((D, et), lambda i, j: (0, jnp.maximum(j - 1, 0))),
                  pl.BlockSpec((1, D), lambda i, j: (0, 0)),
                  pl.BlockSpec((1, D), lambda i, j: (0, 0))],
        out_specs=pl.BlockSpec((tl, D), lambda i, j: (i, 0)),
        out_shape=jax.ShapeDtypeStruct((T, D), F32),
        scratch_shapes=[pltpu.VMEM((tl, D), BF16),
                        pltpu.VMEM((PEER_HEADS, PEER_KEYS, tl), F32),
                        pltpu.VMEM((PEER_HEADS, PEER_KEYS, tl), F32),
                        pltpu.VMEM((PEER_HEADS, PEER_KEYS, tl), F32),
                        pltpu.VMEM((et, tl), F32),
                        pltpu.VMEM((2, et, tl), BF16),
                        pltpu.VMEM((D, tl), F32)],
        compiler_params=_cparams(("parallel", "arbitrary")),
        name="peer_mix",
    )(x, st, stats, u, vt, g2, b2)


def _pack_w_in(w):
    sizes = (512, 512, 128, 256, 64, 4, 256, 256, 256, 4, 4)
    offs = np.concatenate([[0], np.cumsum(sizes)])
    seg = lambda n: w[:, int(offs[n]):int(offs[n + 1])]
    a_in, q_b, c_b, qi, ki, wi, xc, v_m, o_m, i_m, f_m = (seg(n) for n in range(len(sizes)))
    z = lambda n: jnp.zeros((w.shape[0], n), w.dtype)
    main = jnp.concatenate([a_in, q_b, xc, v_m, o_m, c_b], axis=1)
    idx = jnp.concatenate([qi, ki, wi, z(4), i_m, z(4), f_m, z(IDX_W - 256 - SM_F - 4)], axis=1)
    return main, idx


def kernel(x, mem, ln_in_g, ln_in_b, rel_bias, w_in, conv_a_w, conv_a_b, norm_a_g, norm_a_b,
           kv_norm_g, w_uk, w_uv, conv_m_w, conv_m_b, w_qm, w_km, b_i, b_f, norm_m_g, w_out,
           ln1_g, ln1_b, w_cq, w_ckv, w_co, ln2_g, ln2_b, w_pq, sub_k1, sub_k2, peer_u, peer_v,
           ln3_g, ln3_b):
    B, L, D = x.shape
    T = B * L
    depth = w_in.shape[0]
    alpha = (2.0 * depth) ** 0.25
    xs = _entry_ln(x.reshape(T, D), ln_in_g, ln_in_b)
    mem2 = mem.reshape(-1, D)
    for l in range(depth):
        w_main, w_idx = _pack_w_in(w_in[l])
        h_main = _matmul(xs, w_main, 1, 512, 640, "proj_main")
        h_idx = _matmul(xs, w_idx, 3, 512, IDX_W, "proj_idx")
        y_a = _conv_group(h_main, conv_a_w[l], conv_a_b[l], norm_a_g[l], norm_a_b[l], B, L)
        ckv, kp = _dsa_prep(h_main, h_idx, kv_norm_g[l])
        y_b = _dsa_attention(h_main, h_idx, ckv, kp, w_uk[l], w_uv[l], rel_bias, B, L)
        q_m, k_m = _mlstm_prep(h_main, conv_m_w[l], conv_m_b[l], w_qm[l], w_km[l], B, L)
        y_c = _mlstm(q_m, k_m, h_main, h_idx, b_i[l], b_f[l], norm_m_g[l], B, L)
        xs = _mix_out(y_a, y_b, y_c, xs, w_out[l], ln1_g[l], ln1_b[l], alpha)

        kv = _matmul(mem2, w_ckv[l], 1, mem2.shape[0], 512, "xattn_kv")
        xs = _xattn(xs, kv, w_cq[l], w_co[l], ln2_g[l], ln2_b[l], alpha, B, L)

        st = _peer_scores(xs, w_pq[l], sub_k1[l], sub_k2[l])
        stats = _peer_thresholds(st)
        xs = _peer_mix(xs, st, stats, peer_u[l], peer_v[l], ln3_g[l], ln3_b[l], alpha)
    return xs.reshape(B, L, D)
```

```python
import functools
import math

import numpy as np
import jax
import jax.numpy as jnp
from jax import lax
from jax.experimental import pallas as pl
from jax.experimental.pallas import tpu as pltpu

F32 = jnp.float32
BF16 = jnp.bfloat16
I32 = jnp.int32

D_MODEL = 1024
CONV_CH = 256
CONV_WIDTH = 31
DSA_HEADS = 8
DSA_HEAD_DIM = 64
DSA_WIDTH = 512
KV_RANK = 128
IDX_HEADS = 4
IDX_DIM = 64
DSA_TOPK = 256
MLSTM_HEADS = 4
MLSTM_HEAD_DIM = 64
MLSTM_WIDTH = 256
MLSTM_CONV = 4
MLSTM_CHUNK = 64
REL_BUCKETS = 32
REL_MAX_DIST = 128
XATTN_HEADS = 4
XATTN_HEAD_DIM = 256
PEER_HEADS = 8
PEER_KEYS = 128
PEER_QDIM = 256
PEER_TOPK = 16
LN_EPS = 1e-5

LANES = 128
SUBLANES = 8
VMEM_LIMIT = 56 * 1024 * 1024

NEG_INF = float("-inf")
LOG2_E = 1.4426950408889634

MAIN_W = 1920
IDX_W = 384
SM_WI = 64
SM_I = 72
SM_F = 80


def _cparams(sem):
    return pltpu.CompilerParams(dimension_semantics=sem, vmem_limit_bytes=VMEM_LIMIT)


def _dot(a, b):
    return jnp.dot(a, b, preferred_element_type=F32)


def _dot_t(a, b):
    return lax.dot_general(a, b, (((1,), (1,)), ((), ())), preferred_element_type=F32)


def _dot_tl(a, b):
    return lax.dot_general(a, b, (((0,), (0,)), ((), ())), preferred_element_type=F32)


def _split(a):
    hi = a.astype(BF16)
    lo = (a - hi.astype(F32)).astype(BF16)
    return hi, lo


def _dot3(a, b_hi, b_lo, dot=_dot):
    a_hi, a_lo = _split(a)
    return dot(a_hi, b_hi) + dot(a_lo, b_hi) + dot(a_hi, b_lo)


def _layer_norm(x, g, b):
    mu = jnp.mean(x, axis=-1, keepdims=True)
    xc = x - mu
    var = jnp.mean(xc * xc, axis=-1, keepdims=True)
    return xc * lax.rsqrt(var + LN_EPS) * g + b


def _wsplit(w):
    hi = w.astype(BF16)
    lo = (w - hi.astype(F32)).astype(BF16)
    return hi, lo


def _ln_kernel(x_ref, g_ref, b_ref, o_ref):
    o_ref[...] = _layer_norm(x_ref[...], g_ref[...], b_ref[...])


def _entry_ln(x, g, b, tm=512):
    T, D = x.shape
    return pl.pallas_call(
        _ln_kernel,
        grid=(T // tm,),
        in_specs=[pl.BlockSpec((tm, D), lambda i: (i, 0)),
                  pl.BlockSpec((1, D), lambda i: (0, 0)),
                  pl.BlockSpec((1, D), lambda i: (0, 0))],
        out_specs=pl.BlockSpec((tm, D), lambda i: (i, 0)),
        out_shape=jax.ShapeDtypeStruct((T, D), F32),
        compiler_params=_cparams(("parallel",)),
        name="entry_ln",
    )(x, g.reshape(1, D), b.reshape(1, D))


def _mm1_kernel(x_ref, w_ref, o_ref):
    o_ref[...] = _dot(x_ref[...].astype(BF16), w_ref[...])


def _mm3_kernel(x_ref, wh_ref, wl_ref, o_ref):
    o_ref[...] = _dot3(x_ref[...], wh_ref[...], wl_ref[...])


def _matmul(x, w, passes, tm, tn, name):
    T, K = x.shape
    N = w.shape[1]
    x_spec = pl.BlockSpec((tm, K), lambda j, i: (i, 0))
    w_spec = pl.BlockSpec((K, tn), lambda j, i: (0, j))
    if passes == 1:
        kern, ws, w_specs = _mm1_kernel, (w.astype(BF16),), [w_spec]
    else:
        kern, ws, w_specs = _mm3_kernel, _wsplit(w), [w_spec, w_spec]
    return pl.pallas_call(
        kern,
        grid=(N // tn, T // tm),
        in_specs=[x_spec] + w_specs,
        out_specs=pl.BlockSpec((tm, tn), lambda j, i: (i, j)),
        out_shape=jax.ShapeDtypeStruct((T, N), F32),
        compiler_params=_cparams(("parallel", "parallel")),
        name=name,
    )(x, *ws)


CONV_HALO = 32
CONV_ROWS = 64


def _conv_kernel(cur_ref, halo_ref, w_ref, b_ref, g_ref, bb_ref, o_ref, ubuf, *, tl):
    i = pl.program_id(1)
    cur = cur_ref[...]
    ubuf[CONV_HALO:CONV_HALO + tl, :] = cur[:, :CONV_CH] * jax.nn.sigmoid(cur[:, CONV_CH:])
    hal = halo_ref[...]
    uh = hal[:, :CONV_CH] * jax.nn.sigmoid(hal[:, CONV_CH:])
    ubuf[0:CONV_HALO, :] = jnp.where(i > 0, uh, 0.0)
    base = CONV_HALO - (CONV_WIDTH - 1)
    for c in range(tl // CONV_ROWS):
        r0 = c * CONV_ROWS
        acc = jnp.broadcast_to(b_ref[...], (CONV_ROWS, CONV_CH))
        for k in range(CONV_WIDTH):
            acc = acc + w_ref[k:k + 1, :] * ubuf[r0 + base + k:r0 + base + k + CONV_ROWS, :]
        y = _layer_norm(acc, g_ref[...], bb_ref[...])
        o_ref[r0:r0 + CONV_ROWS, :] = y * jax.nn.sigmoid(y)


def _conv_group(h_main, conv_w, conv_b, ln_g, ln_b, B, L, tl=256):
    T = B * L
    nl = L // tl
    hb = tl // CONV_HALO
    return pl.pallas_call(
        functools.partial(_conv_kernel, tl=tl),
        grid=(B, nl),
        in_specs=[
            pl.BlockSpec((tl, 2 * CONV_CH), lambda b, i: (b * nl + i, 0)),
            pl.BlockSpec((CONV_HALO, 2 * CONV_CH),
                         lambda b, i: (jnp.maximum((b * nl + i) * hb - 1, 0), 0)),
            pl.BlockSpec((CONV_WIDTH, CONV_CH), lambda b, i: (0, 0)),
            pl.BlockSpec((1, CONV_CH), lambda b, i: (0, 0)),
            pl.BlockSpec((1, CONV_CH), lambda b, i: (0, 0)),
            pl.BlockSpec((1, CONV_CH), lambda b, i: (0, 0)),
        ],
        out_specs=pl.BlockSpec((tl, CONV_CH), lambda b, i: (b * nl + i, 0)),
        out_shape=jax.ShapeDtypeStruct((T, CONV_CH), F32),
        scratch_shapes=[pltpu.VMEM((CONV_HALO + tl, CONV_CH), F32)],
        compiler_params=_cparams(("parallel", "parallel")),
        name="conv_group",
    )(h_main, h_main, conv_w, conv_b.reshape(1, -1), ln_g.reshape(1, -1), ln_b.reshape(1, -1))


def _dsa_prep_kernel(c_ref, sm_ref, g_ref, ckv_ref, kp_ref):
    c = c_ref[...]
    ms = jnp.mean(c * c, axis=-1, keepdims=True)
    ckv = (c * lax.rsqrt(ms + LN_EPS) * g_ref[...]).astype(BF16)
    ckv_ref[...] = jnp.concatenate([ckv, jnp.ones_like(ckv)], axis=-1)
    k_hi, k_lo = _split(sm_ref[...][:, :IDX_DIM])
    kp_ref[...] = jnp.concatenate([k_hi, k_hi, k_lo, jnp.zeros_like(k_hi)], axis=-1)


def _dsa_prep(h_main, h_idx, kv_g, tm=512):
    T = h_main.shape[0]
    return pl.pallas_call(
        _dsa_prep_kernel,
        grid=(T // tm,),
        in_specs=[pl.BlockSpec((tm, KV_RANK), lambda i: (i, 14)),
                  pl.BlockSpec((tm, LANES), lambda i: (i, 2)),
                  pl.BlockSpec((1, KV_RANK), lambda i: (0, 0))],
        out_specs=[pl.BlockSpec((tm, 2 * KV_RANK), lambda i: (i, 0)),
                   pl.BlockSpec((tm, 4 * IDX_DIM), lambda i: (i, 0))],
        out_shape=[jax.ShapeDtypeStruct((T, 2 * KV_RANK), BF16),
                   jax.ShapeDtypeStruct((T, 4 * IDX_DIM), BF16)],
        compiler_params=_cparams(("parallel",)),
        name="dsa_prep",
    )(h_main, h_idx, kv_g.reshape(1, -1))


TQ = 128
SCORE_COLS = 512
PV_GROUP = 1


I16 = jnp.int16
I16_MIN = -(2 ** 15)


def _dsa_kernel(qb_ref, qi_ref, sm_ref, kp_ref, ckv_ref, wuk_ref, wuv_ref, bias_ref, tri_ref, o_ref,
                keys_ref, hi_ref, lo_ref, selb_ref, ql_ref, s_ref, p_ref, m_ref, al_ref, acc_ref, *,
                k_sel):
    qt = pl.program_id(1)
    nb = SCORE_COLS // LANES
    c_diag = qt // nb
    n_chunks = c_diag + 1
    n_pairs = (n_chunks + 1) // 2
    row_t = qt * TQ + lax.broadcasted_iota(I32, (TQ, LANES), 0)
    lane_c = lax.broadcasted_iota(I32, (TQ, LANES), 1)

    qi = qi_ref[...]
    sm = sm_ref[...]
    qp = []
    for h in range(IDX_HEADS):
        q_hi, q_lo = _split(qi[:, h * IDX_DIM:(h + 1) * IDX_DIM])
        qp.append(jnp.concatenate([q_hi, q_lo, q_hi, jnp.zeros_like(q_hi)], axis=-1))
    qp = jnp.concatenate(qp, axis=0)
    w_fold = (IDX_DIM ** -0.5) * (IDX_HEADS ** -0.5)
    ws = [jnp.broadcast_to(sm[:, SM_WI + h:SM_WI + h + 1] * w_fold, (TQ, LANES))
          for h in range(IDX_HEADS)]

    def score_pair(cp, carry):
        for half in range(2):
            c0 = pl.multiple_of((2 * cp + half) * SCORE_COLS, SCORE_COLS)
            d = _dot_t(qp, kp_ref[pl.ds(c0, SCORE_COLS), :])
            for a in range(nb):
                off = c0 + a * LANES
                s = jnp.zeros((TQ, LANES), F32)
                for h in range(IDX_HEADS):
                    s = s + jnp.maximum(d[h * TQ:(h + 1) * TQ, a * LANES:(a + 1) * LANES], 0.0) * ws[h]
                s = jnp.where(off + lane_c <= row_t, s + 0.0, NEG_INF)
                bits = lax.bitcast_convert_type(s, I32)
                key = bits ^ ((bits >> 31) & 0x7FFFFFFF)
                blk = (2 * cp + half) * nb + a
                keys_ref[blk] = key
                hi_ref[blk] = (key >> 16).astype(I16)
                lo_ref[blk] = ((key & 0xFFFF) + I16_MIN).astype(I16)
        return carry

    lax.fori_loop(0, n_pairs, score_pair, 0)

    def count16(ref, pred):
        def body(c, acc):
            for a in range(2 * nb):
                acc = acc + jnp.where(pred(ref[c * 2 * nb + a]), jnp.int16(1), jnp.int16(0))
            return acc
        acc = lax.fori_loop(0, n_pairs, body, jnp.zeros((TQ, LANES), I16))
        tot = jnp.sum(acc.astype(I32).astype(F32), axis=-1, keepdims=True)
        return jnp.broadcast_to(tot, (TQ, LANES))

    def search16(ref, k_need):
        c_nonneg = count16(ref, lambda blk: blk >= jnp.int16(0))
        ok0 = c_nonneg >= k_need
        th0 = jnp.where(ok0, 0, I16_MIN).astype(I32)
        above0 = jnp.where(ok0, 0.0, c_nonneg)

        def bit_step(it, carry):
            th, above = carry
            cand = th | (jnp.int32(1) << (14 - it))
            cand16 = cand.astype(I16)
            cnt = count16(ref, lambda blk: blk >= cand16)
            ok = cnt >= k_need
            return jnp.where(ok, cand, th), jnp.where(ok, above, cnt)

        return lax.fori_loop(0, 15, bit_step, (th0, above0))

    k_full = jnp.full((TQ, LANES), float(k_sel), F32)
    th_hi, above_hi = search16(hi_ref, k_full)
    th_hi16 = th_hi.astype(I16)
    k_lo = k_full - above_hi

    def bucket_chunk(c, carry):
        for a in range(nb):
            blk = c * nb + a
            lo_ref[blk] = jnp.where(hi_ref[blk] == th_hi16, lo_ref[blk], jnp.int16(I16_MIN))
        return carry

    lax.fori_loop(0, 2 * n_pairs, bucket_chunk, 0)
    th_lo, above_lo = search16(lo_ref, k_lo)
    need = k_lo - above_lo
    theta = (th_hi << 16) | ((th_lo - I16_MIN) & 0xFFFF)

    def sel_pair(cp, carry):
        pcs = []
        for half in range(2):
            tie = jnp.concatenate(
                [jnp.where(keys_ref[(2 * cp + half) * nb + a] == theta, 1.0, 0.0)
                 for a in range(nb)], axis=1).astype(BF16)
            pcs.append(_dot(tie, tri_ref[...]))
        for half in range(2):
            for a in range(nb):
                blk = (2 * cp + half) * nb + a
                key = keys_ref[blk]
                rank = pcs[half][:, a * LANES:(a + 1) * LANES] + carry
                tied_in = jnp.where(key == theta, jnp.where(rank <= need, 0.0, NEG_INF), NEG_INF)
                sb = jnp.where(key > theta, 0.0, tied_in)
                selb_ref[blk] = jnp.where(blk * LANES + lane_c <= row_t, sb, NEG_INF)
            carry = carry + pcs[half][:, SCORE_COLS:SCORE_COLS + LANES]
        return carry

    lax.fori_loop(0, n_pairs, sel_pair, jnp.zeros((TQ, LANES), F32))

    qb = qb_ref[...]
    scale = (DSA_HEAD_DIM ** -0.5) * LOG2_E
    for h in range(DSA_HEADS):
        qh = qb[:, h * DSA_HEAD_DIM:(h + 1) * DSA_HEAD_DIM].astype(BF16)
        ql_ref[h * TQ:(h + 1) * TQ, :] = (_dot(qh, wuk_ref[h]) * scale).astype(BF16)

    m_ref[...] = jnp.full(m_ref.shape, -1e30, F32)
    acc_ref[...] = jnp.zeros(acc_ref.shape, F32)

    def attend(c, near):
        c0 = pl.multiple_of(c * SCORE_COLS, SCORE_COLS)
        ckx = ckv_ref[pl.ds(c0, SCORE_COLS), :]
        ck = ckx[:, :KV_RANK]
        lg = _dot_t(ql_ref[...], ck)
        for h in range(DSA_HEADS):
            blk_max = None
            for a in range(nb):
                s = lg[h * TQ:(h + 1) * TQ, a * LANES:(a + 1) * LANES] + selb_ref[c * nb + a]
                if near:
                    s = s + bias_ref[h, jnp.clip(c * nb + a - qt + 2, 0, 3)]
                s_ref[h, a] = s
                blk_max = s if blk_max is None else jnp.maximum(blk_max, s)
            m_old = m_ref[h]
            m_new = jnp.maximum(m_old, jnp.broadcast_to(jnp.max(blk_max, axis=-1, keepdims=True),
                                                        (TQ, LANES)))
            al_ref[h] = jnp.exp2(m_old - m_new)
            m_ref[h] = m_new
        for g0 in range(0, DSA_HEADS, PV_GROUP):
            for h in range(g0, g0 + PV_GROUP):
                m_new = m_ref[h]
                for a in range(nb):
                    sl = slice(a * LANES, (a + 1) * LANES)
                    p_ref[h * TQ:(h + 1) * TQ, sl] = jnp.exp2(s_ref[h, a] - m_new).astype(BF16)
            pv = _dot(p_ref[g0 * TQ:(g0 + PV_GROUP) * TQ, :], ckx)
            for h in range(g0, g0 + PV_GROUP):
                alpha = al_ref[h]
                acc_ref[h] = (jnp.concatenate([alpha, alpha], axis=1) * acc_ref[h]
                              + pv[(h - g0) * TQ:(h - g0 + 1) * TQ])

    def far(c, carry):
        attend(c, False)
        return carry

    lax.fori_loop(0, jnp.maximum(c_diag - 1, 0), far, 0)

    @pl.when(c_diag > 0)
    def _():
        attend(c_diag - 1, True)

    attend(c_diag, True)

    out = jnp.zeros((TQ, DSA_WIDTH), F32)
    for h in range(DSA_HEADS):
        o_lat = acc_ref[h, :, :KV_RANK] / acc_ref[h, :, KV_RANK:]
        out = out + _dot(o_lat.astype(BF16), wuv_ref[h])
    o_ref[...] = out


def _rel_bucket_table(n):
    max_exact = REL_BUCKETS // 2
    d = np.arange(n)
    df = np.maximum(d, 1).astype(np.float32)
    large = max_exact + (np.log(df / np.float32(max_exact)) / np.float32(math.log(REL_MAX_DIST / max_exact))
                         * np.float32(REL_BUCKETS - max_exact)).astype(np.int32)
    large = np.minimum(large, REL_BUCKETS - 1)
    return np.where(d < max_exact, d, large)


def _dsa_attention(h_main, h_idx, ckv, kp, w_uk, w_uv, rel_bias, B, L):
    T = B * L
    nq = L // TQ
    k_sel = min(DSA_TOPK, L // 4)
    assert SCORE_COLS == 4 * LANES and SCORE_COLS >= k_sel
    dist = TQ + np.arange(TQ)[:, None] - np.arange(2 * TQ)[None, :]
    assert _rel_bucket_table(REL_MAX_DIST * 4)[TQ:].min() == REL_BUCKETS - 1
    bucket = _rel_bucket_table(2 * TQ + 1)[np.maximum(dist, 0)]
    rb = rel_bias.astype(F32)
    onehot = (jnp.asarray(bucket, I32)[:, :, None] == jnp.arange(REL_BUCKETS, dtype=I32)).astype(F32)
    near = jnp.einsum("qkb,bh->hqk", onehot, rb - rb[REL_BUCKETS - 1], precision=lax.Precision.HIGHEST)
    bias_near = near
    bias_near = jnp.pad(bias_near * LOG2_E, ((0, 0), (0, 0), (TQ, TQ)))
    bias_near = bias_near.reshape(DSA_HEADS, TQ, 4, TQ).transpose(0, 2, 1, 3)
    assert L % (2 * SCORE_COLS) == 0
    u = np.arange(SCORE_COLS)[:, None]
    v = np.arange(SCORE_COLS + LANES)[None, :]
    tri = jnp.asarray((u <= v) | (v >= SCORE_COLS), BF16)
    wuv_band = jnp.zeros((DSA_HEADS, KV_RANK, DSA_WIDTH), F32)
    for h in range(DSA_HEADS):
        wuv_band = wuv_band.at[h, :, h * DSA_HEAD_DIM:(h + 1) * DSA_HEAD_DIM].set(w_uv[h])
    kern = functools.partial(_dsa_kernel, k_sel=k_sel)
    lpad = -(-L // (2 * SCORE_COLS)) * (2 * SCORE_COLS)
    return pl.pallas_call(
        kern,
        grid=(B, nq),
        in_specs=[
            pl.BlockSpec((TQ, DSA_WIDTH), lambda b, i: (b * nq + i, 1)),
            pl.BlockSpec((TQ, IDX_HEADS * IDX_DIM), lambda b, i: (b * nq + i, 0)),
            pl.BlockSpec((TQ, LANES), lambda b, i: (b * nq + i, 2)),
            pl.BlockSpec((L, 4 * IDX_DIM), lambda b, i: (b, 0)),
            pl.BlockSpec((L, 2 * KV_RANK), lambda b, i: (b, 0)),
            pl.BlockSpec((DSA_HEADS, DSA_HEAD_DIM, KV_RANK), lambda b, i: (0, 0, 0)),
            pl.BlockSpec((DSA_HEADS, KV_RANK, DSA_WIDTH), lambda b, i: (0, 0, 0)),
            pl.BlockSpec((DSA_HEADS, 4, TQ, TQ), lambda b, i: (0, 0, 0, 0)),
            pl.BlockSpec((SCORE_COLS, SCORE_COLS + LANES), lambda b, i: (0, 0)),
        ],
        out_specs=pl.BlockSpec((TQ, DSA_WIDTH), lambda b, i: (b * nq + i, 0)),
        out_shape=jax.ShapeDtypeStruct((T, DSA_WIDTH), F32),
        scratch_shapes=[
            pltpu.VMEM((lpad // LANES, TQ, LANES), I32),
            pltpu.VMEM((lpad // LANES, TQ, LANES), I16),
            pltpu.VMEM((lpad // LANES, TQ, LANES), I16),
            pltpu.VMEM((lpad // LANES, TQ, LANES), F32),
            pltpu.VMEM((DSA_HEADS * TQ, KV_RANK), BF16),
            pltpu.VMEM((DSA_HEADS, SCORE_COLS // LANES, TQ, LANES), F32),
            pltpu.VMEM((DSA_HEADS * TQ, SCORE_COLS), BF16),
            pltpu.VMEM((DSA_HEADS, TQ, LANES), F32),
            pltpu.VMEM((DSA_HEADS, TQ, LANES), F32),
            pltpu.VMEM((DSA_HEADS, TQ, 2 * KV_RANK), F32),
        ],
        compiler_params=_cparams(("parallel", "arbitrary")),
        name="dsa_attention",
    )(h_main, h_idx, h_idx, kp, ckv, w_uk.astype(BF16), wuv_band.astype(BF16), bias_near, tri)


def _mlstm_prep_kernel(cur_ref, halo_ref, w_ref, b_ref, wqh_ref, wql_ref, wkh_ref, wkl_ref,
                       q_ref, k_ref, xbuf, *, tl):
    i = pl.program_id(1)
    xbuf[SUBLANES:SUBLANES + tl, :] = cur_ref[...]
    xbuf[0:SUBLANES, :] = jnp.where(i > 0, halo_ref[...], 0.0)
    base = SUBLANES - (MLSTM_CONV - 1)
    acc = jnp.broadcast_to(b_ref[...], (tl, MLSTM_WIDTH))
    for k in range(MLSTM_CONV):
        acc = acc + w_ref[k:k + 1, :] * xbuf[base + k:base + k + tl, :]
    xc = acc * jax.nn.sigmoid(acc)
    q_ref[...] = _dot3(xc, wqh_ref[...], wql_ref[...]) * (MLSTM_HEAD_DIM ** -0.5)
    k_ref[...] = _dot3(xc, wkh_ref[...], wkl_ref[...])


def _block_diag(w):
    h, d, e = w.shape
    out = jnp.zeros((h * d, h * e), F32)
    for i in range(h):
        out = out.at[i * d:(i + 1) * d, i * e:(i + 1) * e].set(w[i])
    return out


def _mlstm_prep(h_main, conv_w, conv_b, w_qm, w_km, B, L, tl=512):
    T = B * L
    nl = L // tl
    hb = tl // SUBLANES
    wq = _wsplit(_block_diag(w_qm))
    wk = _wsplit(_block_diag(w_km))
    full = lambda shape: pl.BlockSpec(shape, lambda b, i: (0,) * len(shape))
    return pl.pallas_call(
        functools.partial(_mlstm_prep_kernel, tl=tl),
        grid=(B, nl),
        in_specs=[
            pl.BlockSpec((tl, MLSTM_WIDTH), lambda b, i: (b * nl + i, 4)),
            pl.BlockSpec((SUBLANES, MLSTM_WIDTH),
                         lambda b, i: (jnp.maximum((b * nl + i) * hb - 1, 0), 4)),
            full((MLSTM_CONV, MLSTM_WIDTH)), full((1, MLSTM_WIDTH)),
            full((MLSTM_WIDTH, MLSTM_WIDTH)), full((MLSTM_WIDTH, MLSTM_WIDTH)),
            full((MLSTM_WIDTH, MLSTM_WIDTH)), full((MLSTM_WIDTH, MLSTM_WIDTH)),
        ],
        out_specs=[pl.BlockSpec((tl, MLSTM_WIDTH), lambda b, i: (b * nl + i, 0)),
                   pl.BlockSpec((tl, MLSTM_WIDTH), lambda b, i: (b * nl + i, 0))],
        out_shape=[jax.ShapeDtypeStruct((T, MLSTM_WIDTH), F32),
                   jax.ShapeDtypeStruct((T, MLSTM_WIDTH), F32)],
        scratch_shapes=[pltpu.VMEM((SUBLANES + tl, MLSTM_WIDTH), F32)],
        compiler_params=_cparams(("parallel", "parallel")),
        name="mlstm_prep",
    )(h_main, h_main, conv_w, conv_b.reshape(1, -1), wq[0], wq[1], wk[0], wk[1])


ML_ROWS = 128


def _mlstm_kernel(q_ref, k_ref, v_ref, o_ref, sm_ref, bi_ref, bf_ref, g_ref, out_ref,
                  cm_ref, n_ref, m_ref):
    C = MLSTM_CHUNK
    dh = MLSTM_HEAD_DIM

    @pl.when(pl.program_id(1) == 0)
    def _():
        cm_ref[...] = jnp.zeros(cm_ref.shape, F32)
        n_ref[...] = jnp.zeros(n_ref.shape, F32)
        m_ref[...] = jnp.zeros(m_ref.shape, F32)

    sm_t = sm_ref[...].T
    ig = sm_t[SM_I:SM_I + SUBLANES] + bi_ref[...]
    fg = jax.nn.log_sigmoid(sm_t[SM_F:SM_F + SUBLANES] + bf_ref[...])
    lane = lax.broadcasted_iota(I32, (SUBLANES, ML_ROWS), 1) & (C - 1)
    bcum = fg
    s = 1
    while s < C:
        bcum = bcum + jnp.where(lane >= s, pltpu.roll(bcum, s, axis=1), 0.0)
        s *= 2
    cols = jnp.concatenate([bcum, ig, jnp.zeros((LANES - 2 * SUBLANES, ML_ROWS), F32)], axis=0).T
    tri = (lax.broadcasted_iota(I32, (C, C), 1) <= lax.broadcasted_iota(I32, (C, C), 0))

    q_all, k_all, v_all, o_all = q_ref[...], k_ref[...], v_ref[...], o_ref[...]
    g_all = g_ref[...]
    for c in range(ML_ROWS // C):
        r0 = c * C
        for h in range(MLSTM_HEADS):
            hs = slice(h * dh, (h + 1) * dh)
            qj = q_all[r0:r0 + C, hs]
            kj = k_all[r0:r0 + C, hs]
            vj = v_all[r0:r0 + C, hs]
            b_row = bcum[h:h + 1, r0:r0 + C]
            i_row = ig[h:h + 1, r0:r0 + C]
            b_col = cols[r0:r0 + C, h:h + 1]
            i_col = cols[r0:r0 + C, SUBLANES + h:SUBLANES + h + 1]
            m_prev = m_ref[h:h + 1, 0:1]
            n_prev = n_ref[h:h + 1, :]
            cm_prev = cm_ref[h]

            dm = jnp.where(tri, b_col - b_row + i_row, NEG_INF)
            inter = b_col + m_prev
            m_row = jnp.maximum(inter, jnp.max(dm, axis=-1, keepdims=True))
            w_inter = jnp.exp(inter - m_row)
            qb, kb, vb = qj.astype(BF16), kj.astype(BF16), vj.astype(BF16)
            sw = _dot_t(qb, kb) * jnp.exp(dm - m_row)
            num = _dot(sw.astype(BF16), vb) + w_inter * _dot(qb, cm_prev.astype(BF16))
            den = (jnp.sum(sw, axis=-1, keepdims=True)
                   + w_inter * jnp.sum(qj * n_prev, axis=-1, keepdims=True))
            hh = num / jnp.maximum(jnp.abs(den), jnp.exp(-m_row))

            b_last = b_row[:, C - 1:C]
            g_row = b_last - b_row + i_row
            g_col = b_last - b_col + i_col
            m_new = jnp.maximum(b_last + m_prev, jnp.max(g_row, axis=-1, keepdims=True))
            decay = jnp.exp(b_last + m_prev - m_new)
            kw = kj * jnp.exp(g_col - m_new)
            cm_ref[h] = decay * cm_prev + _dot_tl(kw.astype(BF16), vb)
            n_ref[h:h + 1, :] = decay * n_prev + jnp.sum(kw, axis=0, keepdims=True)
            m_ref[h:h + 1, :] = jnp.broadcast_to(m_new, (1, LANES))

            mu = jnp.mean(hh, axis=-1, keepdims=True)
            hc = hh - mu
            var = jnp.mean(hc * hc, axis=-1, keepdims=True)
            hn = hc * lax.rsqrt(var + LN_EPS) * g_all[:, hs]
            out_ref[r0:r0 + C, hs] = jax.nn.sigmoid(o_all[r0:r0 + C, hs]) * hn


def _mlstm(q, k, h_main, h_idx, b_i, b_f, norm_g, B, L):
    T = B * L
    nl = L // ML_ROWS
    pad8 = lambda v: jnp.pad(v.astype(F32), (0, SUBLANES - MLSTM_HEADS)).reshape(SUBLANES, 1)
    row = lambda col: pl.BlockSpec((ML_ROWS, MLSTM_WIDTH), lambda b, i: (b * nl + i, col))
    return pl.pallas_call(
        _mlstm_kernel,
        grid=(B, nl),
        in_specs=[row(0), row(0), row(5), row(6),
                  pl.BlockSpec((ML_ROWS, LANES), lambda b, i: (b * nl + i, 2)),
                  pl.BlockSpec((SUBLANES, 1), lambda b, i: (0, 0)),
                  pl.BlockSpec((SUBLANES, 1), lambda b, i: (0, 0)),
                  pl.BlockSpec((1, MLSTM_WIDTH), lambda b, i: (0, 0))],
        out_specs=row(0),
        out_shape=jax.ShapeDtypeStruct((T, MLSTM_WIDTH), F32),
        scratch_shapes=[pltpu.VMEM((MLSTM_HEADS, MLSTM_HEAD_DIM, MLSTM_HEAD_DIM), F32),
                        pltpu.VMEM((SUBLANES, MLSTM_HEAD_DIM), F32),
                        pltpu.VMEM((SUBLANES, LANES), F32)],
        compiler_params=_cparams(("parallel", "arbitrary")),
        name="mlstm_scan",
    )(q, k, h_main, h_main, h_idx, pad8(b_i), pad8(b_f), norm_g.reshape(1, -1))


def _mix_out_kernel(ya_ref, yb_ref, yc_ref, x_ref, wa_ref, wb_ref, wc_ref, g_ref, b_ref, o_ref, *,
                    alpha):
    y = (_dot(ya_ref[...].astype(BF16), wa_ref[...])
         + _dot(yb_ref[...].astype(BF16), wb_ref[...])
         + _dot(yc_ref[...].astype(BF16), wc_ref[...]))
    o_ref[...] = _layer_norm(alpha * x_ref[...] + y, g_ref[...], b_ref[...])


def _mix_out(y_a, y_b, y_c, x, w_out, g, b, alpha, tm=512):
    T, D = x.shape
    w = w_out.astype(BF16)
    wa, wb, wc = w[:CONV_CH], w[CONV_CH:CONV_CH + DSA_WIDTH], w[CONV_CH + DSA_WIDTH:]
    rows = lambda width: pl.BlockSpec((tm, width), lambda i: (i, 0))
    full = lambda a: pl.BlockSpec(a.shape, lambda i: (0, 0))
    g2, b2 = g.reshape(1, D), b.reshape(1, D)
    return pl.pallas_call(
        functools.partial(_mix_out_kernel, alpha=alpha),
        grid=(T // tm,),
        in_specs=[rows(CONV_CH), rows(DSA_WIDTH), rows(MLSTM_WIDTH), rows(D),
                  full(wa), full(wb), full(wc), full(g2), full(b2)],
        out_specs=rows(D),
        out_shape=jax.ShapeDtypeStruct((T, D), F32),
        compiler_params=_cparams(("parallel",)),
        name="mix_out",
    )(y_a, y_b, y_c, x, wa, wb, wc, g2, b2)


def _xattn_kernel(x_ref, kv_ref, wq_ref, wo_ref, g_ref, b_ref, o_ref, *, alpha):
    x = x_ref[...]
    q = _dot(x.astype(BF16), wq_ref[...])
    kv = kv_ref[...]
    scale = XATTN_HEAD_DIM ** -0.5
    outs = []
    for h in range(XATTN_HEADS):
        hs = slice(h * XATTN_HEAD_DIM, (h + 1) * XATTN_HEAD_DIM)
        kh = kv[:, hs].astype(BF16)
        vh = kv[:, D_MODEL + h * XATTN_HEAD_DIM:D_MODEL + (h + 1) * XATTN_HEAD_DIM].astype(BF16)
        lg = _dot_t(q[:, hs].astype(BF16), kh) * scale
        lg = lg - jnp.max(lg, axis=-1, keepdims=True)
        p = jnp.exp(lg)
        p = p / jnp.sum(p, axis=-1, keepdims=True)
        outs.append(_dot(p.astype(BF16), vh))
    o = jnp.concatenate(outs, axis=-1)
    y = _dot(o.astype(BF16), wo_ref[...])
    o_ref[...] = _layer_norm(alpha * x + y, g_ref[...], b_ref[...])


def _xattn(x, kv, w_q, w_o, g, b, alpha, B, L, tm=256):
    T, D = x.shape
    nl = L // tm
    M = kv.shape[0] // B
    full = lambda a: pl.BlockSpec(a.shape, lambda bb, i: (0, 0))
    g2, b2 = g.reshape(1, D), b.reshape(1, D)
    wq, wo = w_q.astype(BF16), w_o.astype(BF16)
    return pl.pallas_call(
        functools.partial(_xattn_kernel, alpha=alpha),
        grid=(B, nl),
        in_specs=[pl.BlockSpec((tm, D), lambda bb, i: (bb * nl + i, 0)),
                  pl.BlockSpec((M, 2 * D), lambda bb, i: (bb, 0)),
                  full(wq), full(wo), full(g2), full(b2)],
        out_specs=pl.BlockSpec((tm, D), lambda bb, i: (bb * nl + i, 0)),
        out_shape=jax.ShapeDtypeStruct((T, D), F32),
        compiler_params=_cparams(("parallel", "parallel")),
        name="xattn",
    )(x, kv, wq, wo, g2, b2)


def _peer_score_kernel(x_ref, w_ref, k1h_ref, k1l_ref, k2h_ref, k2l_ref, st_ref):
    q = _dot(x_ref[...].astype(BF16), w_ref[...])
    half = PEER_QDIM // 2
    for h in range(PEER_HEADS):
        for part, (kh_ref, kl_ref) in enumerate(((k1h_ref, k1l_ref), (k2h_ref, k2l_ref))):
            c0 = h * PEER_QDIM + part * half
            q_hi, q_lo = _split(q[:, c0:c0 + half])
            kh, kl = kh_ref[...], kl_ref[...]
            st_ref[2 * h + part] = _dot_t(kh, q_hi) + _dot_t(kl, q_hi) + _dot_t(kh, q_lo)


def _peer_scores(x, w_pq, sub_k1, sub_k2, tm=256):
    T, D = x.shape
    w = w_pq.astype(BF16)
    k1h, k1l = _wsplit(sub_k1)
    k2h, k2l = _wsplit(sub_k2)
    full = lambda a: pl.BlockSpec(a.shape, lambda i: (0, 0))
    return pl.pallas_call(
        _peer_score_kernel,
        grid=(T // tm,),
        in_specs=[pl.BlockSpec((tm, D), lambda i: (i, 0)), full(w),
                  full(k1h), full(k1l), full(k2h), full(k2l)],
        out_specs=pl.BlockSpec((2 * PEER_HEADS, PEER_KEYS, tm), lambda i: (0, 0, i)),
        out_shape=jax.ShapeDtypeStruct((2 * PEER_HEADS, PEER_KEYS, T), F32),
        compiler_params=_cparams(("parallel",)),
        name="peer_scores",
    )(x, w, k1h, k1l, k2h, k2l)


PEER_NTOP = PEER_TOPK + 1
PEER_PAIR_ROWS = tuple(PEER_NTOP // (k + 1) for k in range(PEER_NTOP))
PEER_CAND_ROWS = -(-sum(PEER_PAIR_ROWS) // SUBLANES) * SUBLANES


def _peer_thr_kernel(st_ref, stats_ref, v2_ref, cand_ref):
    def top_rows(x):
        rows = []
        for _ in range(PEER_NTOP):
            m = jnp.max(x, axis=0, keepdims=True)
            rows.append(m)
            x = jnp.where(x == m, NEG_INF, x)
        return rows

    v1 = top_rows(st_ref[0])
    v2 = top_rows(st_ref[1])
    for k in range(PEER_NTOP):
        v2_ref[k:k + 1, :] = v2[k]
    r = 0
    for k, n in enumerate(PEER_PAIR_ROWS):
        cand_ref[r:r + n, :] = v1[k] + v2_ref[0:n, :]
        r += n
    cand_ref[r:PEER_CAND_ROWS, :] = jnp.full((PEER_CAND_ROWS - r, cand_ref.shape[1]), NEG_INF, F32)
    cand = cand_ref[...]
    x = cand
    for _ in range(PEER_TOPK - 1):
        m = jnp.max(x, axis=0, keepdims=True)
        x = jnp.where(x == m, NEG_INF, x)
    thr = jnp.max(x, axis=0, keepdims=True)
    nxt = jnp.max(jnp.where(x == thr, NEG_INF, x), axis=0, keepdims=True)
    top = v1[0] + v2[0]
    z = jnp.sum(jnp.where(cand >= thr, jnp.exp(cand - top), 0.0), axis=0, keepdims=True)
    cut = jnp.where(nxt > NEG_INF, 0.5 * thr + 0.5 * nxt, thr)
    pad = jnp.zeros((SUBLANES - 4, thr.shape[1]), F32)
    stats_ref[0] = jnp.concatenate([cut, v1[0], v2[0], 1.0 / z, pad], axis=0)


def _peer_thresholds(st, tm=256):
    T = st.shape[2]
    return pl.pallas_call(
        _peer_thr_kernel,
        grid=(PEER_HEADS, T // tm),
        in_specs=[pl.BlockSpec((2, PEER_KEYS, tm), lambda h, i: (h, 0, i))],
        out_specs=pl.BlockSpec((1, SUBLANES, tm), lambda h, i: (h, 0, i)),
        out_shape=jax.ShapeDtypeStruct((PEER_HEADS, SUBLANES, T), F32),
        scratch_shapes=[pltpu.VMEM((3 * SUBLANES, tm), F32),
                        pltpu.VMEM((PEER_CAND_ROWS, tm), F32)],
        compiler_params=_cparams(("parallel", "parallel")),
        name="peer_thresholds",
    )(st)


PEER_TL = 512
PEER_ET = 512
PEER_SUB = 64

def _peer_mix_kernel(x_ref, st_ref, stats_ref, u_ref, vt_ref, g_ref, b_ref, o_ref,
                     xb_ref, d1_ref, e1_ref, e2_ref, gate_ref, w_ref, acc_ref, *, alpha):
    j = pl.program_id(1)
    n_tiles = pl.num_programs(1) - 1
    sub_tiles = PEER_ET // PEER_KEYS

    @pl.when(j == 0)
    def _():
        xb_ref[...] = x_ref[...].astype(BF16)
        for h in range(PEER_HEADS):
            st = stats_ref[h]
            s1 = st_ref[2 * h]
            d1_ref[h] = st[0:1] - s1
            e1_ref[h] = jnp.exp(s1 - st[1:2]) * st[3:4]
            e2_ref[h] = jnp.exp(st_ref[2 * h + 1] - st[2:3])
        acc_ref[...] = jnp.zeros(acc_ref.shape, F32)
        w_ref[...] = jnp.zeros(w_ref.shape, BF16)

    slot = j % 2

    def gate_block(a, lb):
        i1 = jnp.minimum(j, n_tiles - 1) * sub_tiles + a
        ls = slice(lb * LANES, (lb + 1) * LANES)
        d_rows = [d1_ref[h, pl.ds(i1, 1), :][:, ls] for h in range(PEER_HEADS)]
        c_rows = [e1_ref[h, pl.ds(i1, 1), :][:, ls] for h in range(PEER_HEADS)]
        for k0 in range(0, PEER_KEYS, PEER_SUB):
            ks = slice(k0, k0 + PEER_SUB)
            gate = None
            for h in range(PEER_HEADS):
                picked = st_ref[2 * h + 1, ks, ls] >= d_rows[h]
                term = jnp.where(picked, e2_ref[h, ks, ls] * c_rows[h], 0.0)
                gate = term if gate is None else gate + term
            gate_ref[a * PEER_KEYS + k0:a * PEER_KEYS + k0 + PEER_SUB, ls] = gate

    for a in range(sub_tiles):
        for lb in range(PEER_TL // LANES):
            gate_block(a, lb)
    acc_ref[...] += _dot(vt_ref[...], w_ref[1 - slot])
    su = _dot_t(u_ref[...], xb_ref[...])
    act = 0.5 * su * (1.0 + lax.erf(su * (2.0 ** -0.5)))
    w_ref[slot] = (gate_ref[...] * act).astype(BF16)

    @pl.when(j == n_tiles)
    def _():
        y = acc_ref[...].T
        o_ref[...] = _layer_norm(alpha * x_ref[...] + y, g_ref[...], b_ref[...])


def _peer_mix(x, st, stats, peer_u, peer_v, g, b, alpha):
    T, D = x.shape
    E = peer_u.shape[0]
    u = peer_u.astype(BF16)
    vt = peer_v.astype(BF16).T
    g2, b2 = g.reshape(1, D), b.reshape(1, D)
    tl, et = PEER_TL, PEER_ET
    n_e = E // et
    return pl.pallas_call(
        functools.partial(_peer_mix_kernel, alpha=alpha),
        grid=(T // tl, n_e + 1),
        in_specs=[pl.BlockSpec((tl, D), lambda i, j: (i, 0)),
                  pl.BlockSpec((2 * PEER_HEADS, PEER_KEYS, tl), lambda i, j: (0, 0, i)),
                  pl.BlockSpec((PEER_HEADS, SUBLANES, tl), lambda i, j: (0, 0, i)),
                  pl.BlockSpec((et, D), lambda i, j: (jnp.minimum(j, n_e - 1), 0)),
                  pl.BlockSpec((D, et), lambda i, j: (0, jnp.maximum(j - 1, 0))),
                  pl.BlockSpec((1, D), lambda i, j: (0, 0)),
                  pl.BlockSpec((1, D), lambda i, j: (0, 0))],
        out_specs=pl.BlockSpec((tl, D), lambda i, j: (i, 0)),
        out_shape=jax.ShapeDtypeStruct((T, D), F32),
        scratch_shapes=[pltpu.VMEM((tl, D), BF16),
                        pltpu.VMEM((PEER_HEADS, PEER_KEYS, tl), F32),
                        pltpu.VMEM((PEER_HEADS, PEER_KEYS, tl), F32),
                        pltpu.VMEM((PEER_HEADS, PEER_KEYS, tl), F32),
                        pltpu.VMEM((et, tl), F32),
                        pltpu.VMEM((2, et, tl), BF16),
                        pltpu.VMEM((D, tl), F32)],
        compiler_params=_cparams(("parallel", "arbitrary")),
        name="peer_mix",
    )(x, st, stats, u, vt, g2, b2)


def _pack_w_in(w):
    sizes = (512, 512, 128, 256, 64, 4, 256, 256, 256, 4, 4)
    offs = np.concatenate([[0], np.cumsum(sizes)])
    seg = lambda n: w[:, int(offs[n]):int(offs[n + 1])]
    a_in, q_b, c_b, qi, ki, wi, xc, v_m, o_m, i_m, f_m = (seg(n) for n in range(len(sizes)))
    z = lambda n: jnp.zeros((w.shape[0], n), w.dtype)
    main = jnp.concatenate([a_in, q_b, xc, v_m, o_m, c_b], axis=1)
    idx = jnp.concatenate([qi, ki, wi, z(4), i_m, z(4), f_m, z(IDX_W - 256 - SM_F - 4)], axis=1)
    return main, idx


def kernel(x, mem, ln_in_g, ln_in_b, rel_bias, w_in, conv_a_w, conv_a_b, norm_a_g, norm_a_b,
           kv_norm_g, w_uk, w_uv, conv_m_w, conv_m_b, w_qm, w_km, b_i, b_f, norm_m_g, w_out,
           ln1_g, ln1_b, w_cq, w_ckv, w_co, ln2_g, ln2_b, w_pq, sub_k1, sub_k2, peer_u, peer_v,
           ln3_g, ln3_b):
    B, L, D = x.shape
    T = B * L
    depth = w_in.shape[0]
    alpha = (2.0 * depth) ** 0.25
    xs = _entry_ln(x.reshape(T, D), ln_in_g, ln_in_b)
    mem2 = mem.reshape(-1, D)
    for l in range(depth):
        w_main, w_idx = _pack_w_in(w_in[l])
        h_main = _matmul(xs, w_main, 1, 512, 640, "proj_main")
        h_idx = _matmul(xs, w_idx, 3, 512, IDX_W, "proj_idx")
        y_a = _conv_group(h_main, conv_a_w[l], conv_a_b[l], norm_a_g[l], norm_a_b[l], B, L)
        ckv, kp = _dsa_prep(h_main, h_idx, kv_norm_g[l])
        y_b = _dsa_attention(h_main, h_idx, ckv, kp, w_uk[l], w_uv[l], rel_bias, B, L)
        q_m, k_m = _mlstm_prep(h_main, conv_m_w[l], conv_m_b[l], w_qm[l], w_km[l], B, L)
        y_c = _mlstm(q_m, k_m, h_main, h_idx, b_i[l], b_f[l], norm_m_g[l], B, L)
        xs = _mix_out(y_a, y_b, y_c, xs, w_out[l], ln1_g[l], ln1_b[l], alpha)

        kv = _matmul(mem2, w_ckv[l], 1, mem2.shape[0], 512, "xattn_kv")
        xs = _xattn(xs, kv, w_cq[l], w_co[l], ln2_g[l], ln2_b[l], alpha, B, L)

        st = _peer_scores(xs, w_pq[l], sub_k1[l], sub_k2[l])
        stats = _peer_thresholds(st)
        xs = _peer_mix(xs, st, stats, peer_u[l], peer_v[l], ln3_g[l], ln3_b[l], alpha)
    return xs.reshape(B, L, D)
```

```python
import functools
import math

import numpy as np
import jax
import jax.numpy as jnp
from jax import lax
from jax.experimental import pallas as pl
from jax.experimental.pallas import tpu as pltpu

F32 = jnp.float32
BF16 = jnp.bfloat16
I32 = jnp.int32

D_MODEL = 1024
CONV_CH = 256
CONV_WIDTH = 31
DSA_HEADS = 8
DSA_HEAD_DIM = 64
DSA_WIDTH = 512
KV_RANK = 128
IDX_HEADS = 4
IDX_DIM = 64
DSA_TOPK = 256
MLSTM_HEADS = 4
MLSTM_HEAD_DIM = 64
MLSTM_WIDTH = 256
MLSTM_CONV = 4
MLSTM_CHUNK = 64
REL_BUCKETS = 32
REL_MAX_DIST = 128
XATTN_HEADS = 4
XATTN_HEAD_DIM = 256
PEER_HEADS = 8
PEER_KEYS = 128
PEER_QDIM = 256
PEER_TOPK = 16
LN_EPS = 1e-5

LANES = 128
SUBLANES = 8
VMEM_LIMIT = 56 * 1024 * 1024

NEG_INF = float("-inf")
LOG2_E = 1.4426950408889634

MAIN_W = 1920
IDX_W = 384
SM_WI = 64
SM_I = 72
SM_F = 80


def _cparams(sem):
    return pltpu.CompilerParams(dimension_semantics=sem, vmem_limit_bytes=VMEM_LIMIT)


def _dot(a, b):
    return jnp.dot(a, b, preferred_element_type=F32)


def _dot_t(a, b):
    return lax.dot_general(a, b, (((1,), (1,)), ((), ())), preferred_element_type=F32)


def _dot_tl(a, b):
    return lax.dot_general(a, b, (((0,), (0,)), ((), ())), preferred_element_type=F32)


def _split(a):
    hi = a.astype(BF16)
    lo = (a - hi.astype(F32)).astype(BF16)
    return hi, lo


def _dot3(a, b_hi, b_lo, dot=_dot):
    a_hi, a_lo = _split(a)
    return dot(a_hi, b_hi) + dot(a_lo, b_hi) + dot(a_hi, b_lo)


def _layer_norm(x, g, b):
    mu = jnp.mean(x, axis=-1, keepdims=True)
    xc = x - mu
    var = jnp.mean(xc * xc, axis=-1, keepdims=True)
    return xc * lax.rsqrt(var + LN_EPS) * g + b


def _wsplit(w):
    hi = w.astype(BF16)
    lo = (w - hi.astype(F32)).astype(BF16)
    return hi, lo


def _ln_kernel(x_ref, g_ref, b_ref, o_ref):
    o_ref[...] = _layer_norm(x_ref[...], g_ref[...], b_ref[...])


def _entry_ln(x, g, b, tm=512):
    T, D = x.shape
    return pl.pallas_call(
        _ln_kernel,
        grid=(T // tm,),
        in_specs=[pl.BlockSpec((tm, D), lambda i: (i, 0)),
                  pl.BlockSpec((1, D), lambda i: (0, 0)),
                  pl.BlockSpec((1, D), lambda i: (0, 0))],
        out_specs=pl.BlockSpec((tm, D), lambda i: (i, 0)),
        out_shape=jax.ShapeDtypeStruct((T, D), F32),
        compiler_params=_cparams(("parallel",)),
        name="entry_ln",
    )(x, g.reshape(1, D), b.reshape(1, D))


def _mm1_kernel(x_ref, w_ref, o_ref):
    o_ref[...] = _dot(x_ref[...].astype(BF16), w_ref[...])


def _mm3_kernel(x_ref, wh_ref, wl_ref, o_ref):
    o_ref[...] = _dot3(x_ref[...], wh_ref[...], wl_ref[...])


def _matmul(x, w, passes, tm, tn, name):
    T, K = x.shape
    N = w.shape[1]
    x_spec = pl.BlockSpec((tm, K), lambda j, i: (i, 0))
    w_spec = pl.BlockSpec((K, tn), lambda j, i: (0, j))
    if passes == 1:
        kern, ws, w_specs = _mm1_kernel, (w.astype(BF16),), [w_spec]
    else:
        kern, ws, w_specs = _mm3_kernel, _wsplit(w), [w_spec, w_spec]
    return pl.pallas_call(
        kern,
        grid=(N // tn, T // tm),
        in_specs=[x_spec] + w_specs,
        out_specs=pl.BlockSpec((tm, tn), lambda j, i: (i, j)),
        out_shape=jax.ShapeDtypeStruct((T, N), F32),
        compiler_params=_cparams(("parallel", "parallel")),
        name=name,
    )(x, *ws)


CONV_HALO = 32
CONV_ROWS = 64


def _conv_kernel(cur_ref, halo_ref, w_ref, b_ref, g_ref, bb_ref, o_ref, ubuf, *, tl):
    i = pl.program_id(1)
    cur = cur_ref[...]
    ubuf[CONV_HALO:CONV_HALO + tl, :] = cur[:, :CONV_CH] * jax.nn.sigmoid(cur[:, CONV_CH:])
    hal = halo_ref[...]
    uh = hal[:, :CONV_CH] * jax.nn.sigmoid(hal[:, CONV_CH:])
    ubuf[0:CONV_HALO, :] = jnp.where(i > 0, uh, 0.0)
    base = CONV_HALO - (CONV_WIDTH - 1)
    for c in range(tl // CONV_ROWS):
        r0 = c * CONV_ROWS
        acc = jnp.broadcast_to(b_ref[...], (CONV_ROWS, CONV_CH))
        for k in range(CONV_WIDTH):
            acc = acc + w_ref[k:k + 1, :] * ubuf[r0 + base + k:r0 + base + k + CONV_ROWS, :]
        y = _layer_norm(acc, g_ref[...], bb_ref[...])
        o_ref[r0:r0 + CONV_ROWS, :] = y * jax.nn.sigmoid(y)


def _conv_group(h_main, conv_w, conv_b, ln_g, ln_b, B, L, tl=256):
    T = B * L
    nl = L // tl
    hb = tl // CONV_HALO
    return pl.pallas_call(
        functools.partial(_conv_kernel, tl=tl),
        grid=(B, nl),
        in_specs=[
            pl.BlockSpec((tl, 2 * CONV_CH), lambda b, i: (b * nl + i, 0)),
            pl.BlockSpec((CONV_HALO, 2 * CONV_CH),
                         lambda b, i: (jnp.maximum((b * nl + i) * hb - 1, 0), 0)),
            pl.BlockSpec((CONV_WIDTH, CONV_CH), lambda b, i: (0, 0)),
            pl.BlockSpec((1, CONV_CH), lambda b, i: (0, 0)),
            pl.BlockSpec((1, CONV_CH), lambda b, i: (0, 0)),
            pl.BlockSpec((1, CONV_CH), lambda b, i: (0, 0)),
        ],
        out_specs=pl.BlockSpec((tl, CONV_CH), lambda b, i: (b * nl + i, 0)),
        out_shape=jax.ShapeDtypeStruct((T, CONV_CH), F32),
        scratch_shapes=[pltpu.VMEM((CONV_HALO + tl, CONV_CH), F32)],
        compiler_params=_cparams(("parallel", "parallel")),
        name="conv_group",
    )(h_main, h_main, conv_w, conv_b.reshape(1, -1), ln_g.reshape(1, -1), ln_b.reshape(1, -1))


def _dsa_prep_kernel(c_ref, sm_ref, g_ref, ckv_ref, kp_ref):
    c = c_ref[...]
    ms = jnp.mean(c * c, axis=-1, keepdims=True)
    ckv = (c * lax.rsqrt(ms + LN_EPS) * g_ref[...]).astype(BF16)
    ckv_ref[...] = jnp.concatenate([ckv, jnp.ones_like(ckv)], axis=-1)
    k_hi, k_lo = _split(sm_ref[...][:, :IDX_DIM])
    kp_ref[...] = jnp.concatenate([k_hi, k_hi, k_lo, jnp.zeros_like(k_hi)], axis=-1)


def _dsa_prep(h_main, h_idx, kv_g, tm=512):
    T = h_main.shape[0]
    return pl.pallas_call(
        _dsa_prep_kernel,
        grid=(T // tm,),
        in_specs=[pl.BlockSpec((tm, KV_RANK), lambda i: (i, 14)),
                  pl.BlockSpec((tm, LANES), lambda i: (i, 2)),
                  pl.BlockSpec((1, KV_RANK), lambda i: (0, 0))],
        out_specs=[pl.BlockSpec((tm, 2 * KV_RANK), lambda i: (i, 0)),
                   pl.BlockSpec((tm, 4 * IDX_DIM), lambda i: (i, 0))],
        out_shape=[jax.ShapeDtypeStruct((T, 2 * KV_RANK), BF16),
                   jax.ShapeDtypeStruct((T, 4 * IDX_DIM), BF16)],
        compiler_params=_cparams(("parallel",)),
        name="dsa_prep",
    )(h_main, h_idx, kv_g.reshape(1, -1))


TQ = 256
SCORE_COLS = 512
PV_GROUP = 1


I16 = jnp.int16
I16_MIN = -(2 ** 15)


def _dsa_kernel(qb_ref, qi_ref, sm_ref, kp_ref, ckv_ref, wuk_ref, wuv_ref, bias_ref, tri_ref, o_ref,
                hi_ref, lo_ref, selb_ref, ql_ref, s_ref, p_ref, m_ref, al_ref, acc_ref, *, k_sel):
    qt = pl.program_id(1)
    nb = SCORE_COLS // LANES
    c_diag = ((qt + 1) * TQ - 1) // SCORE_COLS
    n_chunks = c_diag + 1
    n_pairs = (n_chunks + 1) // 2
    row_t = qt * TQ + lax.broadcasted_iota(I32, (TQ, LANES), 0)
    lane_c = lax.broadcasted_iota(I32, (TQ, LANES), 1)

    qi = qi_ref[...]
    sm = sm_ref[...]
    qp = []
    for h in range(IDX_HEADS):
        q_hi, q_lo = _split(qi[:, h * IDX_DIM:(h + 1) * IDX_DIM])
        qp.append(jnp.concatenate([q_hi, q_lo, q_hi, jnp.zeros_like(q_hi)], axis=-1))
    qp = jnp.concatenate(qp, axis=0)
    w_fold = (IDX_DIM ** -0.5) * (IDX_HEADS ** -0.5)
    ws = [jnp.broadcast_to(sm[:, SM_WI + h:SM_WI + h + 1] * w_fold, (TQ, LANES))
          for h in range(IDX_HEADS)]

    def score_pair(cp, carry):
        for half in range(2):
            c0 = pl.multiple_of((2 * cp + half) * SCORE_COLS, SCORE_COLS)
            d = _dot_t(qp, kp_ref[pl.ds(c0, SCORE_COLS), :])
            for a in range(nb):
                off = c0 + a * LANES
                s = jnp.zeros((TQ, LANES), F32)
                for h in range(IDX_HEADS):
                    s = s + jnp.maximum(d[h * TQ:(h + 1) * TQ, a * LANES:(a + 1) * LANES], 0.0) * ws[h]
                s = jnp.where(off + lane_c <= row_t, s + 0.0, NEG_INF)
                bits = lax.bitcast_convert_type(s, I32)
                key = bits ^ ((bits >> 31) & 0x7FFFFFFF)
                blk = (2 * cp + half) * nb + a
                hi_ref[blk] = (key >> 16).astype(I16)
                lo_ref[blk] = ((key & 0xFFFF) + I16_MIN).astype(I16)
        return carry

    lax.fori_loop(0, n_pairs, score_pair, 0)

    def count16(ref, pred):
        def body(c, acc):
            for a in range(2 * nb):
                acc = acc + jnp.where(pred(ref[c * 2 * nb + a]), jnp.int16(1), jnp.int16(0))
            return acc
        acc = lax.fori_loop(0, n_pairs, body, jnp.zeros((TQ, LANES), I16))
        tot = jnp.sum(acc.astype(I32).astype(F32), axis=-1, keepdims=True)
        return jnp.broadcast_to(tot, (TQ, LANES))

    def search16(ref, k_need):
        c_nonneg = count16(ref, lambda blk: blk >= jnp.int16(0))
        ok0 = c_nonneg >= k_need
        th0 = jnp.where(ok0, 0, I16_MIN).astype(I32)
        above0 = jnp.where(ok0, 0.0, c_nonneg)

        def bit_step(it, carry):
            th, above = carry
            cand = th | (jnp.int32(1) << (14 - it))
            cand16 = cand.astype(I16)
            cnt = count16(ref, lambda blk: blk >= cand16)
            ok = cnt >= k_need
            return jnp.where(ok, cand, th), jnp.where(ok, above, cnt)

        return lax.fori_loop(0, 15, bit_step, (th0, above0))

    k_full = jnp.full((TQ, LANES), float(k_sel), F32)
    th_hi, above_hi = search16(hi_ref, k_full)
    th_hi16 = th_hi.astype(I16)
    k_lo = k_full - above_hi

    def bucket_chunk(c, carry):
        for a in range(nb):
            blk = c * nb + a
            lo_ref[blk] = jnp.where(hi_ref[blk] == th_hi16, lo_ref[blk], jnp.int16(I16_MIN))
        return carry

    lax.fori_loop(0, 2 * n_pairs, bucket_chunk, 0)
    th_lo, above_lo = search16(lo_ref, k_lo)
    need = k_lo - above_lo
    th_lo16 = th_lo.astype(I16)
    one16, zero16 = jnp.int16(1), jnp.int16(0)

    def flags(blk):
        hi, lom = hi_ref[blk], lo_ref[blk]
        in_bucket = hi == th_hi16
        above = jnp.where(hi > th_hi16, one16, jnp.where(lom > th_lo16, one16, zero16))
        tied = jnp.where(in_bucket, jnp.where(lom == th_lo16, one16, zero16), zero16)
        return above.astype(I32).astype(F32), tied.astype(I32).astype(F32)

    def sel_pair(cp, carry):
        pcs, fl = [], []
        for half in range(2):
            fl.append([flags((2 * cp + half) * nb + a) for a in range(nb)])
            tie = jnp.concatenate([t for _, t in fl[half]], axis=1).astype(BF16)
            pcs.append(_dot(tie, tri_ref[...]))
        for half in range(2):
            for a in range(nb):
                blk = (2 * cp + half) * nb + a
                above, tied = fl[half][a]
                rank = pcs[half][:, a * LANES:(a + 1) * LANES] + carry
                tied_in = jnp.where(tied > 0.0, jnp.where(rank <= need, 0.0, NEG_INF), NEG_INF)
                sb = jnp.where(above > 0.0, 0.0, tied_in)
                selb_ref[blk] = jnp.where(blk * LANES + lane_c <= row_t, sb, NEG_INF)
            carry = carry + pcs[half][:, SCORE_COLS:SCORE_COLS + LANES]
        return carry

    lax.fori_loop(0, n_pairs, sel_pair, jnp.zeros((TQ, LANES), F32))

    qb = qb_ref[...]
    scale = (DSA_HEAD_DIM ** -0.5) * LOG2_E
    for h in range(DSA_HEADS):
        qh = qb[:, h * DSA_HEAD_DIM:(h + 1) * DSA_HEAD_DIM].astype(BF16)
        ql_ref[h * TQ:(h + 1) * TQ, :] = (_dot(qh, wuk_ref[h]) * scale).astype(BF16)

    m_ref[...] = jnp.full(m_ref.shape, -1e30, F32)
    acc_ref[...] = jnp.zeros(acc_ref.shape, F32)

    def attend(c, near):
        c0 = pl.multiple_of(c * SCORE_COLS, SCORE_COLS)
        ckx = ckv_ref[pl.ds(c0, SCORE_COLS), :]
        ck = ckx[:, :KV_RANK]
        lg = _dot_t(ql_ref[...], ck)
        for h in range(DSA_HEADS):
            blk_max = None
            for a in range(nb):
                s = lg[h * TQ:(h + 1) * TQ, a * LANES:(a + 1) * LANES] + selb_ref[c * nb + a]
                if near:
                    q_blocks = TQ // LANES
                    s = s + jnp.concatenate(
                        [bias_ref[h, jnp.clip(c * nb + a - (qt * q_blocks + r) + 2, 0, 3)]
                         for r in range(q_blocks)], axis=0)
                s_ref[h, a] = s
                blk_max = s if blk_max is None else jnp.maximum(blk_max, s)
            m_old = m_ref[h]
            m_new = jnp.maximum(m_old, jnp.broadcast_to(jnp.max(blk_max, axis=-1, keepdims=True),
                                                        (TQ, LANES)))
            al_ref[h] = jnp.exp2(m_old - m_new)
            m_ref[h] = m_new
        for g0 in range(0, DSA_HEADS, PV_GROUP):
            for h in range(g0, g0 + PV_GROUP):
                m_new = m_ref[h]
                for a in range(nb):
                    sl = slice(a * LANES, (a + 1) * LANES)
                    p_ref[h * TQ:(h + 1) * TQ, sl] = jnp.exp2(s_ref[h, a] - m_new).astype(BF16)
            pv = _dot(p_ref[g0 * TQ:(g0 + PV_GROUP) * TQ, :], ckx)
            for h in range(g0, g0 + PV_GROUP):
                alpha = al_ref[h]
                acc_ref[h] = (jnp.concatenate([alpha, alpha], axis=1) * acc_ref[h]
                              + pv[(h - g0) * TQ:(h - g0 + 1) * TQ])

    def far(c, carry):
        attend(c, False)
        return carry

    lax.fori_loop(0, jnp.maximum(c_diag - 1, 0), far, 0)

    @pl.when(c_diag > 0)
    def _():
        attend(c_diag - 1, True)

    attend(c_diag, True)

    out = jnp.zeros((TQ, DSA_WIDTH), F32)
    for h in range(DSA_HEADS):
        o_lat = acc_ref[h, :, :KV_RANK] / acc_ref[h, :, KV_RANK:]
        out = out + _dot(o_lat.astype(BF16), wuv_ref[h])
    o_ref[...] = out


def _rel_bucket_table(n):
    max_exact = REL_BUCKETS // 2
    d = np.arange(n)
    df = np.maximum(d, 1).astype(np.float32)
    large = max_exact + (np.log(df / np.float32(max_exact)) / np.float32(math.log(REL_MAX_DIST / max_exact))
                         * np.float32(REL_BUCKETS - max_exact)).astype(np.int32)
    large = np.minimum(large, REL_BUCKETS - 1)
    return np.where(d < max_exact, d, large)


def _dsa_attention(h_main, h_idx, ckv, kp, w_uk, w_uv, rel_bias, B, L):
    T = B * L
    nq = L // TQ
    k_sel = min(DSA_TOPK, L // 4)
    assert SCORE_COLS == 4 * LANES and SCORE_COLS >= k_sel
    blk = LANES
    dist = blk + np.arange(blk)[:, None] - np.arange(2 * blk)[None, :]
    assert _rel_bucket_table(REL_MAX_DIST * 4)[blk:].min() == REL_BUCKETS - 1
    bucket = _rel_bucket_table(2 * blk + 1)[np.maximum(dist, 0)]
    rb = rel_bias.astype(F32)
    onehot = (jnp.asarray(bucket, I32)[:, :, None] == jnp.arange(REL_BUCKETS, dtype=I32)).astype(F32)
    near = jnp.einsum("qkb,bh->hqk", onehot, rb - rb[REL_BUCKETS - 1], precision=lax.Precision.HIGHEST)
    bias_near = jnp.pad(near * LOG2_E, ((0, 0), (0, 0), (blk, blk)))
    bias_near = bias_near.reshape(DSA_HEADS, blk, 4, blk).transpose(0, 2, 1, 3)
    assert L % (2 * SCORE_COLS) == 0 and TQ % LANES == 0 and SCORE_COLS % TQ == 0
    once = pl.Buffered(1)
    u = np.arange(SCORE_COLS)[:, None]
    v = np.arange(SCORE_COLS + LANES)[None, :]
    tri = jnp.asarray((u <= v) | (v >= SCORE_COLS), BF16)
    wuv_band = jnp.zeros((DSA_HEADS, KV_RANK, DSA_WIDTH), F32)
    for h in range(DSA_HEADS):
        wuv_band = wuv_band.at[h, :, h * DSA_HEAD_DIM:(h + 1) * DSA_HEAD_DIM].set(w_uv[h])
    kern = functools.partial(_dsa_kernel, k_sel=k_sel)
    lpad = -(-L // (2 * SCORE_COLS)) * (2 * SCORE_COLS)
    return pl.pallas_call(
        kern,
        grid=(B, nq),
        in_specs=[
            pl.BlockSpec((TQ, DSA_WIDTH), lambda b, i: (b * nq + i, 1)),
            pl.BlockSpec((TQ, IDX_HEADS * IDX_DIM), lambda b, i: (b * nq + i, 0)),
            pl.BlockSpec((TQ, LANES), lambda b, i: (b * nq + i, 2)),
            pl.BlockSpec((L, 4 * IDX_DIM), lambda b, i: (b, 0), pipeline_mode=once),
            pl.BlockSpec((L, 2 * KV_RANK), lambda b, i: (b, 0), pipeline_mode=once),
            pl.BlockSpec((DSA_HEADS, DSA_HEAD_DIM, KV_RANK), lambda b, i: (0, 0, 0)),
            pl.BlockSpec((DSA_HEADS, KV_RANK, DSA_WIDTH), lambda b, i: (0, 0, 0)),
            pl.BlockSpec((DSA_HEADS, 4, blk, blk), lambda b, i: (0, 0, 0, 0), pipeline_mode=once),
            pl.BlockSpec((SCORE_COLS, SCORE_COLS + LANES), lambda b, i: (0, 0)),
        ],
        out_specs=pl.BlockSpec((TQ, DSA_WIDTH), lambda b, i: (b * nq + i, 0)),
        out_shape=jax.ShapeDtypeStruct((T, DSA_WIDTH), F32),
        scratch_shapes=[
            pltpu.VMEM((lpad // LANES, TQ, LANES), I16),
            pltpu.VMEM((lpad // LANES, TQ, LANES), I16),
            pltpu.VMEM((lpad // LANES, TQ, LANES), F32),
            pltpu.VMEM((DSA_HEADS * TQ, KV_RANK), BF16),
            pltpu.VMEM((DSA_HEADS, SCORE_COLS // LANES, TQ, LANES), F32),
            pltpu.VMEM((DSA_HEADS * TQ, SCORE_COLS), BF16),
            pltpu.VMEM((DSA_HEADS, TQ, LANES), F32),
            pltpu.VMEM((DSA_HEADS, TQ, LANES), F32),
            pltpu.VMEM((DSA_HEADS, TQ, 2 * KV_RANK), F32),
        ],
        compiler_params=_cparams(("parallel", "arbitrary")),
        name="dsa_attention",
    )(h_main, h_idx, h_idx, kp, ckv, w_uk.astype(BF16), wuv_band.astype(BF16), bias_near, tri)


def _mlstm_prep_kernel(cur_ref, halo_ref, w_ref, b_ref, wqh_ref, wql_ref, wkh_ref, wkl_ref,
                       q_ref, k_ref, xbuf, *, tl):
    i = pl.program_id(1)
    xbuf[SUBLANES:SUBLANES + tl, :] = cur_ref[...]
    xbuf[0:SUBLANES, :] = jnp.where(i > 0, halo_ref[...], 0.0)
    base = SUBLANES - (MLSTM_CONV - 1)
    acc = jnp.broadcast_to(b_ref[...], (tl, MLSTM_WIDTH))
    for k in range(MLSTM_CONV):
        acc = acc + w_ref[k:k + 1, :] * xbuf[base + k:base + k + tl, :]
    xc = acc * jax.nn.sigmoid(acc)
    q_ref[...] = _dot3(xc, wqh_ref[...], wql_ref[...]) * (MLSTM_HEAD_DIM ** -0.5)
    k_ref[...] = _dot3(xc, wkh_ref[...], wkl_ref[...])


def _block_diag(w):
    h, d, e = w.shape
    out = jnp.zeros((h * d, h * e), F32)
    for i in range(h):
        out = out.at[i * d:(i + 1) * d, i * e:(i + 1) * e].set(w[i])
    return out


def _mlstm_prep(h_main, conv_w, conv_b, w_qm, w_km, B, L, tl=512):
    T = B * L
    nl = L // tl
    hb = tl // SUBLANES
    wq = _wsplit(_block_diag(w_qm))
    wk = _wsplit(_block_diag(w_km))
    full = lambda shape: pl.BlockSpec(shape, lambda b, i: (0,) * len(shape))
    return pl.pallas_call(
        functools.partial(_mlstm_prep_kernel, tl=tl),
        grid=(B, nl),
        in_specs=[
            pl.BlockSpec((tl, MLSTM_WIDTH), lambda b, i: (b * nl + i, 4)),
            pl.BlockSpec((SUBLANES, MLSTM_WIDTH),
                         lambda b, i: (jnp.maximum((b * nl + i) * hb - 1, 0), 4)),
            full((MLSTM_CONV, MLSTM_WIDTH)), full((1, MLSTM_WIDTH)),
            full((MLSTM_WIDTH, MLSTM_WIDTH)), full((MLSTM_WIDTH, MLSTM_WIDTH)),
            full((MLSTM_WIDTH, MLSTM_WIDTH)), full((MLSTM_WIDTH, MLSTM_WIDTH)),
        ],
        out_specs=[pl.BlockSpec((tl, MLSTM_WIDTH), lambda b, i: (b * nl + i, 0)),
                   pl.BlockSpec((tl, MLSTM_WIDTH), lambda b, i: (b * nl + i, 0))],
        out_shape=[jax.ShapeDtypeStruct((T, MLSTM_WIDTH), F32),
                   jax.ShapeDtypeStruct((T, MLSTM_WIDTH), F32)],
        scratch_shapes=[pltpu.VMEM((SUBLANES + tl, MLSTM_WIDTH), F32)],
        compiler_params=_cparams(("parallel", "parallel")),
        name="mlstm_prep",
    )(h_main, h_main, conv_w, conv_b.reshape(1, -1), wq[0], wq[1], wk[0], wk[1])


ML_ROWS = 128


def _mlstm_kernel(q_ref, k_ref, v_ref, o_ref, sm_ref, bi_ref, bf_ref, g_ref, out_ref,
                  cm_ref, n_ref, m_ref):
    C = MLSTM_CHUNK
    dh = MLSTM_HEAD_DIM

    @pl.when(pl.program_id(1) == 0)
    def _():
        cm_ref[...] = jnp.zeros(cm_ref.shape, F32)
        n_ref[...] = jnp.zeros(n_ref.shape, F32)
        m_ref[...] = jnp.zeros(m_ref.shape, F32)

    sm_t = sm_ref[...].T
    ig = sm_t[SM_I:SM_I + SUBLANES] + bi_ref[...]
    fg = jax.nn.log_sigmoid(sm_t[SM_F:SM_F + SUBLANES] + bf_ref[...])
    lane = lax.broadcasted_iota(I32, (SUBLANES, ML_ROWS), 1) & (C - 1)
    bcum = fg
    s = 1
    while s < C:
        bcum = bcum + jnp.where(lane >= s, pltpu.roll(bcum, s, axis=1), 0.0)
        s *= 2
    cols = jnp.concatenate([bcum, ig, jnp.zeros((LANES - 2 * SUBLANES, ML_ROWS), F32)], axis=0).T
    tri = (lax.broadcasted_iota(I32, (C, C), 1) <= lax.broadcasted_iota(I32, (C, C), 0))

    q_all, k_all, v_all, o_all = q_ref[...], k_ref[...], v_ref[...], o_ref[...]
    g_all = g_ref[...]
    for c in range(ML_ROWS // C):
        r0 = c * C
        for h in range(MLSTM_HEADS):
            hs = slice(h * dh, (h + 1) * dh)
            qj = q_all[r0:r0 + C, hs]
            kj = k_all[r0:r0 + C, hs]
            vj = v_all[r0:r0 + C, hs]
            b_row = bcum[h:h + 1, r0:r0 + C]
            i_row = ig[h:h + 1, r0:r0 + C]
            b_col = cols[r0:r0 + C, h:h + 1]
            i_col = cols[r0:r0 + C, SUBLANES + h:SUBLANES + h + 1]
            m_prev = m_ref[h:h + 1, 0:1]
            n_prev = n_ref[h:h + 1, :]
            cm_prev = cm_ref[h]

            dm = jnp.where(tri, b_col - b_row + i_row, NEG_INF)
            inter = b_col + m_prev
            m_row = jnp.maximum(inter, jnp.max(dm, axis=-1, keepdims=True))
            w_inter = jnp.exp(inter - m_row)
            qb, kb, vb = qj.astype(BF16), kj.astype(BF16), vj.astype(BF16)
            sw = _dot_t(qb, kb) * jnp.exp(dm - m_row)
            num = _dot(sw.astype(BF16), vb) + w_inter * _dot(qb, cm_prev.astype(BF16))
            den = (jnp.sum(sw, axis=-1, keepdims=True)
                   + w_inter * jnp.sum(qj * n_prev, axis=-1, keepdims=True))
            hh = num / jnp.maximum(jnp.abs(den), jnp.exp(-m_row))

            b_last = b_row[:, C - 1:C]
            g_row = b_last - b_row + i_row
            g_col = b_last - b_col + i_col
            m_new = jnp.maximum(b_last + m_prev, jnp.max(g_row, axis=-1, keepdims=True))
            decay = jnp.exp(b_last + m_prev - m_new)
            kw = kj * jnp.exp(g_col - m_new)
            cm_ref[h] = decay * cm_prev + _dot_tl(kw.astype(BF16), vb)
            n_ref[h:h + 1, :] = decay * n_prev + jnp.sum(kw, axis=0, keepdims=True)
            m_ref[h:h + 1, :] = jnp.broadcast_to(m_new, (1, LANES))

            mu = jnp.mean(hh, axis=-1, keepdims=True)
            hc = hh - mu
            var = jnp.mean(hc * hc, axis=-1, keepdims=True)
            hn = hc * lax.rsqrt(var + LN_EPS) * g_all[:, hs]
            out_ref[r0:r0 + C, hs] = jax.nn.sigmoid(o_all[r0:r0 + C, hs]) * hn


def _mlstm(q, k, h_main, h_idx, b_i, b_f, norm_g, B, L):
    T = B * L
    nl = L // ML_ROWS
    pad8 = lambda v: jnp.pad(v.astype(F32), (0, SUBLANES - MLSTM_HEADS)).reshape(SUBLANES, 1)
    row = lambda col: pl.BlockSpec((ML_ROWS, MLSTM_WIDTH), lambda b, i: (b * nl + i, col))
    return pl.pallas_call(
        _mlstm_kernel,
        grid=(B, nl),
        in_specs=[row(0), row(0), row(5), row(6),
                  pl.BlockSpec((ML_ROWS, LANES), lambda b, i: (b * nl + i, 2)),
                  pl.BlockSpec((SUBLANES, 1), lambda b, i: (0, 0)),
                  pl.BlockSpec((SUBLANES, 1), lambda b, i: (0, 0)),
                  pl.BlockSpec((1, MLSTM_WIDTH), lambda b, i: (0, 0))],
        out_specs=row(0),
        out_shape=jax.ShapeDtypeStruct((T, MLSTM_WIDTH), F32),
        scratch_shapes=[pltpu.VMEM((MLSTM_HEADS, MLSTM_HEAD_DIM, MLSTM_HEAD_DIM), F32),
                        pltpu.VMEM((SUBLANES, MLSTM_HEAD_DIM), F32),
                        pltpu.VMEM((SUBLANES, LANES), F32)],
        compiler_params=_cparams(("parallel", "arbitrary")),
        name="mlstm_scan",
    )(q, k, h_main, h_main, h_idx, pad8(b_i), pad8(b_f), norm_g.reshape(1, -1))


def _mix_out_kernel(ya_ref, yb_ref, yc_ref, x_ref, wa_ref, wb_ref, wc_ref, g_ref, b_ref, o_ref, *,
                    alpha):
    y = (_dot(ya_ref[...].astype(BF16), wa_ref[...])
         + _dot(yb_ref[...].astype(BF16), wb_ref[...])
         + _dot(yc_ref[...].astype(BF16), wc_ref[...]))
    o_ref[...] = _layer_norm(alpha * x_ref[...] + y, g_ref[...], b_ref[...])


def _mix_out(y_a, y_b, y_c, x, w_out, g, b, alpha, tm=512):
    T, D = x.shape
    w = w_out.astype(BF16)
    wa, wb, wc = w[:CONV_CH], w[CONV_CH:CONV_CH + DSA_WIDTH], w[CONV_CH + DSA_WIDTH:]
    rows = lambda width: pl.BlockSpec((tm, width), lambda i: (i, 0))
    full = lambda a: pl.BlockSpec(a.shape, lambda i: (0, 0))
    g2, b2 = g.reshape(1, D), b.reshape(1, D)
    return pl.pallas_call(
        functools.partial(_mix_out_kernel, alpha=alpha),
        grid=(T // tm,),
        in_specs=[rows(CONV_CH), rows(DSA_WIDTH), rows(MLSTM_WIDTH), rows(D),
                  full(wa), full(wb), full(wc), full(g2), full(b2)],
        out_specs=rows(D),
        out_shape=jax.ShapeDtypeStruct((T, D), F32),
        compiler_params=_cparams(("parallel",)),
        name="mix_out",
    )(y_a, y_b, y_c, x, wa, wb, wc, g2, b2)


def _xattn_kernel(x_ref, kv_ref, wq_ref, wo_ref, g_ref, b_ref, o_ref, *, alpha):
    x = x_ref[...]
    q = _dot(x.astype(BF16), wq_ref[...])
    kv = kv_ref[...]
    scale = XATTN_HEAD_DIM ** -0.5
    outs = []
    for h in range(XATTN_HEADS):
        hs = slice(h * XATTN_HEAD_DIM, (h + 1) * XATTN_HEAD_DIM)
        kh = kv[:, hs].astype(BF16)
        vh = kv[:, D_MODEL + h * XATTN_HEAD_DIM:D_MODEL + (h + 1) * XATTN_HEAD_DIM].astype(BF16)
        lg = _dot_t(q[:, hs].astype(BF16), kh) * scale
        lg = lg - jnp.max(lg, axis=-1, keepdims=True)
        p = jnp.exp(lg)
        p = p / jnp.sum(p, axis=-1, keepdims=True)
        outs.append(_dot(p.astype(BF16), vh))
    o = jnp.concatenate(outs, axis=-1)
    y = _dot(o.astype(BF16), wo_ref[...])
    o_ref[...] = _layer_norm(alpha * x + y, g_ref[...], b_ref[...])


def _xattn(x, kv, w_q, w_o, g, b, alpha, B, L, tm=256):
    T, D = x.shape
    nl = L // tm
    M = kv.shape[0] // B
    full = lambda a: pl.BlockSpec(a.shape, lambda bb, i: (0, 0))
    g2, b2 = g.reshape(1, D), b.reshape(1, D)
    wq, wo = w_q.astype(BF16), w_o.astype(BF16)
    return pl.pallas_call(
        functools.partial(_xattn_kernel, alpha=alpha),
        grid=(B, nl),
        in_specs=[pl.BlockSpec((tm, D), lambda bb, i: (bb * nl + i, 0)),
                  pl.BlockSpec((M, 2 * D), lambda bb, i: (bb, 0)),
                  full(wq), full(wo), full(g2), full(b2)],
        out_specs=pl.BlockSpec((tm, D), lambda bb, i: (bb * nl + i, 0)),
        out_shape=jax.ShapeDtypeStruct((T, D), F32),
        compiler_params=_cparams(("parallel", "parallel")),
        name="xattn",
    )(x, kv, wq, wo, g2, b2)


def _peer_score_kernel(x_ref, w_ref, k1h_ref, k1l_ref, k2h_ref, k2l_ref, st_ref):
    q = _dot(x_ref[...].astype(BF16), w_ref[...])
    half = PEER_QDIM // 2
    for h in range(PEER_HEADS):
        for part, (kh_ref, kl_ref) in enumerate(((k1h_ref, k1l_ref), (k2h_ref, k2l_ref))):
            c0 = h * PEER_QDIM + part * half
            q_hi, q_lo = _split(q[:, c0:c0 + half])
            kh, kl = kh_ref[...], kl_ref[...]
            st_ref[2 * h + part] = _dot_t(kh, q_hi) + _dot_t(kl, q_hi) + _dot_t(kh, q_lo)


def _peer_scores(x, w_pq, sub_k1, sub_k2, tm=256):
    T, D = x.shape
    w = w_pq.astype(BF16)
    k1h, k1l = _wsplit(sub_k1)
    k2h, k2l = _wsplit(sub_k2)
    full = lambda a: pl.BlockSpec(a.shape, lambda i: (0, 0))
    return pl.pallas_call(
        _peer_score_kernel,
        grid=(T // tm,),
        in_specs=[pl.BlockSpec((tm, D), lambda i: (i, 0)), full(w),
                  full(k1h), full(k1l), full(k2h), full(k2l)],
        out_specs=pl.BlockSpec((2 * PEER_HEADS, PEER_KEYS, tm), lambda i: (0, 0, i)),
        out_shape=jax.ShapeDtypeStruct((2 * PEER_HEADS, PEER_KEYS, T), F32),
        compiler_params=_cparams(("parallel",)),
        name="peer_scores",
    )(x, w, k1h, k1l, k2h, k2l)


PEER_NTOP = PEER_TOPK + 1
PEER_PAIR_ROWS = tuple(PEER_NTOP // (k + 1) for k in range(PEER_NTOP))
PEER_CAND_ROWS = -(-sum(PEER_PAIR_ROWS) // SUBLANES) * SUBLANES


def _peer_thr_kernel(st_ref, stats_ref, v2_ref, cand_ref):
    def top_rows(x):
        rows = []
        for _ in range(PEER_NTOP):
            m = jnp.max(x, axis=0, keepdims=True)
            rows.append(m)
            x = jnp.where(x == m, NEG_INF, x)
        return rows

    v1 = top_rows(st_ref[0])
    v2 = top_rows(st_ref[1])
    for k in range(PEER_NTOP):
        v2_ref[k:k + 1, :] = v2[k]
    r = 0
    for k, n in enumerate(PEER_PAIR_ROWS):
        cand_ref[r:r + n, :] = v1[k] + v2_ref[0:n, :]
        r += n
    cand_ref[r:PEER_CAND_ROWS, :] = jnp.full((PEER_CAND_ROWS - r, cand_ref.shape[1]), NEG_INF, F32)
    cand = cand_ref[...]
    x = cand
    for _ in range(PEER_TOPK - 1):
        m = jnp.max(x, axis=0, keepdims=True)
        x = jnp.where(x == m, NEG_INF, x)
    thr = jnp.max(x, axis=0, keepdims=True)
    nxt = jnp.max(jnp.where(x == thr, NEG_INF, x), axis=0, keepdims=True)
    top = v1[0] + v2[0]
    z = jnp.sum(jnp.where(cand >= thr, jnp.exp(cand - top), 0.0), axis=0, keepdims=True)
    cut = jnp.where(nxt > NEG_INF, 0.5 * thr + 0.5 * nxt, thr)
    pad = jnp.zeros((SUBLANES - 4, thr.shape[1]), F32)
    stats_ref[0] = jnp.concatenate([cut, v1[0], v2[0], 1.0 / z, pad], axis=0)


def _peer_thresholds(st, tm=256):
    T = st.shape[2]
    return pl.pallas_call(
        _peer_thr_kernel,
        grid=(PEER_HEADS, T // tm),
        in_specs=[pl.BlockSpec((2, PEER_KEYS, tm), lambda h, i: (h, 0, i))],
        out_specs=pl.BlockSpec((1, SUBLANES, tm), lambda h, i: (h, 0, i)),
        out_shape=jax.ShapeDtypeStruct((PEER_HEADS, SUBLANES, T), F32),
        scratch_shapes=[pltpu.VMEM((3 * SUBLANES, tm), F32),
                        pltpu.VMEM((PEER_CAND_ROWS, tm), F32)],
        compiler_params=_cparams(("parallel", "parallel")),
        name="peer_thresholds",
    )(st)


PEER_TL = 512
PEER_ET = 512
PEER_SUB = 64

def _peer_mix_kernel(x_ref, st_ref, stats_ref, u_ref, vt_ref, g_ref, b_ref, o_ref,
                     xb_ref, d1_ref, e1_ref, e2_ref, gate_ref, w_ref, acc_ref, *, alpha):
    j = pl.program_id(1)
    n_tiles = pl.num_programs(1) - 1
    sub_tiles = PEER_ET // PEER_KEYS

    @pl.when(j == 0)
    def _():
        xb_ref[...] = x_ref[...].astype(BF16)
        for h in range(PEER_HEADS):
            st = stats_ref[h]
            s1 = st_ref[2 * h]
            d1_ref[h] = st[0:1] - s1
            e1_ref[h] = jnp.exp(s1 - st[1:2]) * st[3:4]
            e2_ref[h] = jnp.exp(st_ref[2 * h + 1] - st[2:3])
        acc_ref[...] = jnp.zeros(acc_ref.shape, F32)
        w_ref[...] = jnp.zeros(w_ref.shape, BF16)

    slot = j % 2

    def gate_block(a, lb):
        i1 = jnp.minimum(j, n_tiles - 1) * sub_tiles + a
        ls = slice(lb * LANES, (lb + 1) * LANES)
        d_rows = [d1_ref[h, pl.ds(i1, 1), :][:, ls] for h in range(PEER_HEADS)]
        c_rows = [e1_ref[h, pl.ds(i1, 1), :][:, ls] for h in range(PEER_HEADS)]
        for k0 in range(0, PEER_KEYS, PEER_SUB):
            ks = slice(k0, k0 + PEER_SUB)
            gate = None
            for h in range(PEER_HEADS):
                picked = st_ref[2 * h + 1, ks, ls] >= d_rows[h]
                term = jnp.where(picked, e2_ref[h, ks, ls] * c_rows[h], 0.0)
                gate = term if gate is None else gate + term
            gate_ref[a * PEER_KEYS + k0:a * PEER_KEYS + k0 + PEER_SUB, ls] = gate

    for a in range(sub_tiles):
        for lb in range(PEER_TL // LANES):
            gate_block(a, lb)
    acc_ref[...] += _dot(vt_ref[...], w_ref[1 - slot])
    su = _dot_t(u_ref[...], xb_ref[...])
    act = 0.5 * su * (1.0 + lax.erf(su * (2.0 ** -0.5)))
    w_ref[slot] = (gate_ref[...] * act).astype(BF16)

    @pl.when(j == n_tiles)
    def _():
        y = acc_ref[...].T
        o_ref[...] = _layer_norm(alpha * x_ref[...] + y, g_ref[...], b_ref[...])


def _peer_mix(x, st, stats, peer_u, peer_v, g, b, alpha):
    T, D = x.shape
    E = peer_u.shape[0]
    u = peer_u.astype(BF16)
    vt = peer_v.astype(BF16).T
    g2, b2 = g.reshape(1, D), b.reshape(1, D)
    tl, et = PEER_TL, PEER_ET
    n_e = E // et
    return pl.pallas_call(
        functools.partial(_peer_mix_kernel, alpha=alpha),
        grid=(T // tl, n_e + 1),
        in_specs=[pl.BlockSpec((tl, D), lambda i, j: (i, 0)),
                  pl.BlockSpec((2 * PEER_HEADS, PEER_KEYS, tl), lambda i, j: (0, 0, i)),
                  pl.BlockSpec((PEER_HEADS, SUBLANES, tl), lambda i, j: (0, 0, i)),
                  pl.BlockSpec((et, D), lambda i, j: (jnp.minimum(j, n_e - 1), 0)),
                  pl.BlockSpec((D, et), lambda i, j: (0, jnp.maximum(j - 1, 0))),
                  pl.BlockSpec((1, D), lambda i, j: (0, 0)),
                  pl.BlockSpec((1, D), lambda i, j: (0, 0))],
        out_specs=pl.BlockSpec((tl, D), lambda i, j: (i, 0)),
        out_shape=jax.ShapeDtypeStruct((T, D), F32),
        scratch_shapes=[pltpu.VMEM((tl, D), BF16),
                        pltpu.VMEM((PEER_HEADS, PEER_KEYS, tl), F32),
                        pltpu.VMEM((PEER_HEADS, PEER_KEYS, tl), F32),
                        pltpu.VMEM((PEER_HEADS, PEER_KEYS, tl), F32),
                        pltpu.VMEM((et, tl), F32),
                        pltpu.VMEM((2, et, tl), BF16),
                        pltpu.VMEM((D, tl), F32)],
        compiler_params=_cparams(("parallel", "arbitrary")),
        name="peer_mix",
    )(x, st, stats, u, vt, g2, b2)


def _pack_w_in(w):
    sizes = (512, 512, 128, 256, 64, 4, 256, 256, 256, 4, 4)
    offs = np.concatenate([[0], np.cumsum(sizes)])
    seg = lambda n: w[:, int(offs[n]):int(offs[n + 1])]
    a_in, q_b, c_b, qi, ki, wi, xc, v_m, o_m, i_m, f_m = (seg(n) for n in range(len(sizes)))
    z = lambda n: jnp.zeros((w.shape[0], n), w.dtype)
    main = jnp.concatenate([a_in, q_b, xc, v_m, o_m, c_b], axis=1)
    idx = jnp.concatenate([qi, ki, wi, z(4), i_m, z(4), f_m, z(IDX_W - 256 - SM_F - 4)], axis=1)
    return main, idx


def kernel(x, mem, ln_in_g, ln_in_b, rel_bias, w_in, conv_a_w, conv_a_b, norm_a_g, norm_a_b,
           kv_norm_g, w_uk, w_uv, conv_m_w, conv_m_b, w_qm, w_km, b_i, b_f, norm_m_g, w_out,
           ln1_g, ln1_b, w_cq, w_ckv, w_co, ln2_g, ln2_b, w_pq, sub_k1, sub_k2, peer_u, peer_v,
           ln3_g, ln3_b):
    B, L, D = x.shape
    T = B * L
    depth = w_in.shape[0]
    alpha = (2.0 * depth) ** 0.25
    xs = _entry_ln(x.reshape(T, D), ln_in_g, ln_in_b)
    mem2 = mem.reshape(-1, D)
    for l in range(depth):
        w_main, w_idx = _pack_w_in(w_in[l])
        h_main = _matmul(xs, w_main, 1, 512, 640, "proj_main")
        h_idx = _matmul(xs, w_idx, 3, 512, IDX_W, "proj_idx")
        y_a = _conv_group(h_main, conv_a_w[l], conv_a_b[l], norm_a_g[l], norm_a_b[l], B, L)
        ckv, kp = _dsa_prep(h_main, h_idx, kv_norm_g[l])
        y_b = _dsa_attention(h_main, h_idx, ckv, kp, w_uk[l], w_uv[l], rel_bias, B, L)
        q_m, k_m = _mlstm_prep(h_main, conv_m_w[l], conv_m_b[l], w_qm[l], w_km[l], B, L)
        y_c = _mlstm(q_m, k_m, h_main, h_idx, b_i[l], b_f[l], norm_m_g[l], B, L)
        xs = _mix_out(y_a, y_b, y_c, xs, w_out[l], ln1_g[l], ln1_b[l], alpha)

        kv = _matmul(mem2, w_ckv[l], 1, mem2.shape[0], 512, "xattn_kv")
        xs = _xattn(xs, kv, w_cq[l], w_co[l], ln2_g[l], ln2_b[l], alpha, B, L)

        st = _peer_scores(xs, w_pq[l], sub_k1[l], sub_k2[l])
        stats = _peer_thresholds(st)
        xs = _peer_mix(xs, st, stats, peer_u[l], peer_v[l], ln3_g[l], ln3_b[l], alpha)
    return xs.reshape(B, L, D)
```

```python
import functools
import math

import numpy as np
import jax
import jax.numpy as jnp
from jax import lax
from jax.experimental import pallas as pl
from jax.experimental.pallas import tpu as pltpu

F32 = jnp.float32
BF16 = jnp.bfloat16
I32 = jnp.int32

D_MODEL = 1024
CONV_CH = 256
CONV_WIDTH = 31
DSA_HEADS = 8
DSA_HEAD_DIM = 64
DSA_WIDTH = 512
KV_RANK = 128
IDX_HEADS = 4
IDX_DIM = 64
DSA_TOPK = 256
MLSTM_HEADS = 4
MLSTM_HEAD_DIM = 64
MLSTM_WIDTH = 256
MLSTM_CONV = 4
MLSTM_CHUNK = 64
REL_BUCKETS = 32
REL_MAX_DIST = 128
XATTN_HEADS = 4
XATTN_HEAD_DIM = 256
PEER_HEADS = 8
PEER_KEYS = 128
PEER_QDIM = 256
PEER_TOPK = 16
LN_EPS = 1e-5

LANES = 128
SUBLANES = 8
VMEM_LIMIT = 56 * 1024 * 1024

NEG_INF = float("-inf")
LOG2_E = 1.4426950408889634

MAIN_W = 1920
IDX_W = 384
SM_WI = 64
SM_I = 72
SM_F = 80


def _cparams(sem):
    return pltpu.CompilerParams(dimension_semantics=sem, vmem_limit_bytes=VMEM_LIMIT)


def _dot(a, b):
    return jnp.dot(a, b, preferred_element_type=F32)


def _dot_t(a, b):
    return lax.dot_general(a, b, (((1,), (1,)), ((), ())), preferred_element_type=F32)


def _dot_tl(a, b):
    return lax.dot_general(a, b, (((0,), (0,)), ((), ())), preferred_element_type=F32)


def _split(a):
    hi = a.astype(BF16)
    lo = (a - hi.astype(F32)).astype(BF16)
    return hi, lo


def _dot3(a, b_hi, b_lo, dot=_dot):
    a_hi, a_lo = _split(a)
    return dot(a_hi, b_hi) + dot(a_lo, b_hi) + dot(a_hi, b_lo)


def _layer_norm(x, g, b):
    mu = jnp.mean(x, axis=-1, keepdims=True)
    xc = x - mu
    var = jnp.mean(xc * xc, axis=-1, keepdims=True)
    return xc * lax.rsqrt(var + LN_EPS) * g + b


def _wsplit(w):
    hi = w.astype(BF16)
    lo = (w - hi.astype(F32)).astype(BF16)
    return hi, lo


def _ln_kernel(x_ref, g_ref, b_ref, o_ref):
    o_ref[...] = _layer_norm(x_ref[...], g_ref[...], b_ref[...])


def _entry_ln(x, g, b, tm=512):
    T, D = x.shape
    return pl.pallas_call(
        _ln_kernel,
        grid=(T // tm,),
        in_specs=[pl.BlockSpec((tm, D), lambda i: (i, 0)),
                  pl.BlockSpec((1, D), lambda i: (0, 0)),
                  pl.BlockSpec((1, D), lambda i: (0, 0))],
        out_specs=pl.BlockSpec((tm, D), lambda i: (i, 0)),
        out_shape=jax.ShapeDtypeStruct((T, D), F32),
        compiler_params=_cparams(("parallel",)),
        name="entry_ln",
    )(x, g.reshape(1, D), b.reshape(1, D))


def _mm1_kernel(x_ref, w_ref, o_ref):
    o_ref[...] = _dot(x_ref[...].astype(BF16), w_ref[...])


def _mm3_kernel(x_ref, wh_ref, wl_ref, o_ref):
    o_ref[...] = _dot3(x_ref[...], wh_ref[...], wl_ref[...])


def _matmul(x, w, passes, tm, tn, name):
    T, K = x.shape
    N = w.shape[1]
    x_spec = pl.BlockSpec((tm, K), lambda j, i: (i, 0))
    w_spec = pl.BlockSpec((K, tn), lambda j, i: (0, j))
    if passes == 1:
        kern, ws, w_specs = _mm1_kernel, (w.astype(BF16),), [w_spec]
    else:
        kern, ws, w_specs = _mm3_kernel, _wsplit(w), [w_spec, w_spec]
    return pl.pallas_call(
        kern,
        grid=(N // tn, T // tm),
        in_specs=[x_spec] + w_specs,
        out_specs=pl.BlockSpec((tm, tn), lambda j, i: (i, j)),
        out_shape=jax.ShapeDtypeStruct((T, N), F32),
        compiler_params=_cparams(("parallel", "parallel")),
        name=name,
    )(x, *ws)


CONV_HALO = 32
CONV_ROWS = 64


def _conv_kernel(cur_ref, halo_ref, w_ref, b_ref, g_ref, bb_ref, o_ref, ubuf, *, tl):
    i = pl.program_id(1)
    cur = cur_ref[...]
    ubuf[CONV_HALO:CONV_HALO + tl, :] = cur[:, :CONV_CH] * jax.nn.sigmoid(cur[:, CONV_CH:])
    hal = halo_ref[...]
    uh = hal[:, :CONV_CH] * jax.nn.sigmoid(hal[:, CONV_CH:])
    ubuf[0:CONV_HALO, :] = jnp.where(i > 0, uh, 0.0)
    base = CONV_HALO - (CONV_WIDTH - 1)
    for c in range(tl // CONV_ROWS):
        r0 = c * CONV_ROWS
        acc = jnp.broadcast_to(b_ref[...], (CONV_ROWS, CONV_CH))
        for k in range(CONV_WIDTH):
            acc = acc + w_ref[k:k + 1, :] * ubuf[r0 + base + k:r0 + base + k + CONV_ROWS, :]
        y = _layer_norm(acc, g_ref[...], bb_ref[...])
        o_ref[r0:r0 + CONV_ROWS, :] = y * jax.nn.sigmoid(y)


def _conv_group(h_main, conv_w, conv_b, ln_g, ln_b, B, L, tl=256):
    T = B * L
    nl = L // tl
    hb = tl // CONV_HALO
    return pl.pallas_call(
        functools.partial(_conv_kernel, tl=tl),
        grid=(B, nl),
        in_specs=[
            pl.BlockSpec((tl, 2 * CONV_CH), lambda b, i: (b * nl + i, 0)),
            pl.BlockSpec((CONV_HALO, 2 * CONV_CH),
                         lambda b, i: (jnp.maximum((b * nl + i) * hb - 1, 0), 0)),
            pl.BlockSpec((CONV_WIDTH, CONV_CH), lambda b, i: (0, 0)),
            pl.BlockSpec((1, CONV_CH), lambda b, i: (0, 0)),
            pl.BlockSpec((1, CONV_CH), lambda b, i: (0, 0)),
            pl.BlockSpec((1, CONV_CH), lambda b, i: (0, 0)),
        ],
        out_specs=pl.BlockSpec((tl, CONV_CH), lambda b, i: (b * nl + i, 0)),
        out_shape=jax.ShapeDtypeStruct((T, CONV_CH), F32),
        scratch_shapes=[pltpu.VMEM((CONV_HALO + tl, CONV_CH), F32)],
        compiler_params=_cparams(("parallel", "parallel")),
        name="conv_group",
    )(h_main, h_main, conv_w, conv_b.reshape(1, -1), ln_g.reshape(1, -1), ln_b.reshape(1, -1))


def _dsa_prep_kernel(c_ref, sm_ref, g_ref, ckv_ref, kp_ref):
    c = c_ref[...]
    ms = jnp.mean(c * c, axis=-1, keepdims=True)
    ckv = (c * lax.rsqrt(ms + LN_EPS) * g_ref[...]).astype(BF16)
    ckv_ref[...] = jnp.concatenate([ckv, jnp.ones_like(ckv)], axis=-1)
    k_hi, k_lo = _split(sm_ref[...][:, :IDX_DIM])
    kp_ref[...] = jnp.concatenate([k_hi, k_hi, k_lo, jnp.zeros_like(k_hi)], axis=-1)


def _dsa_prep(h_main, h_idx, kv_g, tm=512):
    T = h_main.shape[0]
    return pl.pallas_call(
        _dsa_prep_kernel,
        grid=(T // tm,),
        in_specs=[pl.BlockSpec((tm, KV_RANK), lambda i: (i, 14)),
                  pl.BlockSpec((tm, LANES), lambda i: (i, 2)),
                  pl.BlockSpec((1, KV_RANK), lambda i: (0, 0))],
        out_specs=[pl.BlockSpec((tm, 2 * KV_RANK), lambda i: (i, 0)),
                   pl.BlockSpec((tm, 4 * IDX_DIM), lambda i: (i, 0))],
        out_shape=[jax.ShapeDtypeStruct((T, 2 * KV_RANK), BF16),
                   jax.ShapeDtypeStruct((T, 4 * IDX_DIM), BF16)],
        compiler_params=_cparams(("parallel",)),
        name="dsa_prep",
    )(h_main, h_idx, kv_g.reshape(1, -1))


TQ = 128
SCORE_COLS = 512
PV_GROUP = 1


I16 = jnp.int16
I16_MIN = -(2 ** 15)


def _dsa_kernel(qb_ref, qi_ref, sm_ref, kp_ref, ckv_ref, wuk_ref, wuv_ref, bias_ref, tri_ref, o_ref,
                hi_ref, lo_ref, selb_ref, ql_ref, s_ref, p_ref, m_ref, al_ref, acc_ref, *, k_sel):
    qt = pl.program_id(1)
    nb = SCORE_COLS // LANES
    c_diag = ((qt + 1) * TQ - 1) // SCORE_COLS
    n_chunks = c_diag + 1
    n_pairs = (n_chunks + 1) // 2
    row_t = qt * TQ + lax.broadcasted_iota(I32, (TQ, LANES), 0)
    lane_c = lax.broadcasted_iota(I32, (TQ, LANES), 1)

    qi = qi_ref[...]
    sm = sm_ref[...]
    qp = []
    for h in range(IDX_HEADS):
        q_hi, q_lo = _split(qi[:, h * IDX_DIM:(h + 1) * IDX_DIM])
        qp.append(jnp.concatenate([q_hi, q_lo, q_hi, jnp.zeros_like(q_hi)], axis=-1))
    qp = jnp.concatenate(qp, axis=0)
    w_fold = (IDX_DIM ** -0.5) * (IDX_HEADS ** -0.5)
    ws = [jnp.broadcast_to(sm[:, SM_WI + h:SM_WI + h + 1] * w_fold, (TQ, LANES))
          for h in range(IDX_HEADS)]

    def score_pair(cp, carry):
        for half in range(2):
            c0 = pl.multiple_of((2 * cp + half) * SCORE_COLS, SCORE_COLS)
            d = _dot_t(qp, kp_ref[pl.ds(c0, SCORE_COLS), :])
            for a in range(nb):
                off = c0 + a * LANES
                s = jnp.zeros((TQ, LANES), F32)
                for h in range(IDX_HEADS):
                    s = s + jnp.maximum(d[h * TQ:(h + 1) * TQ, a * LANES:(a + 1) * LANES], 0.0) * ws[h]
                s = jnp.where(off + lane_c <= row_t, s + 0.0, NEG_INF)
                bits = lax.bitcast_convert_type(s, I32)
                key = bits ^ ((bits >> 31) & 0x7FFFFFFF)
                blk = (2 * cp + half) * nb + a
                hi_ref[blk] = (key >> 16).astype(I16)
                lo_ref[blk] = ((key & 0xFFFF) + I16_MIN).astype(I16)
        return carry

    lax.fori_loop(0, n_pairs, score_pair, 0)

    def count16(ref, pred):
        def body(c, acc):
            for a in range(2 * nb):
                acc = acc + jnp.where(pred(ref[c * 2 * nb + a]), jnp.int16(1), jnp.int16(0))
            return acc
        acc = lax.fori_loop(0, n_pairs, body, jnp.zeros((TQ, LANES), I16))
        tot = jnp.sum(acc.astype(I32).astype(F32), axis=-1, keepdims=True)
        return jnp.broadcast_to(tot, (TQ, LANES))

    def search16(ref, k_need):
        c_nonneg = count16(ref, lambda blk: blk >= jnp.int16(0))
        ok0 = c_nonneg >= k_need
        th0 = jnp.where(ok0, 0, I16_MIN).astype(I32)
        above0 = jnp.where(ok0, 0.0, c_nonneg)

        def bit_step(it, carry):
            th, above = carry
            cand = th | (jnp.int32(1) << (14 - it))
            cand16 = cand.astype(I16)
            cnt = count16(ref, lambda blk: blk >= cand16)
            ok = cnt >= k_need
            return jnp.where(ok, cand, th), jnp.where(ok, above, cnt)

        return lax.fori_loop(0, 15, bit_step, (th0, above0))

    k_full = jnp.full((TQ, LANES), float(k_sel), F32)
    th_hi, above_hi = search16(hi_ref, k_full)
    th_hi16 = th_hi.astype(I16)
    k_lo = k_full - above_hi

    def bucket_chunk(c, carry):
        for a in range(nb):
            blk = c * nb + a
            lo_ref[blk] = jnp.where(hi_ref[blk] == th_hi16, lo_ref[blk], jnp.int16(I16_MIN))
        return carry

    lax.fori_loop(0, 2 * n_pairs, bucket_chunk, 0)
    th_lo, above_lo = search16(lo_ref, k_lo)
    need = k_lo - above_lo
    th_lo16 = th_lo.astype(I16)
    one16, zero16 = jnp.int16(1), jnp.int16(0)

    def flags(blk):
        hi, lom = hi_ref[blk], lo_ref[blk]
        in_bucket = hi == th_hi16
        above = jnp.where(hi > th_hi16, one16, jnp.where(lom > th_lo16, one16, zero16))
        tied = jnp.where(in_bucket, jnp.where(lom == th_lo16, one16, zero16), zero16)
        return above.astype(I32).astype(F32), tied.astype(I32).astype(F32)

    def sel_pair(cp, carry):
        pcs, fl = [], []
        for half in range(2):
            fl.append([flags((2 * cp + half) * nb + a) for a in range(nb)])
            tie = jnp.concatenate([t for _, t in fl[half]], axis=1).astype(BF16)
            pcs.append(_dot(tie, tri_ref[...]))
        for half in range(2):
            for a in range(nb):
                blk = (2 * cp + half) * nb + a
                above, tied = fl[half][a]
                rank = pcs[half][:, a * LANES:(a + 1) * LANES] + carry
                tied_in = jnp.where(tied > 0.0, jnp.where(rank <= need, 0.0, NEG_INF), NEG_INF)
                sb = jnp.where(above > 0.0, 0.0, tied_in)
                selb_ref[blk] = jnp.where(blk * LANES + lane_c <= row_t, sb, NEG_INF)
            carry = carry + pcs[half][:, SCORE_COLS:SCORE_COLS + LANES]
        return carry

    lax.fori_loop(0, n_pairs, sel_pair, jnp.zeros((TQ, LANES), F32))

    qb = qb_ref[...]
    scale = (DSA_HEAD_DIM ** -0.5) * LOG2_E
    for h in range(DSA_HEADS):
        qh = qb[:, h * DSA_HEAD_DIM:(h + 1) * DSA_HEAD_DIM].astype(BF16)
        ql_ref[h * TQ:(h + 1) * TQ, :] = (_dot(qh, wuk_ref[h]) * scale).astype(BF16)

    m_ref[...] = jnp.full(m_ref.shape, -1e30, F32)
    acc_ref[...] = jnp.zeros(acc_ref.shape, F32)

    def attend(c, near):
        c0 = pl.multiple_of(c * SCORE_COLS, SCORE_COLS)
        ckx = ckv_ref[pl.ds(c0, SCORE_COLS), :]
        ck = ckx[:, :KV_RANK]
        lg = _dot_t(ql_ref[...], ck)
        for h in range(DSA_HEADS):
            blk_max = None
            for a in range(nb):
                s = lg[h * TQ:(h + 1) * TQ, a * LANES:(a + 1) * LANES] + selb_ref[c * nb + a]
                if near:
                    q_blocks = TQ // LANES
                    s = s + jnp.concatenate(
                        [bias_ref[h, jnp.clip(c * nb + a - (qt * q_blocks + r) + 2, 0, 3)]
                         for r in range(q_blocks)], axis=0)
                s_ref[h, a] = s
                blk_max = s if blk_max is None else jnp.maximum(blk_max, s)
            m_old = m_ref[h]
            m_new = jnp.maximum(m_old, jnp.broadcast_to(jnp.max(blk_max, axis=-1, keepdims=True),
                                                        (TQ, LANES)))
            al_ref[h] = jnp.exp2(m_old - m_new)
            m_ref[h] = m_new
        for g0 in range(0, DSA_HEADS, PV_GROUP):
            for h in range(g0, g0 + PV_GROUP):
                m_new = m_ref[h]
                for a in range(nb):
                    sl = slice(a * LANES, (a + 1) * LANES)
                    p_ref[h * TQ:(h + 1) * TQ, sl] = jnp.exp2(s_ref[h, a] - m_new).astype(BF16)
            pv = _dot(p_ref[g0 * TQ:(g0 + PV_GROUP) * TQ, :], ckx)
            for h in range(g0, g0 + PV_GROUP):
                alpha = al_ref[h]
                acc_ref[h] = (jnp.concatenate([alpha, alpha], axis=1) * acc_ref[h]
                              + pv[(h - g0) * TQ:(h - g0 + 1) * TQ])

    def far(c, carry):
        attend(c, False)
        return carry

    lax.fori_loop(0, jnp.maximum(c_diag - 1, 0), far, 0)

    @pl.when(c_diag > 0)
    def _():
        attend(c_diag - 1, True)

    attend(c_diag, True)

    out = jnp.zeros((TQ, DSA_WIDTH), F32)
    for h in range(DSA_HEADS):
        o_lat = acc_ref[h, :, :KV_RANK] / acc_ref[h, :, KV_RANK:]
        out = out + _dot(o_lat.astype(BF16), wuv_ref[h])
    o_ref[...] = out


def _rel_bucket_table(n):
    max_exact = REL_BUCKETS // 2
    d = np.arange(n)
    df = np.maximum(d, 1).astype(np.float32)
    large = max_exact + (np.log(df / np.float32(max_exact)) / np.float32(math.log(REL_MAX_DIST / max_exact))
                         * np.float32(REL_BUCKETS - max_exact)).astype(np.int32)
    large = np.minimum(large, REL_BUCKETS - 1)
    return np.where(d < max_exact, d, large)


def _dsa_attention(h_main, h_idx, ckv, kp, w_uk, w_uv, rel_bias, B, L):
    T = B * L
    nq = L // TQ
    k_sel = min(DSA_TOPK, L // 4)
    assert SCORE_COLS == 4 * LANES and SCORE_COLS >= k_sel
    blk = LANES
    dist = blk + np.arange(blk)[:, None] - np.arange(2 * blk)[None, :]
    assert _rel_bucket_table(REL_MAX_DIST * 4)[blk:].min() == REL_BUCKETS - 1
    bucket = _rel_bucket_table(2 * blk + 1)[np.maximum(dist, 0)]
    rb = rel_bias.astype(F32)
    onehot = (jnp.asarray(bucket, I32)[:, :, None] == jnp.arange(REL_BUCKETS, dtype=I32)).astype(F32)
    near = jnp.einsum("qkb,bh->hqk", onehot, rb - rb[REL_BUCKETS - 1], precision=lax.Precision.HIGHEST)
    bias_near = jnp.pad(near * LOG2_E, ((0, 0), (0, 0), (blk, blk)))
    bias_near = bias_near.reshape(DSA_HEADS, blk, 4, blk).transpose(0, 2, 1, 3)
    assert L % (2 * SCORE_COLS) == 0 and TQ % LANES == 0 and SCORE_COLS % TQ == 0
    once = pl.Buffered(1)
    u = np.arange(SCORE_COLS)[:, None]
    v = np.arange(SCORE_COLS + LANES)[None, :]
    tri = jnp.asarray((u <= v) | (v >= SCORE_COLS), BF16)
    wuv_band = jnp.zeros((DSA_HEADS, KV_RANK, DSA_WIDTH), F32)
    for h in range(DSA_HEADS):
        wuv_band = wuv_band.at[h, :, h * DSA_HEAD_DIM:(h + 1) * DSA_HEAD_DIM].set(w_uv[h])
    kern = functools.partial(_dsa_kernel, k_sel=k_sel)
    lpad = -(-L // (2 * SCORE_COLS)) * (2 * SCORE_COLS)
    return pl.pallas_call(
        kern,
        grid=(B, nq),
        in_specs=[
            pl.BlockSpec((TQ, DSA_WIDTH), lambda b, i: (b * nq + i, 1)),
            pl.BlockSpec((TQ, IDX_HEADS * IDX_DIM), lambda b, i: (b * nq + i, 0)),
            pl.BlockSpec((TQ, LANES), lambda b, i: (b * nq + i, 2)),
            pl.BlockSpec((L, 4 * IDX_DIM), lambda b, i: (b, 0), pipeline_mode=once),
            pl.BlockSpec((L, 2 * KV_RANK), lambda b, i: (b, 0), pipeline_mode=once),
            pl.BlockSpec((DSA_HEADS, DSA_HEAD_DIM, KV_RANK), lambda b, i: (0, 0, 0)),
            pl.BlockSpec((DSA_HEADS, KV_RANK, DSA_WIDTH), lambda b, i: (0, 0, 0)),
            pl.BlockSpec((DSA_HEADS, 4, blk, blk), lambda b, i: (0, 0, 0, 0), pipeline_mode=once),
            pl.BlockSpec((SCORE_COLS, SCORE_COLS + LANES), lambda b, i: (0, 0)),
        ],
        out_specs=pl.BlockSpec((TQ, DSA_WIDTH), lambda b, i: (b * nq + i, 0)),
        out_shape=jax.ShapeDtypeStruct((T, DSA_WIDTH), F32),
        scratch_shapes=[
            pltpu.VMEM((lpad // LANES, TQ, LANES), I16),
            pltpu.VMEM((lpad // LANES, TQ, LANES), I16),
            pltpu.VMEM((lpad // LANES, TQ, LANES), F32),
            pltpu.VMEM((DSA_HEADS * TQ, KV_RANK), BF16),
            pltpu.VMEM((DSA_HEADS, SCORE_COLS // LANES, TQ, LANES), F32),
            pltpu.VMEM((DSA_HEADS * TQ, SCORE_COLS), BF16),
            pltpu.VMEM((DSA_HEADS, TQ, LANES), F32),
            pltpu.VMEM((DSA_HEADS, TQ, LANES), F32),
            pltpu.VMEM((DSA_HEADS, TQ, 2 * KV_RANK), F32),
        ],
        compiler_params=_cparams(("parallel", "arbitrary")),
        name="dsa_attention",
    )(h_main, h_idx, h_idx, kp, ckv, w_uk.astype(BF16), wuv_band.astype(BF16), bias_near, tri)


def _mlstm_prep_kernel(cur_ref, halo_ref, w_ref, b_ref, wqh_ref, wql_ref, wkh_ref, wkl_ref,
                       q_ref, k_ref, xbuf, *, tl):
    i = pl.program_id(1)
    xbuf[SUBLANES:SUBLANES + tl, :] = cur_ref[...]
    xbuf[0:SUBLANES, :] = jnp.where(i > 0, halo_ref[...], 0.0)
    base = SUBLANES - (MLSTM_CONV - 1)
    acc = jnp.broadcast_to(b_ref[...], (tl, MLSTM_WIDTH))
    for k in range(MLSTM_CONV):
        acc = acc + w_ref[k:k + 1, :] * xbuf[base + k:base + k + tl, :]
    xc = acc * jax.nn.sigmoid(acc)
    q_ref[...] = _dot3(xc, wqh_ref[...], wql_ref[...]) * (MLSTM_HEAD_DIM ** -0.5)
    k_ref[...] = _dot3(xc, wkh_ref[...], wkl_ref[...])


def _block_diag(w):
    h, d, e = w.shape
    out = jnp.zeros((h * d, h * e), F32)
    for i in range(h):
        out = out.at[i * d:(i + 1) * d, i * e:(i + 1) * e].set(w[i])
    return out


def _mlstm_prep(h_main, conv_w, conv_b, w_qm, w_km, B, L, tl=512):
    T = B * L
    nl = L // tl
    hb = tl // SUBLANES
    wq = _wsplit(_block_diag(w_qm))
    wk = _wsplit(_block_diag(w_km))
    full = lambda shape: pl.BlockSpec(shape, lambda b, i: (0,) * len(shape))
    return pl.pallas_call(
        functools.partial(_mlstm_prep_kernel, tl=tl),
        grid=(B, nl),
        in_specs=[
            pl.BlockSpec((tl, MLSTM_WIDTH), lambda b, i: (b * nl + i, 4)),
            pl.BlockSpec((SUBLANES, MLSTM_WIDTH),
                         lambda b, i: (jnp.maximum((b * nl + i) * hb - 1, 0), 4)),
            full((MLSTM_CONV, MLSTM_WIDTH)), full((1, MLSTM_WIDTH)),
            full((MLSTM_WIDTH, MLSTM_WIDTH)), full((MLSTM_WIDTH, MLSTM_WIDTH)),
            full((MLSTM_WIDTH, MLSTM_WIDTH)), full((MLSTM_WIDTH, MLSTM_WIDTH)),
        ],
        out_specs=[pl.BlockSpec((tl, MLSTM_WIDTH), lambda b, i: (b * nl + i, 0)),
                   pl.BlockSpec((tl, MLSTM_WIDTH), lambda b, i: (b * nl + i, 0))],
        out_shape=[jax.ShapeDtypeStruct((T, MLSTM_WIDTH), F32),
                   jax.ShapeDtypeStruct((T, MLSTM_WIDTH), F32)],
        scratch_shapes=[pltpu.VMEM((SUBLANES + tl, MLSTM_WIDTH), F32)],
        compiler_params=_cparams(("parallel", "parallel")),
        name="mlstm_prep",
    )(h_main, h_main, conv_w, conv_b.reshape(1, -1), wq[0], wq[1], wk[0], wk[1])


ML_ROWS = 128


def _mlstm_kernel(q_ref, k_ref, v_ref, o_ref, sm_ref, bi_ref, bf_ref, g_ref, out_ref,
                  cm_ref, n_ref, m_ref):
    C = MLSTM_CHUNK
    dh = MLSTM_HEAD_DIM

    @pl.when(pl.program_id(1) == 0)
    def _():
        cm_ref[...] = jnp.zeros(cm_ref.shape, F32)
        n_ref[...] = jnp.zeros(n_ref.shape, F32)
        m_ref[...] = jnp.zeros(m_ref.shape, F32)

    sm_t = sm_ref[...].T
    ig = sm_t[SM_I:SM_I + SUBLANES] + bi_ref[...]
    fg = jax.nn.log_sigmoid(sm_t[SM_F:SM_F + SUBLANES] + bf_ref[...])
    lane = lax.broadcasted_iota(I32, (SUBLANES, ML_ROWS), 1) & (C - 1)
    bcum = fg
    s = 1
    while s < C:
        bcum = bcum + jnp.where(lane >= s, pltpu.roll(bcum, s, axis=1), 0.0)
        s *= 2
    cols = jnp.concatenate([bcum, ig, jnp.zeros((LANES - 2 * SUBLANES, ML_ROWS), F32)], axis=0).T
    tri = (lax.broadcasted_iota(I32, (C, C), 1) <= lax.broadcasted_iota(I32, (C, C), 0))

    q_all, k_all, v_all, o_all = q_ref[...], k_ref[...], v_ref[...], o_ref[...]
    g_all = g_ref[...]
    for c in range(ML_ROWS // C):
        r0 = c * C
        for h in range(MLSTM_HEADS):
            hs = slice(h * dh, (h + 1) * dh)
            qj = q_all[r0:r0 + C, hs]
            kj = k_all[r0:r0 + C, hs]
            vj = v_all[r0:r0 + C, hs]
            b_row = bcum[h:h + 1, r0:r0 + C]
            i_row = ig[h:h + 1, r0:r0 + C]
            b_col = cols[r0:r0 + C, h:h + 1]
            i_col = cols[r0:r0 + C, SUBLANES + h:SUBLANES + h + 1]
            m_prev = m_ref[h:h + 1, 0:1]
            n_prev = n_ref[h:h + 1, :]
            cm_prev = cm_ref[h]

            dm = jnp.where(tri, b_col - b_row + i_row, NEG_INF)
            inter = b_col + m_prev
            m_row = jnp.maximum(inter, jnp.max(dm, axis=-1, keepdims=True))
            w_inter = jnp.exp(inter - m_row)
            qb, kb, vb = qj.astype(BF16), kj.astype(BF16), vj.astype(BF16)
            sw = _dot_t(qb, kb) * jnp.exp(dm - m_row)
            num = _dot(sw.astype(BF16), vb) + w_inter * _dot(qb, cm_prev.astype(BF16))
            den = (jnp.sum(sw, axis=-1, keepdims=True)
                   + w_inter * jnp.sum(qj * n_prev, axis=-1, keepdims=True))
            hh = num / jnp.maximum(jnp.abs(den), jnp.exp(-m_row))

            b_last = b_row[:, C - 1:C]
            g_row = b_last - b_row + i_row
            g_col = b_last - b_col + i_col
            m_new = jnp.maximum(b_last + m_prev, jnp.max(g_row, axis=-1, keepdims=True))
            decay = jnp.exp(b_last + m_prev - m_new)
            kw = kj * jnp.exp(g_col - m_new)
            cm_ref[h] = decay * cm_prev + _dot_tl(kw.astype(BF16), vb)
            n_ref[h:h + 1, :] = decay * n_prev + jnp.sum(kw, axis=0, keepdims=True)
            m_ref[h:h + 1, :] = jnp.broadcast_to(m_new, (1, LANES))

            mu = jnp.mean(hh, axis=-1, keepdims=True)
            hc = hh - mu
            var = jnp.mean(hc * hc, axis=-1, keepdims=True)
            hn = hc * lax.rsqrt(var + LN_EPS) * g_all[:, hs]
            out_ref[r0:r0 + C, hs] = jax.nn.sigmoid(o_all[r0:r0 + C, hs]) * hn


def _mlstm(q, k, h_main, h_idx, b_i, b_f, norm_g, B, L):
    T = B * L
    nl = L // ML_ROWS
    pad8 = lambda v: jnp.pad(v.astype(F32), (0, SUBLANES - MLSTM_HEADS)).reshape(SUBLANES, 1)
    row = lambda col: pl.BlockSpec((ML_ROWS, MLSTM_WIDTH), lambda b, i: (b * nl + i, col))
    return pl.pallas_call(
        _mlstm_kernel,
        grid=(B, nl),
        in_specs=[row(0), row(0), row(5), row(6),
                  pl.BlockSpec((ML_ROWS, LANES), lambda b, i: (b * nl + i, 2)),
                  pl.BlockSpec((SUBLANES, 1), lambda b, i: (0, 0)),
                  pl.BlockSpec((SUBLANES, 1), lambda b, i: (0, 0)),
                  pl.BlockSpec((1, MLSTM_WIDTH), lambda b, i: (0, 0))],
        out_specs=row(0),
        out_shape=jax.ShapeDtypeStruct((T, MLSTM_WIDTH), F32),
        scratch_shapes=[pltpu.VMEM((MLSTM_HEADS, MLSTM_HEAD_DIM, MLSTM_HEAD_DIM), F32),
                        pltpu.VMEM((SUBLANES, MLSTM_HEAD_DIM), F32),
                        pltpu.VMEM((SUBLANES, LANES), F32)],
        compiler_params=_cparams(("parallel", "arbitrary")),
        name="mlstm_scan",
    )(q, k, h_main, h_main, h_idx, pad8(b_i), pad8(b_f), norm_g.reshape(1, -1))


def _mix_out_kernel(ya_ref, yb_ref, yc_ref, x_ref, wa_ref, wb_ref, wc_ref, g_ref, b_ref, o_ref, *,
                    alpha):
    y = (_dot(ya_ref[...].astype(BF16), wa_ref[...])
         + _dot(yb_ref[...].astype(BF16), wb_ref[...])
         + _dot(yc_ref[...].astype(BF16), wc_ref[...]))
    o_ref[...] = _layer_norm(alpha * x_ref[...] + y, g_ref[...], b_ref[...])


def _mix_out(y_a, y_b, y_c, x, w_out, g, b, alpha, tm=512):
    T, D = x.shape
    w = w_out.astype(BF16)
    wa, wb, wc = w[:CONV_CH], w[CONV_CH:CONV_CH + DSA_WIDTH], w[CONV_CH + DSA_WIDTH:]
    rows = lambda width: pl.BlockSpec((tm, width), lambda i: (i, 0))
    full = lambda a: pl.BlockSpec(a.shape, lambda i: (0, 0))
    g2, b2 = g.reshape(1, D), b.reshape(1, D)
    return pl.pallas_call(
        functools.partial(_mix_out_kernel, alpha=alpha),
        grid=(T // tm,),
        in_specs=[rows(CONV_CH), rows(DSA_WIDTH), rows(MLSTM_WIDTH), rows(D),
                  full(wa), full(wb), full(wc), full(g2), full(b2)],
        out_specs=rows(D),
        out_shape=jax.ShapeDtypeStruct((T, D), F32),
        compiler_params=_cparams(("parallel",)),
        name="mix_out",
    )(y_a, y_b, y_c, x, wa, wb, wc, g2, b2)


def _xattn_kernel(x_ref, kv_ref, wq_ref, wo_ref, g_ref, b_ref, o_ref, *, alpha):
    x = x_ref[...]
    q = _dot(x.astype(BF16), wq_ref[...])
    kv = kv_ref[...]
    scale = XATTN_HEAD_DIM ** -0.5
    outs = []
    for h in range(XATTN_HEADS):
        hs = slice(h * XATTN_HEAD_DIM, (h + 1) * XATTN_HEAD_DIM)
        kh = kv[:, hs].astype(BF16)
        vh = kv[:, D_MODEL + h * XATTN_HEAD_DIM:D_MODEL + (h + 1) * XATTN_HEAD_DIM].astype(BF16)
        lg = _dot_t(q[:, hs].astype(BF16), kh) * scale
        lg = lg - jnp.max(lg, axis=-1, keepdims=True)
        p = jnp.exp(lg)
        p = p / jnp.sum(p, axis=-1, keepdims=True)
        outs.append(_dot(p.astype(BF16), vh))
    o = jnp.concatenate(outs, axis=-1)
    y = _dot(o.astype(BF16), wo_ref[...])
    o_ref[...] = _layer_norm(alpha * x + y, g_ref[...], b_ref[...])


def _xattn(x, kv, w_q, w_o, g, b, alpha, B, L, tm=256):
    T, D = x.shape
    nl = L // tm
    M = kv.shape[0] // B
    full = lambda a: pl.BlockSpec(a.shape, lambda bb, i: (0, 0))
    g2, b2 = g.reshape(1, D), b.reshape(1, D)
    wq, wo = w_q.astype(BF16), w_o.astype(BF16)
    return pl.pallas_call(
        functools.partial(_xattn_kernel, alpha=alpha),
        grid=(B, nl),
        in_specs=[pl.BlockSpec((tm, D), lambda bb, i: (bb * nl + i, 0)),
                  pl.BlockSpec((M, 2 * D), lambda bb, i: (bb, 0)),
                  full(wq), full(wo), full(g2), full(b2)],
        out_specs=pl.BlockSpec((tm, D), lambda bb, i: (bb * nl + i, 0)),
        out_shape=jax.ShapeDtypeStruct((T, D), F32),
        compiler_params=_cparams(("parallel", "parallel")),
        name="xattn",
    )(x, kv, wq, wo, g2, b2)


def _peer_score_kernel(x_ref, w_ref, k1h_ref, k1l_ref, k2h_ref, k2l_ref, st_ref):
    q = _dot(x_ref[...].astype(BF16), w_ref[...])
    half = PEER_QDIM // 2
    for h in range(PEER_HEADS):
        for part, (kh_ref, kl_ref) in enumerate(((k1h_ref, k1l_ref), (k2h_ref, k2l_ref))):
            c0 = h * PEER_QDIM + part * half
            q_hi, q_lo = _split(q[:, c0:c0 + half])
            kh, kl = kh_ref[...], kl_ref[...]
            st_ref[2 * h + part] = _dot_t(kh, q_hi) + _dot_t(kl, q_hi) + _dot_t(kh, q_lo)


def _peer_scores(x, w_pq, sub_k1, sub_k2, tm=512):
    T, D = x.shape
    w = w_pq.astype(BF16)
    k1h, k1l = _wsplit(sub_k1)
    k2h, k2l = _wsplit(sub_k2)
    full = lambda a: pl.BlockSpec(a.shape, lambda i: (0, 0))
    return pl.pallas_call(
        _peer_score_kernel,
        grid=(T // tm,),
        in_specs=[pl.BlockSpec((tm, D), lambda i: (i, 0)), full(w),
                  full(k1h), full(k1l), full(k2h), full(k2l)],
        out_specs=pl.BlockSpec((2 * PEER_HEADS, PEER_KEYS, tm), lambda i: (0, 0, i)),
        out_shape=jax.ShapeDtypeStruct((2 * PEER_HEADS, PEER_KEYS, T), F32),
        compiler_params=_cparams(("parallel",)),
        name="peer_scores",
    )(x, w, k1h, k1l, k2h, k2l)


PEER_NTOP = PEER_TOPK + 1
PEER_PAIR_ROWS = tuple(PEER_NTOP // (k + 1) for k in range(PEER_NTOP))
PEER_CAND_ROWS = -(-sum(PEER_PAIR_ROWS) // SUBLANES) * SUBLANES


def _peer_thr_kernel(st_ref, stats_ref, v2_ref, cand_ref):
    def top_rows(x):
        rows = []
        for _ in range(PEER_NTOP):
            m = jnp.max(x, axis=0, keepdims=True)
            rows.append(m)
            x = jnp.where(x == m, NEG_INF, x)
        return rows

    v1 = top_rows(st_ref[0])
    v2 = top_rows(st_ref[1])
    for k in range(PEER_NTOP):
        v2_ref[k:k + 1, :] = v2[k]
    r = 0
    for k, n in enumerate(PEER_PAIR_ROWS):
        cand_ref[r:r + n, :] = v1[k] + v2_ref[0:n, :]
        r += n
    cand_ref[r:PEER_CAND_ROWS, :] = jnp.full((PEER_CAND_ROWS - r, cand_ref.shape[1]), NEG_INF, F32)
    cand = cand_ref[...]
    x = cand
    for _ in range(PEER_TOPK - 1):
        m = jnp.max(x, axis=0, keepdims=True)
        x = jnp.where(x == m, NEG_INF, x)
    thr = jnp.max(x, axis=0, keepdims=True)
    nxt = jnp.max(jnp.where(x == thr, NEG_INF, x), axis=0, keepdims=True)
    top = v1[0] + v2[0]
    z = jnp.sum(jnp.where(cand >= thr, jnp.exp(cand - top), 0.0), axis=0, keepdims=True)
    cut = jnp.where(nxt > NEG_INF, 0.5 * thr + 0.5 * nxt, thr)
    pad = jnp.zeros((SUBLANES - 4, thr.shape[1]), F32)
    stats_ref[0] = jnp.concatenate([cut, v1[0], v2[0], 1.0 / z, pad], axis=0)


def _peer_thresholds(st, tm=512):
    T = st.shape[2]
    return pl.pallas_call(
        _peer_thr_kernel,
        grid=(PEER_HEADS, T // tm),
        in_specs=[pl.BlockSpec((2, PEER_KEYS, tm), lambda h, i: (h, 0, i))],
        out_specs=pl.BlockSpec((1, SUBLANES, tm), lambda h, i: (h, 0, i)),
        out_shape=jax.ShapeDtypeStruct((PEER_HEADS, SUBLANES, T), F32),
        scratch_shapes=[pltpu.VMEM((3 * SUBLANES, tm), F32),
                        pltpu.VMEM((PEER_CAND_ROWS, tm), F32)],
        compiler_params=_cparams(("parallel", "parallel")),
        name="peer_thresholds",
    )(st)


PEER_TL = 512
PEER_ET = 512
PEER_SUB = 64
GELU_FOLD = 2.0 ** -0.5

def _peer_mix_kernel(x_ref, st_ref, stats_ref, u_ref, vt_ref, g_ref, b_ref, o_ref,
                     xb_ref, d1_ref, e1_ref, e2_ref, gate_ref, w_ref, acc_ref, *, alpha):
    j = pl.program_id(1)
    n_tiles = pl.num_programs(1) - 1
    sub_tiles = PEER_ET // PEER_KEYS
    slot = j % 2

    @pl.when(j == 0)
    def _():
        xb_ref[...] = x_ref[...].astype(BF16)
        for h in range(PEER_HEADS):
            st = stats_ref[h]
            s1 = st_ref[2 * h]
            d1_ref[h] = st[0:1] - s1
            e1_ref[h] = jnp.exp(s1 - st[1:2]) * (st[3:4] * GELU_FOLD)
            e2_ref[h] = jnp.exp(st_ref[2 * h + 1] - st[2:3])
        acc_ref[...] = jnp.zeros(acc_ref.shape, F32)
        w_ref[...] = jnp.zeros(w_ref.shape, BF16)

    def gate_block(a, lb):
        i1 = jnp.minimum(j, n_tiles - 1) * sub_tiles + a
        ls = slice(lb * LANES, (lb + 1) * LANES)
        d_rows = [d1_ref[h, pl.ds(i1, 1), :][:, ls] for h in range(PEER_HEADS)]
        c_rows = [e1_ref[h, pl.ds(i1, 1), :][:, ls] for h in range(PEER_HEADS)]
        for k0 in range(0, PEER_KEYS, PEER_SUB):
            ks = slice(k0, k0 + PEER_SUB)
            gate = None
            for h in range(PEER_HEADS):
                picked = st_ref[2 * h + 1, ks, ls] >= d_rows[h]
                term = jnp.where(picked, e2_ref[h, ks, ls] * c_rows[h], 0.0)
                gate = term if gate is None else gate + term
            gate_ref[a * PEER_KEYS + k0:a * PEER_KEYS + k0 + PEER_SUB, ls] = gate

    for a in range(sub_tiles):
        for lb in range(PEER_TL // LANES):
            gate_block(a, lb)
    acc_ref[...] += _dot(vt_ref[...], w_ref[1 - slot])
    su = _dot_t(u_ref[...], xb_ref[...])
    act = su + su * lax.erf(su)
    w_ref[slot] = (gate_ref[...] * act).astype(BF16)

    @pl.when(j == n_tiles)
    def _():
        y = acc_ref[...].T
        o_ref[...] = _layer_norm(alpha * x_ref[...] + y, g_ref[...], b_ref[...])


def _peer_mix(x, st, stats, peer_u, peer_v, g, b, alpha):
    T, D = x.shape
    E = peer_u.shape[0]
    u = (peer_u * GELU_FOLD).astype(BF16)
    vt = peer_v.astype(BF16).T
    g2, b2 = g.reshape(1, D), b.reshape(1, D)
    tl, et = PEER_TL, PEER_ET
    n_e = E // et
    return pl.pallas_call(
        functools.partial(_peer_mix_kernel, alpha=alpha),
        grid=(T // tl, n_e + 1),
        in_specs=[pl.BlockSpec((tl, D), lambda i, j: (i, 0)),
                  pl.BlockSpec((2 * PEER_HEADS, PEER_KEYS, tl), lambda i, j: (0, 0, i)),
                  pl.BlockSpec((PEER_HEADS, SUBLANES, tl), lambda i, j: (0, 0, i)),
                  pl.BlockSpec((et, D), lambda i, j: (jnp.minimum(j, n_e - 1), 0)),
                  pl.BlockSpec((D, et), lambda i, j: (0, jnp.maximum(j - 1, 0))),
                  pl.BlockSpec((1, D), lambda i, j: (0, 0)),
                  pl.BlockSpec((1, D), lambda i, j: (0, 0))],
        out_specs=pl.BlockSpec((tl, D), lambda i, j: (i, 0)),
        out_shape=jax.ShapeDtypeStruct((T, D), F32),
        scratch_shapes=[pltpu.VMEM((tl, D), BF16),
                        pltpu.VMEM((PEER_HEADS, PEER_KEYS, tl), F32),
                        pltpu.VMEM((PEER_HEADS, PEER_KEYS, tl), F32),
                        pltpu.VMEM((PEER_HEADS, PEER_KEYS, tl), F32),
                        pltpu.VMEM((et, tl), F32),
                        pltpu.VMEM((2, et, tl), BF16),
                        pltpu.VMEM((D, tl), F32)],
        compiler_params=_cparams(("parallel", "arbitrary")),
        name="peer_mix",
    )(x, st, stats, u, vt, g2, b2)


def _pack_w_in(w):
    sizes = (512, 512, 128, 256, 64, 4, 256, 256, 256, 4, 4)
    offs = np.concatenate([[0], np.cumsum(sizes)])
    seg = lambda n: w[:, int(offs[n]):int(offs[n + 1])]
    a_in, q_b, c_b, qi, ki, wi, xc, v_m, o_m, i_m, f_m = (seg(n) for n in range(len(sizes)))
    z = lambda n: jnp.zeros((w.shape[0], n), w.dtype)
    main = jnp.concatenate([a_in, q_b, xc, v_m, o_m, c_b], axis=1)
    idx = jnp.concatenate([qi, ki, wi, z(4), i_m, z(4), f_m, z(IDX_W - 256 - SM_F - 4)], axis=1)
    return main, idx


def kernel(x, mem, ln_in_g, ln_in_b, rel_bias, w_in, conv_a_w, conv_a_b, norm_a_g, norm_a_b,
           kv_norm_g, w_uk, w_uv, conv_m_w, conv_m_b, w_qm, w_km, b_i, b_f, norm_m_g, w_out,
           ln1_g, ln1_b, w_cq, w_ckv, w_co, ln2_g, ln2_b, w_pq, sub_k1, sub_k2, peer_u, peer_v,
           ln3_g, ln3_b):
    B, L, D = x.shape
    T = B * L
    depth = w_in.shape[0]
    alpha = (2.0 * depth) ** 0.25
    xs = _entry_ln(x.reshape(T, D), ln_in_g, ln_in_b)
    mem2 = mem.reshape(-1, D)
    for l in range(depth):
        w_main, w_idx = _pack_w_in(w_in[l])
        h_main = _matmul(xs, w_main, 1, 512, 640, "proj_main")
        h_idx = _matmul(xs, w_idx, 3, 512, IDX_W, "proj_idx")
        y_a = _conv_group(h_main, conv_a_w[l], conv_a_b[l], norm_a_g[l], norm_a_b[l], B, L)
        ckv, kp = _dsa_prep(h_main, h_idx, kv_norm_g[l])
        y_b = _dsa_attention(h_main, h_idx, ckv, kp, w_uk[l], w_uv[l], rel_bias, B, L)
        q_m, k_m = _mlstm_prep(h_main, conv_m_w[l], conv_m_b[l], w_qm[l], w_km[l], B, L)
        y_c = _mlstm(q_m, k_m, h_main, h_idx, b_i[l], b_f[l], norm_m_g[l], B, L)
        xs = _mix_out(y_a, y_b, y_c, xs, w_out[l], ln1_g[l], ln1_b[l], alpha)

        kv = _matmul(mem2, w_ckv[l], 1, mem2.shape[0], 512, "xattn_kv")
        xs = _xattn(xs, kv, w_cq[l], w_co[l], ln2_g[l], ln2_b[l], alpha, B, L)

        st = _peer_scores(xs, w_pq[l], sub_k1[l], sub_k2[l])
        stats = _peer_thresholds(st)
        xs = _peer_mix(xs, st, stats, peer_u[l], peer_v[l], ln3_g[l], ln3_b[l], alpha)
    return xs.reshape(B, L, D)
```

```python
import functools
import math

import numpy as np
import jax
import jax.numpy as jnp
from jax import lax
from jax.experimental import pallas as pl
from jax.experimental.pallas import tpu as pltpu

F32 = jnp.float32
BF16 = jnp.bfloat16
I32 = jnp.int32

D_MODEL = 1024
CONV_CH = 256
CONV_WIDTH = 31
DSA_HEADS = 8
DSA_HEAD_DIM = 64
DSA_WIDTH = 512
KV_RANK = 128
IDX_HEADS = 4
IDX_DIM = 64
DSA_TOPK = 256
MLSTM_HEADS = 4
MLSTM_HEAD_DIM = 64
MLSTM_WIDTH = 256
MLSTM_CONV = 4
MLSTM_CHUNK = 64
REL_BUCKETS = 32
REL_MAX_DIST = 128
XATTN_HEADS = 4
XATTN_HEAD_DIM = 256
PEER_HEADS = 8
PEER_KEYS = 128
PEER_QDIM = 256
PEER_TOPK = 16
LN_EPS = 1e-5

LANES = 128
SUBLANES = 8
VMEM_LIMIT = 56 * 1024 * 1024

NEG_INF = float("-inf")
LOG2_E = 1.4426950408889634

MAIN_W = 1920
IDX_W = 384
SM_WI = 64
SM_I = 72
SM_F = 80


def _cparams(sem):
    return pltpu.CompilerParams(dimension_semantics=sem, vmem_limit_bytes=VMEM_LIMIT)


def _dot(a, b):
    return jnp.dot(a, b, preferred_element_type=F32)


def _dot_t(a, b):
    return lax.dot_general(a, b, (((1,), (1,)), ((), ())), preferred_element_type=F32)


def _dot_tl(a, b):
    return lax.dot_general(a, b, (((0,), (0,)), ((), ())), preferred_element_type=F32)


def _split(a):
    hi = a.astype(BF16)
    lo = (a - hi.astype(F32)).astype(BF16)
    return hi, lo


def _dot3(a, b_hi, b_lo, dot=_dot):
    a_hi, a_lo = _split(a)
    return dot(a_hi, b_hi) + dot(a_lo, b_hi) + dot(a_hi, b_lo)


def _layer_norm(x, g, b):
    mu = jnp.mean(x, axis=-1, keepdims=True)
    xc = x - mu
    var = jnp.mean(xc * xc, axis=-1, keepdims=True)
    return xc * lax.rsqrt(var + LN_EPS) * g + b


def _wsplit(w):
    hi = w.astype(BF16)
    lo = (w - hi.astype(F32)).astype(BF16)
    return hi, lo


def _ln_kernel(x_ref, g_ref, b_ref, o_ref):
    o_ref[...] = _layer_norm(x_ref[...], g_ref[...], b_ref[...])


def _entry_ln(x, g, b, tm=512):
    T, D = x.shape
    return pl.pallas_call(
        _ln_kernel,
        grid=(T // tm,),
        in_specs=[pl.BlockSpec((tm, D), lambda i: (i, 0)),
                  pl.BlockSpec((1, D), lambda i: (0, 0)),
                  pl.BlockSpec((1, D), lambda i: (0, 0))],
        out_specs=pl.BlockSpec((tm, D), lambda i: (i, 0)),
        out_shape=jax.ShapeDtypeStruct((T, D), F32),
        compiler_params=_cparams(("parallel",)),
        name="entry_ln",
    )(x, g.reshape(1, D), b.reshape(1, D))


def _mm1_kernel(x_ref, w_ref, o_ref):
    o_ref[...] = _dot(x_ref[...].astype(BF16), w_ref[...])


def _mm3_kernel(x_ref, wh_ref, wl_ref, o_ref):
    o_ref[...] = _dot3(x_ref[...], wh_ref[...], wl_ref[...])


def _matmul(x, w, passes, tm, tn, name):
    T, K = x.shape
    N = w.shape[1]
    x_spec = pl.BlockSpec((tm, K), lambda j, i: (i, 0))
    w_spec = pl.BlockSpec((K, tn), lambda j, i: (0, j))
    if passes == 1:
        kern, ws, w_specs = _mm1_kernel, (w.astype(BF16),), [w_spec]
    else:
        kern, ws, w_specs = _mm3_kernel, _wsplit(w), [w_spec, w_spec]
    return pl.pallas_call(
        kern,
        grid=(N // tn, T // tm),
        in_specs=[x_spec] + w_specs,
        out_specs=pl.BlockSpec((tm, tn), lambda j, i: (i, j)),
        out_shape=jax.ShapeDtypeStruct((T, N), F32),
        compiler_params=_cparams(("parallel", "parallel")),
        name=name,
    )(x, *ws)


CONV_HALO = 32
CONV_ROWS = 64


def _conv_kernel(cur_ref, halo_ref, w_ref, b_ref, g_ref, bb_ref, o_ref, ubuf, *, tl):
    i = pl.program_id(1)
    cur = cur_ref[...]
    ubuf[CONV_HALO:CONV_HALO + tl, :] = cur[:, :CONV_CH] * jax.nn.sigmoid(cur[:, CONV_CH:])
    hal = halo_ref[...]
    uh = hal[:, :CONV_CH] * jax.nn.sigmoid(hal[:, CONV_CH:])
    ubuf[0:CONV_HALO, :] = jnp.where(i > 0, uh, 0.0)
    base = CONV_HALO - (CONV_WIDTH - 1)
    for c in range(tl // CONV_ROWS):
        r0 = c * CONV_ROWS
        acc = jnp.broadcast_to(b_ref[...], (CONV_ROWS, CONV_CH))
        for k in range(CONV_WIDTH):
            acc = acc + w_ref[k:k + 1, :] * ubuf[r0 + base + k:r0 + base + k + CONV_ROWS, :]
        y = _layer_norm(acc, g_ref[...], bb_ref[...])
        o_ref[r0:r0 + CONV_ROWS, :] = y * jax.nn.sigmoid(y)


def _conv_group(h_main, conv_w, conv_b, ln_g, ln_b, B, L, tl=512):
    T = B * L
    nl = L // tl
    hb = tl // CONV_HALO
    return pl.pallas_call(
        functools.partial(_conv_kernel, tl=tl),
        grid=(B, nl),
        in_specs=[
            pl.BlockSpec((tl, 2 * CONV_CH), lambda b, i: (b * nl + i, 0)),
            pl.BlockSpec((CONV_HALO, 2 * CONV_CH),
                         lambda b, i: (jnp.maximum((b * nl + i) * hb - 1, 0), 0)),
            pl.BlockSpec((CONV_WIDTH, CONV_CH), lambda b, i: (0, 0)),
            pl.BlockSpec((1, CONV_CH), lambda b, i: (0, 0)),
            pl.BlockSpec((1, CONV_CH), lambda b, i: (0, 0)),
            pl.BlockSpec((1, CONV_CH), lambda b, i: (0, 0)),
        ],
        out_specs=pl.BlockSpec((tl, CONV_CH), lambda b, i: (b * nl + i, 0)),
        out_shape=jax.ShapeDtypeStruct((T, CONV_CH), F32),
        scratch_shapes=[pltpu.VMEM((CONV_HALO + tl, CONV_CH), F32)],
        compiler_params=_cparams(("parallel", "parallel")),
        name="conv_group",
    )(h_main, h_main, conv_w, conv_b.reshape(1, -1), ln_g.reshape(1, -1), ln_b.reshape(1, -1))


def _dsa_prep_kernel(c_ref, sm_ref, g_ref, ckv_ref, kp_ref):
    c = c_ref[...]
    ms = jnp.mean(c * c, axis=-1, keepdims=True)
    ckv = (c * lax.rsqrt(ms + LN_EPS) * g_ref[...]).astype(BF16)
    ckv_ref[...] = jnp.concatenate([ckv, jnp.ones_like(ckv)], axis=-1)
    k_hi, k_lo = _split(sm_ref[...][:, :IDX_DIM])
    kp_ref[...] = jnp.concatenate([k_hi, k_hi, k_lo, jnp.zeros_like(k_hi)], axis=-1)


def _dsa_prep(h_main, h_idx, kv_g, tm=512):
    T = h_main.shape[0]
    return pl.pallas_call(
        _dsa_prep_kernel,
        grid=(T // tm,),
        in_specs=[pl.BlockSpec((tm, KV_RANK), lambda i: (i, 14)),
                  pl.BlockSpec((tm, LANES), lambda i: (i, 2)),
                  pl.BlockSpec((1, KV_RANK), lambda i: (0, 0))],
        out_specs=[pl.BlockSpec((tm, 2 * KV_RANK), lambda i: (i, 0)),
                   pl.BlockSpec((tm, 4 * IDX_DIM), lambda i: (i, 0))],
        out_shape=[jax.ShapeDtypeStruct((T, 2 * KV_RANK), BF16),
                   jax.ShapeDtypeStruct((T, 4 * IDX_DIM), BF16)],
        compiler_params=_cparams(("parallel",)),
        name="dsa_prep",
    )(h_main, h_idx, kv_g.reshape(1, -1))


TQ = 128
SCORE_COLS = 512
PV_GROUP = 1


I16 = jnp.int16
I16_MIN = -(2 ** 15)


def _dsa_kernel(qb_ref, qi_ref, sm_ref, kp_ref, ckv_ref, wuk_ref, wuv_ref, bias_ref, tri_ref, o_ref,
                hi_ref, lo_ref, selb_ref, ql_ref, s_ref, p_ref, m_ref, al_ref, acc_ref, *, k_sel):
    qt = pl.program_id(1)
    nb = SCORE_COLS // LANES
    c_diag = ((qt + 1) * TQ - 1) // SCORE_COLS
    n_chunks = c_diag + 1
    n_pairs = (n_chunks + 1) // 2
    row_t = qt * TQ + lax.broadcasted_iota(I32, (TQ, LANES), 0)
    lane_c = lax.broadcasted_iota(I32, (TQ, LANES), 1)

    qi = qi_ref[...]
    sm = sm_ref[...]
    qp = []
    for h in range(IDX_HEADS):
        q_hi, q_lo = _split(qi[:, h * IDX_DIM:(h + 1) * IDX_DIM])
        qp.append(jnp.concatenate([q_hi, q_lo, q_hi, jnp.zeros_like(q_hi)], axis=-1))
    qp = jnp.concatenate(qp, axis=0)
    w_fold = (IDX_DIM ** -0.5) * (IDX_HEADS ** -0.5)
    ws = [jnp.broadcast_to(sm[:, SM_WI + h:SM_WI + h + 1] * w_fold, (TQ, LANES))
          for h in range(IDX_HEADS)]

    def score_pair(cp, carry):
        for half in range(2):
            c0 = pl.multiple_of((2 * cp + half) * SCORE_COLS, SCORE_COLS)
            d = _dot_t(qp, kp_ref[pl.ds(c0, SCORE_COLS), :])
            for a in range(nb):
                off = c0 + a * LANES
                s = jnp.zeros((TQ, LANES), F32)
                for h in range(IDX_HEADS):
                    s = s + jnp.maximum(d[h * TQ:(h + 1) * TQ, a * LANES:(a + 1) * LANES], 0.0) * ws[h]
                s = jnp.where(off + lane_c <= row_t, s + 0.0, NEG_INF)
                bits = lax.bitcast_convert_type(s, I32)
                key = bits ^ ((bits >> 31) & 0x7FFFFFFF)
                blk = (2 * cp + half) * nb + a
                hi_ref[blk] = (key >> 16).astype(I16)
                lo_ref[blk] = ((key & 0xFFFF) + I16_MIN).astype(I16)
        return carry

    lax.fori_loop(0, n_pairs, score_pair, 0)

    def count16(ref, pred):
        def body(c, acc):
            for a in range(2 * nb):
                acc = acc + jnp.where(pred(ref[c * 2 * nb + a]), jnp.int16(1), jnp.int16(0))
            return acc
        acc = lax.fori_loop(0, n_pairs, body, jnp.zeros((TQ, LANES), I16))
        tot = jnp.sum(acc.astype(I32).astype(F32), axis=-1, keepdims=True)
        return jnp.broadcast_to(tot, (TQ, LANES))

    def search16(ref, k_need):
        c_nonneg = count16(ref, lambda blk: blk >= jnp.int16(0))
        ok0 = c_nonneg >= k_need
        th0 = jnp.where(ok0, 0, I16_MIN).astype(I32)
        above0 = jnp.where(ok0, 0.0, c_nonneg)

        def bit_step(it, carry):
            th, above = carry
            cand = th | (jnp.int32(1) << (14 - it))
            cand16 = cand.astype(I16)
            cnt = count16(ref, lambda blk: blk >= cand16)
            ok = cnt >= k_need
            return jnp.where(ok, cand, th), jnp.where(ok, above, cnt)

        return lax.fori_loop(0, 15, bit_step, (th0, above0))

    k_full = jnp.full((TQ, LANES), float(k_sel), F32)
    th_hi, above_hi = search16(hi_ref, k_full)
    th_hi16 = th_hi.astype(I16)
    k_lo = k_full - above_hi

    def bucket_chunk(c, carry):
        for a in range(nb):
            blk = c * nb + a
            lo_ref[blk] = jnp.where(hi_ref[blk] == th_hi16, lo_ref[blk], jnp.int16(I16_MIN))
        return carry

    lax.fori_loop(0, 2 * n_pairs, bucket_chunk, 0)
    th_lo, above_lo = search16(lo_ref, k_lo)
    need = k_lo - above_lo
    th_lo16 = th_lo.astype(I16)
    one16, zero16 = jnp.int16(1), jnp.int16(0)

    def flags(blk):
        hi, lom = hi_ref[blk], lo_ref[blk]
        in_bucket = hi == th_hi16
        above = jnp.where(hi > th_hi16, one16, jnp.where(lom > th_lo16, one16, zero16))
        tied = jnp.where(in_bucket, jnp.where(lom == th_lo16, one16, zero16), zero16)
        return above.astype(I32).astype(F32), tied.astype(I32).astype(F32)

    def sel_pair(cp, carry):
        pcs, fl = [], []
        for half in range(2):
            fl.append([flags((2 * cp + half) * nb + a) for a in range(nb)])
            tie = jnp.concatenate([t for _, t in fl[half]], axis=1).astype(BF16)
            pcs.append(_dot(tie, tri_ref[...]))
        for half in range(2):
            for a in range(nb):
                blk = (2 * cp + half) * nb + a
                above, tied = fl[half][a]
                rank = pcs[half][:, a * LANES:(a + 1) * LANES] + carry
                tied_in = jnp.where(tied > 0.0, jnp.where(rank <= need, 0.0, NEG_INF), NEG_INF)
                sb = jnp.where(above > 0.0, 0.0, tied_in)
                selb_ref[blk] = jnp.where(blk * LANES + lane_c <= row_t, sb, NEG_INF)
            carry = carry + pcs[half][:, SCORE_COLS:SCORE_COLS + LANES]
        return carry

    lax.fori_loop(0, n_pairs, sel_pair, jnp.zeros((TQ, LANES), F32))

    qb = qb_ref[...]
    scale = (DSA_HEAD_DIM ** -0.5) * LOG2_E
    for h in range(DSA_HEADS):
        qh = qb[:, h * DSA_HEAD_DIM:(h + 1) * DSA_HEAD_DIM].astype(BF16)
        ql_ref[h * TQ:(h + 1) * TQ, :] = (_dot(qh, wuk_ref[h]) * scale).astype(BF16)

    m_ref[...] = jnp.full(m_ref.shape, -1e30, F32)
    acc_ref[...] = jnp.zeros(acc_ref.shape, F32)

    def attend(c, near):
        c0 = pl.multiple_of(c * SCORE_COLS, SCORE_COLS)
        ckx = ckv_ref[pl.ds(c0, SCORE_COLS), :]
        ck = ckx[:, :KV_RANK]
        lg = _dot_t(ql_ref[...], ck)
        for h in range(DSA_HEADS):
            blk_max = None
            for a in range(nb):
                s = lg[h * TQ:(h + 1) * TQ, a * LANES:(a + 1) * LANES] + selb_ref[c * nb + a]
                if near:
                    q_blocks = TQ // LANES
                    s = s + jnp.concatenate(
                        [bias_ref[h, jnp.clip(c * nb + a - (qt * q_blocks + r) + 2, 0, 3)]
                         for r in range(q_blocks)], axis=0)
                s_ref[h, a] = s
                blk_max = s if blk_max is None else jnp.maximum(blk_max, s)
            m_old = m_ref[h]
            m_new = jnp.maximum(m_old, jnp.broadcast_to(jnp.max(blk_max, axis=-1, keepdims=True),
                                                        (TQ, LANES)))
            al_ref[h] = jnp.exp2(m_old - m_new)
            m_ref[h] = m_new
        for g0 in range(0, DSA_HEADS, PV_GROUP):
            for h in range(g0, g0 + PV_GROUP):
                m_new = m_ref[h]
                for a in range(nb):
                    sl = slice(a * LANES, (a + 1) * LANES)
                    p_ref[h * TQ:(h + 1) * TQ, sl] = jnp.exp2(s_ref[h, a] - m_new).astype(BF16)
            pv = _dot(p_ref[g0 * TQ:(g0 + PV_GROUP) * TQ, :], ckx)
            for h in range(g0, g0 + PV_GROUP):
                alpha = al_ref[h]
                acc_ref[h] = (jnp.concatenate([alpha, alpha], axis=1) * acc_ref[h]
                              + pv[(h - g0) * TQ:(h - g0 + 1) * TQ])

    def far(c, carry):
        attend(c, False)
        return carry

    lax.fori_loop(0, jnp.maximum(c_diag - 1, 0), far, 0)

    @pl.when(c_diag > 0)
    def _():
        attend(c_diag - 1, True)

    attend(c_diag, True)

    out = jnp.zeros((TQ, DSA_WIDTH), F32)
    for h in range(DSA_HEADS):
        o_lat = acc_ref[h, :, :KV_RANK] / acc_ref[h, :, KV_RANK:]
        out = out + _dot(o_lat.astype(BF16), wuv_ref[h])
    o_ref[...] = out


def _rel_bucket_table(n):
    max_exact = REL_BUCKETS // 2
    d = np.arange(n)
    df = np.maximum(d, 1).astype(np.float32)
    large = max_exact + (np.log(df / np.float32(max_exact)) / np.float32(math.log(REL_MAX_DIST / max_exact))
                         * np.float32(REL_BUCKETS - max_exact)).astype(np.int32)
    large = np.minimum(large, REL_BUCKETS - 1)
    return np.where(d < max_exact, d, large)


def _dsa_attention(h_main, h_idx, ckv, kp, w_uk, w_uv, rel_bias, B, L):
    T = B * L
    nq = L // TQ
    k_sel = min(DSA_TOPK, L // 4)
    assert SCORE_COLS == 4 * LANES and SCORE_COLS >= k_sel
    blk = LANES
    dist = blk + np.arange(blk)[:, None] - np.arange(2 * blk)[None, :]
    assert _rel_bucket_table(REL_MAX_DIST * 4)[blk:].min() == REL_BUCKETS - 1
    bucket = _rel_bucket_table(2 * blk + 1)[np.maximum(dist, 0)]
    rb = rel_bias.astype(F32)
    onehot = (jnp.asarray(bucket, I32)[:, :, None] == jnp.arange(REL_BUCKETS, dtype=I32)).astype(F32)
    near = jnp.einsum("qkb,bh->hqk", onehot, rb - rb[REL_BUCKETS - 1], precision=lax.Precision.HIGHEST)
    bias_near = jnp.pad(near * LOG2_E, ((0, 0), (0, 0), (blk, blk)))
    bias_near = bias_near.reshape(DSA_HEADS, blk, 4, blk).transpose(0, 2, 1, 3)
    assert L % (2 * SCORE_COLS) == 0 and TQ % LANES == 0 and SCORE_COLS % TQ == 0
    once = pl.Buffered(1)
    u = np.arange(SCORE_COLS)[:, None]
    v = np.arange(SCORE_COLS + LANES)[None, :]
    tri = jnp.asarray((u <= v) | (v >= SCORE_COLS), BF16)
    wuv_band = jnp.zeros((DSA_HEADS, KV_RANK, DSA_WIDTH), F32)
    for h in range(DSA_HEADS):
        wuv_band = wuv_band.at[h, :, h * DSA_HEAD_DIM:(h + 1) * DSA_HEAD_DIM].set(w_uv[h])
    kern = functools.partial(_dsa_kernel, k_sel=k_sel)
    lpad = -(-L // (2 * SCORE_COLS)) * (2 * SCORE_COLS)
    return pl.pallas_call(
        kern,
        grid=(B, nq),
        in_specs=[
            pl.BlockSpec((TQ, DSA_WIDTH), lambda b, i: (b * nq + i, 1)),
            pl.BlockSpec((TQ, IDX_HEADS * IDX_DIM), lambda b, i: (b * nq + i, 0)),
            pl.BlockSpec((TQ, LANES), lambda b, i: (b * nq + i, 2)),
            pl.BlockSpec((L, 4 * IDX_DIM), lambda b, i: (b, 0), pipeline_mode=once),
            pl.BlockSpec((L, 2 * KV_RANK), lambda b, i: (b, 0), pipeline_mode=once),
            pl.BlockSpec((DSA_HEADS, DSA_HEAD_DIM, KV_RANK), lambda b, i: (0, 0, 0)),
            pl.BlockSpec((DSA_HEADS, KV_RANK, DSA_WIDTH), lambda b, i: (0, 0, 0)),
            pl.BlockSpec((DSA_HEADS, 4, blk, blk), lambda b, i: (0, 0, 0, 0), pipeline_mode=once),
            pl.BlockSpec((SCORE_COLS, SCORE_COLS + LANES), lambda b, i: (0, 0)),
        ],
        out_specs=pl.BlockSpec((TQ, DSA_WIDTH), lambda b, i: (b * nq + i, 0)),
        out_shape=jax.ShapeDtypeStruct((T, DSA_WIDTH), F32),
        scratch_shapes=[
            pltpu.VMEM((lpad // LANES, TQ, LANES), I16),
            pltpu.VMEM((lpad // LANES, TQ, LANES), I16),
            pltpu.VMEM((lpad // LANES, TQ, LANES), F32),
            pltpu.VMEM((DSA_HEADS * TQ, KV_RANK), BF16),
            pltpu.VMEM((DSA_HEADS, SCORE_COLS // LANES, TQ, LANES), F32),
            pltpu.VMEM((DSA_HEADS * TQ, SCORE_COLS), BF16),
            pltpu.VMEM((DSA_HEADS, TQ, LANES), F32),
            pltpu.VMEM((DSA_HEADS, TQ, LANES), F32),
            pltpu.VMEM((DSA_HEADS, TQ, 2 * KV_RANK), F32),
        ],
        compiler_params=_cparams(("parallel", "arbitrary")),
        name="dsa_attention",
    )(h_main, h_idx, h_idx, kp, ckv, w_uk.astype(BF16), wuv_band.astype(BF16), bias_near, tri)


def _mlstm_prep_kernel(cur_ref, halo_ref, w_ref, b_ref, wqh_ref, wql_ref, wkh_ref, wkl_ref,
                       q_ref, k_ref, xbuf, *, tl):
    i = pl.program_id(1)
    xbuf[SUBLANES:SUBLANES + tl, :] = cur_ref[...]
    xbuf[0:SUBLANES, :] = jnp.where(i > 0, halo_ref[...], 0.0)
    base = SUBLANES - (MLSTM_CONV - 1)
    acc = jnp.broadcast_to(b_ref[...], (tl, MLSTM_WIDTH))
    for k in range(MLSTM_CONV):
        acc = acc + w_ref[k:k + 1, :] * xbuf[base + k:base + k + tl, :]
    xc = acc * jax.nn.sigmoid(acc)
    q_ref[...] = _dot3(xc, wqh_ref[...], wql_ref[...]) * (MLSTM_HEAD_DIM ** -0.5)
    k_ref[...] = _dot3(xc, wkh_ref[...], wkl_ref[...])


def _block_diag(w):
    h, d, e = w.shape
    out = jnp.zeros((h * d, h * e), F32)
    for i in range(h):
        out = out.at[i * d:(i + 1) * d, i * e:(i + 1) * e].set(w[i])
    return out


def _mlstm_prep(h_main, conv_w, conv_b, w_qm, w_km, B, L, tl=512):
    T = B * L
    nl = L // tl
    hb = tl // SUBLANES
    wq = _wsplit(_block_diag(w_qm))
    wk = _wsplit(_block_diag(w_km))
    full = lambda shape: pl.BlockSpec(shape, lambda b, i: (0,) * len(shape))
    return pl.pallas_call(
        functools.partial(_mlstm_prep_kernel, tl=tl),
        grid=(B, nl),
        in_specs=[
            pl.BlockSpec((tl, MLSTM_WIDTH), lambda b, i: (b * nl + i, 4)),
            pl.BlockSpec((SUBLANES, MLSTM_WIDTH),
                         lambda b, i: (jnp.maximum((b * nl + i) * hb - 1, 0), 4)),
            full((MLSTM_CONV, MLSTM_WIDTH)), full((1, MLSTM_WIDTH)),
            full((MLSTM_WIDTH, MLSTM_WIDTH)), full((MLSTM_WIDTH, MLSTM_WIDTH)),
            full((MLSTM_WIDTH, MLSTM_WIDTH)), full((MLSTM_WIDTH, MLSTM_WIDTH)),
        ],
        out_specs=[pl.BlockSpec((tl, MLSTM_WIDTH), lambda b, i: (b * nl + i, 0)),
                   pl.BlockSpec((tl, MLSTM_WIDTH), lambda b, i: (b * nl + i, 0))],
        out_shape=[jax.ShapeDtypeStruct((T, MLSTM_WIDTH), F32),
                   jax.ShapeDtypeStruct((T, MLSTM_WIDTH), F32)],
        scratch_shapes=[pltpu.VMEM((SUBLANES + tl, MLSTM_WIDTH), F32)],
        compiler_params=_cparams(("parallel", "parallel")),
        name="mlstm_prep",
    )(h_main, h_main, conv_w, conv_b.reshape(1, -1), wq[0], wq[1], wk[0], wk[1])


ML_ROWS = 128


def _mlstm_kernel(q_ref, k_ref, v_ref, o_ref, sm_ref, bi_ref, bf_ref, g_ref, out_ref,
                  cm_ref, n_ref, m_ref):
    C = MLSTM_CHUNK
    dh = MLSTM_HEAD_DIM

    @pl.when(pl.program_id(1) == 0)
    def _():
        cm_ref[...] = jnp.zeros(cm_ref.shape, F32)
        n_ref[...] = jnp.zeros(n_ref.shape, F32)
        m_ref[...] = jnp.zeros(m_ref.shape, F32)

    sm_t = sm_ref[...].T
    ig = sm_t[SM_I:SM_I + SUBLANES] + bi_ref[...]
    fg = jax.nn.log_sigmoid(sm_t[SM_F:SM_F + SUBLANES] + bf_ref[...])
    lane = lax.broadcasted_iota(I32, (SUBLANES, ML_ROWS), 1) & (C - 1)
    bcum = fg
    s = 1
    while s < C:
        bcum = bcum + jnp.where(lane >= s, pltpu.roll(bcum, s, axis=1), 0.0)
        s *= 2
    cols = jnp.concatenate([bcum, ig, jnp.zeros((LANES - 2 * SUBLANES, ML_ROWS), F32)], axis=0).T
    tri = (lax.broadcasted_iota(I32, (C, C), 1) <= lax.broadcasted_iota(I32, (C, C), 0))

    q_all, k_all, v_all, o_all = q_ref[...], k_ref[...], v_ref[...], o_ref[...]
    g_all = g_ref[...]
    for c in range(ML_ROWS // C):
        r0 = c * C
        for h in range(MLSTM_HEADS):
            hs = slice(h * dh, (h + 1) * dh)
            qj = q_all[r0:r0 + C, hs]
            kj = k_all[r0:r0 + C, hs]
            vj = v_all[r0:r0 + C, hs]
            b_row = bcum[h:h + 1, r0:r0 + C]
            i_row = ig[h:h + 1, r0:r0 + C]
            b_col = cols[r0:r0 + C, h:h + 1]
            i_col = cols[r0:r0 + C, SUBLANES + h:SUBLANES + h + 1]
            m_prev = m_ref[h:h + 1, 0:1]
            n_prev = n_ref[h:h + 1, :]
            cm_prev = cm_ref[h]

            dm = jnp.where(tri, b_col - b_row + i_row, NEG_INF)
            inter = b_col + m_prev
            m_row = jnp.maximum(inter, jnp.max(dm, axis=-1, keepdims=True))
            w_inter = jnp.exp(inter - m_row)
            qb, kb, vb = qj.astype(BF16), kj.astype(BF16), vj.astype(BF16)
            sw = _dot_t(qb, kb) * jnp.exp(dm - m_row)
            num = _dot(sw.astype(BF16), vb) + w_inter * _dot(qb, cm_prev.astype(BF16))
            den = (jnp.sum(sw, axis=-1, keepdims=True)
                   + w_inter * jnp.sum(qj * n_prev, axis=-1, keepdims=True))
            hh = num / jnp.maximum(jnp.abs(den), jnp.exp(-m_row))

            b_last = b_row[:, C - 1:C]
            g_row = b_last - b_row + i_row
            g_col = b_last - b_col + i_col
            m_new = jnp.maximum(b_last + m_prev, jnp.max(g_row, axis=-1, keepdims=True))
            decay = jnp.exp(b_last + m_prev - m_new)
            kw = kj * jnp.exp(g_col - m_new)
            cm_ref[h] = decay * cm_prev + _dot_tl(kw.astype(BF16), vb)
            n_ref[h:h + 1, :] = decay * n_prev + jnp.sum(kw, axis=0, keepdims=True)
            m_ref[h:h + 1, :] = jnp.broadcast_to(m_new, (1, LANES))

            mu = jnp.mean(hh, axis=-1, keepdims=True)
            hc = hh - mu
            var = jnp.mean(hc * hc, axis=-1, keepdims=True)
            hn = hc * lax.rsqrt(var + LN_EPS) * g_all[:, hs]
            out_ref[r0:r0 + C, hs] = jax.nn.sigmoid(o_all[r0:r0 + C, hs]) * hn


def _mlstm(q, k, h_main, h_idx, b_i, b_f, norm_g, B, L):
    T = B * L
    nl = L // ML_ROWS
    pad8 = lambda v: jnp.pad(v.astype(F32), (0, SUBLANES - MLSTM_HEADS)).reshape(SUBLANES, 1)
    row = lambda col: pl.BlockSpec((ML_ROWS, MLSTM_WIDTH), lambda b, i: (b * nl + i, col))
    return pl.pallas_call(
        _mlstm_kernel,
        grid=(B, nl),
        in_specs=[row(0), row(0), row(5), row(6),
                  pl.BlockSpec((ML_ROWS, LANES), lambda b, i: (b * nl + i, 2)),
                  pl.BlockSpec((SUBLANES, 1), lambda b, i: (0, 0)),
                  pl.BlockSpec((SUBLANES, 1), lambda b, i: (0, 0)),
                  pl.BlockSpec((1, MLSTM_WIDTH), lambda b, i: (0, 0))],
        out_specs=row(0),
        out_shape=jax.ShapeDtypeStruct((T, MLSTM_WIDTH), F32),
        scratch_shapes=[pltpu.VMEM((MLSTM_HEADS, MLSTM_HEAD_DIM, MLSTM_HEAD_DIM), F32),
                        pltpu.VMEM((SUBLANES, MLSTM_HEAD_DIM), F32),
                        pltpu.VMEM((SUBLANES, LANES), F32)],
        compiler_params=_cparams(("parallel", "arbitrary")),
        name="mlstm_scan",
    )(q, k, h_main, h_main, h_idx, pad8(b_i), pad8(b_f), norm_g.reshape(1, -1))


def _mix_out_kernel(ya_ref, yb_ref, yc_ref, x_ref, wa_ref, wb_ref, wc_ref, g_ref, b_ref, o_ref, *,
                    alpha):
    y = (_dot(ya_ref[...].astype(BF16), wa_ref[...])
         + _dot(yb_ref[...].astype(BF16), wb_ref[...])
         + _dot(yc_ref[...].astype(BF16), wc_ref[...]))
    o_ref[...] = _layer_norm(alpha * x_ref[...] + y, g_ref[...], b_ref[...])


def _mix_out(y_a, y_b, y_c, x, w_out, g, b, alpha, tm=512):
    T, D = x.shape
    w = w_out.astype(BF16)
    wa, wb, wc = w[:CONV_CH], w[CONV_CH:CONV_CH + DSA_WIDTH], w[CONV_CH + DSA_WIDTH:]
    rows = lambda width: pl.BlockSpec((tm, width), lambda i: (i, 0))
    full = lambda a: pl.BlockSpec(a.shape, lambda i: (0, 0))
    g2, b2 = g.reshape(1, D), b.reshape(1, D)
    return pl.pallas_call(
        functools.partial(_mix_out_kernel, alpha=alpha),
        grid=(T // tm,),
        in_specs=[rows(CONV_CH), rows(DSA_WIDTH), rows(MLSTM_WIDTH), rows(D),
                  full(wa), full(wb), full(wc), full(g2), full(b2)],
        out_specs=rows(D),
        out_shape=jax.ShapeDtypeStruct((T, D), F32),
        compiler_params=_cparams(("parallel",)),
        name="mix_out",
    )(y_a, y_b, y_c, x, wa, wb, wc, g2, b2)


def _xattn_kernel(x_ref, kv_ref, wq_ref, wo_ref, g_ref, b_ref, o_ref, *, alpha):
    x = x_ref[...]
    q = _dot(x.astype(BF16), wq_ref[...])
    kv = kv_ref[...]
    scale = XATTN_HEAD_DIM ** -0.5
    outs = []
    for h in range(XATTN_HEADS):
        hs = slice(h * XATTN_HEAD_DIM, (h + 1) * XATTN_HEAD_DIM)
        kh = kv[:, hs].astype(BF16)
        vh = kv[:, D_MODEL + h * XATTN_HEAD_DIM:D_MODEL + (h + 1) * XATTN_HEAD_DIM].astype(BF16)
        lg = _dot_t(q[:, hs].astype(BF16), kh) * scale
        lg = lg - jnp.max(lg, axis=-1, keepdims=True)
        p = jnp.exp(lg)
        p = p / jnp.sum(p, axis=-1, keepdims=True)
        outs.append(_dot(p.astype(BF16), vh))
    o = jnp.concatenate(outs, axis=-1)
    y = _dot(o.astype(BF16), wo_ref[...])
    o_ref[...] = _layer_norm(alpha * x + y, g_ref[...], b_ref[...])


def _xattn(x, kv, w_q, w_o, g, b, alpha, B, L, tm=512):
    T, D = x.shape
    nl = L // tm
    M = kv.shape[0] // B
    full = lambda a: pl.BlockSpec(a.shape, lambda bb, i: (0, 0))
    g2, b2 = g.reshape(1, D), b.reshape(1, D)
    wq, wo = w_q.astype(BF16), w_o.astype(BF16)
    return pl.pallas_call(
        functools.partial(_xattn_kernel, alpha=alpha),
        grid=(B, nl),
        in_specs=[pl.BlockSpec((tm, D), lambda bb, i: (bb * nl + i, 0)),
                  pl.BlockSpec((M, 2 * D), lambda bb, i: (bb, 0)),
                  full(wq), full(wo), full(g2), full(b2)],
        out_specs=pl.BlockSpec((tm, D), lambda bb, i: (bb * nl + i, 0)),
        out_shape=jax.ShapeDtypeStruct((T, D), F32),
        compiler_params=_cparams(("parallel", "parallel")),
        name="xattn",
    )(x, kv, wq, wo, g2, b2)


def _peer_score_kernel(x_ref, w_ref, k1h_ref, k1l_ref, k2h_ref, k2l_ref, st_ref):
    q = _dot(x_ref[...].astype(BF16), w_ref[...])
    half = PEER_QDIM // 2
    for h in range(PEER_HEADS):
        for part, (kh_ref, kl_ref) in enumerate(((k1h_ref, k1l_ref), (k2h_ref, k2l_ref))):
            c0 = h * PEER_QDIM + part * half
            q_hi, q_lo = _split(q[:, c0:c0 + half])
            kh, kl = kh_ref[...], kl_ref[...]
            st_ref[2 * h + part] = _dot_t(kh, q_hi) + _dot_t(kl, q_hi) + _dot_t(kh, q_lo)


def _peer_scores(x, w_pq, sub_k1, sub_k2, tm=512):
    T, D = x.shape
    w = w_pq.astype(BF16)
    k1h, k1l = _wsplit(sub_k1)
    k2h, k2l = _wsplit(sub_k2)
    full = lambda a: pl.BlockSpec(a.shape, lambda i: (0, 0))
    return pl.pallas_call(
        _peer_score_kernel,
        grid=(T // tm,),
        in_specs=[pl.BlockSpec((tm, D), lambda i: (i, 0)), full(w),
                  full(k1h), full(k1l), full(k2h), full(k2l)],
        out_specs=pl.BlockSpec((2 * PEER_HEADS, PEER_KEYS, tm), lambda i: (0, 0, i)),
        out_shape=jax.ShapeDtypeStruct((2 * PEER_HEADS, PEER_KEYS, T), F32),
        compiler_params=_cparams(("parallel",)),
        name="peer_scores",
    )(x, w, k1h, k1l, k2h, k2l)


PEER_NTOP = PEER_TOPK + 1
PEER_PAIR_ROWS = tuple(PEER_NTOP // (k + 1) for k in range(PEER_NTOP))
PEER_CAND_ROWS = -(-sum(PEER_PAIR_ROWS) // SUBLANES) * SUBLANES


def _peer_thr_kernel(st_ref, stats_ref, v2_ref, cand_ref):
    def top_rows(x):
        rows = []
        for _ in range(PEER_NTOP):
            m = jnp.max(x, axis=0, keepdims=True)
            rows.append(m)
            x = jnp.where(x == m, NEG_INF, x)
        return rows

    v1 = top_rows(st_ref[0])
    v2 = top_rows(st_ref[1])
    for k in range(PEER_NTOP):
        v2_ref[k:k + 1, :] = v2[k]
    r = 0
    for k, n in enumerate(PEER_PAIR_ROWS):
        cand_ref[r:r + n, :] = v1[k] + v2_ref[0:n, :]
        r += n
    cand_ref[r:PEER_CAND_ROWS, :] = jnp.full((PEER_CAND_ROWS - r, cand_ref.shape[1]), NEG_INF, F32)
    cand = cand_ref[...]
    x = cand
    for _ in range(PEER_TOPK - 1):
        m = jnp.max(x, axis=0, keepdims=True)
        x = jnp.where(x == m, NEG_INF, x)
    thr = jnp.max(x, axis=0, keepdims=True)
    nxt = jnp.max(jnp.where(x == thr, NEG_INF, x), axis=0, keepdims=True)
    top = v1[0] + v2[0]
    z = jnp.sum(jnp.where(cand >= thr, jnp.exp(cand - top), 0.0), axis=0, keepdims=True)
    cut = jnp.where(nxt > NEG_INF, 0.5 * thr + 0.5 * nxt, thr)
    pad = jnp.zeros((SUBLANES - 4, thr.shape[1]), F32)
    stats_ref[0] = jnp.concatenate([cut, v1[0], v2[0], 1.0 / z, pad], axis=0)


def _peer_thresholds(st, tm=1024):
    T = st.shape[2]
    return pl.pallas_call(
        _peer_thr_kernel,
        grid=(PEER_HEADS, T // tm),
        in_specs=[pl.BlockSpec((2, PEER_KEYS, tm), lambda h, i: (h, 0, i))],
        out_specs=pl.BlockSpec((1, SUBLANES, tm), lambda h, i: (h, 0, i)),
        out_shape=jax.ShapeDtypeStruct((PEER_HEADS, SUBLANES, T), F32),
        scratch_shapes=[pltpu.VMEM((3 * SUBLANES, tm), F32),
                        pltpu.VMEM((PEER_CAND_ROWS, tm), F32)],
        compiler_params=_cparams(("parallel", "parallel")),
        name="peer_thresholds",
    )(st)


PEER_TL = 512
PEER_ET = 512
PEER_SUB = 64
GELU_FOLD = 2.0 ** -0.5

def _peer_mix_kernel(x_ref, st_ref, stats_ref, u_ref, vt_ref, g_ref, b_ref, o_ref,
                     xb_ref, d1_ref, e1_ref, e2_ref, gate_ref, w_ref, acc_ref, *, alpha):
    j = pl.program_id(1)
    n_tiles = pl.num_programs(1) - 1
    sub_tiles = PEER_ET // PEER_KEYS
    slot = j % 2

    @pl.when(j == 0)
    def _():
        xb_ref[...] = x_ref[...].astype(BF16)
        for h in range(PEER_HEADS):
            st = stats_ref[h]
            s1 = st_ref[2 * h]
            d1_ref[h] = st[0:1] - s1
            e1_ref[h] = jnp.exp(s1 - st[1:2]) * (st[3:4] * GELU_FOLD)
            e2_ref[h] = jnp.exp(st_ref[2 * h + 1] - st[2:3])
        acc_ref[...] = jnp.zeros(acc_ref.shape, F32)
        w_ref[...] = jnp.zeros(w_ref.shape, BF16)

    def gate_block(a, lb):
        i1 = jnp.minimum(j, n_tiles - 1) * sub_tiles + a
        ls = slice(lb * LANES, (lb + 1) * LANES)
        d_rows = [d1_ref[h, pl.ds(i1, 1), :][:, ls] for h in range(PEER_HEADS)]
        c_rows = [e1_ref[h, pl.ds(i1, 1), :][:, ls] for h in range(PEER_HEADS)]
        for k0 in range(0, PEER_KEYS, PEER_SUB):
            ks = slice(k0, k0 + PEER_SUB)
            gate = None
            for h in range(PEER_HEADS):
                picked = st_ref[2 * h + 1, ks, ls] >= d_rows[h]
                term = jnp.where(picked, e2_ref[h, ks, ls] * c_rows[h], 0.0)
                gate = term if gate is None else gate + term
            gate_ref[a * PEER_KEYS + k0:a * PEER_KEYS + k0 + PEER_SUB, ls] = gate

    for a in range(sub_tiles):
        for lb in range(PEER_TL // LANES):
            gate_block(a, lb)
    acc_ref[...] += _dot(vt_ref[...], w_ref[1 - slot])
    su = _dot_t(u_ref[...], xb_ref[...])
    act = su + su * lax.erf(su)
    w_ref[slot] = (gate_ref[...] * act).astype(BF16)

    @pl.when(j == n_tiles)
    def _():
        y = acc_ref[...].T
        o_ref[...] = _layer_norm(alpha * x_ref[...] + y, g_ref[...], b_ref[...])


def _peer_mix(x, st, stats, peer_u, peer_v, g, b, alpha):
    T, D = x.shape
    E = peer_u.shape[0]
    u = (peer_u * GELU_FOLD).astype(BF16)
    vt = peer_v.astype(BF16).T
    g2, b2 = g.reshape(1, D), b.reshape(1, D)
    tl, et = PEER_TL, PEER_ET
    n_e = E // et
    return pl.pallas_call(
        functools.partial(_peer_mix_kernel, alpha=alpha),
        grid=(T // tl, n_e + 1),
        in_specs=[pl.BlockSpec((tl, D), lambda i, j: (i, 0)),
                  pl.BlockSpec((2 * PEER_HEADS, PEER_KEYS, tl), lambda i, j: (0, 0, i)),
                  pl.BlockSpec((PEER_HEADS, SUBLANES, tl), lambda i, j: (0, 0, i)),
                  pl.BlockSpec((et, D), lambda i, j: (jnp.minimum(j, n_e - 1), 0)),
                  pl.BlockSpec((D, et), lambda i, j: (0, jnp.maximum(j - 1, 0))),
                  pl.BlockSpec((1, D), lambda i, j: (0, 0)),
                  pl.BlockSpec((1, D), lambda i, j: (0, 0))],
        out_specs=pl.BlockSpec((tl, D), lambda i, j: (i, 0)),
        out_shape=jax.ShapeDtypeStruct((T, D), F32),
        scratch_shapes=[pltpu.VMEM((tl, D), BF16),
                        pltpu.VMEM((PEER_HEADS, PEER_KEYS, tl), F32),
                        pltpu.VMEM((PEER_HEADS, PEER_KEYS, tl), F32),
                        pltpu.VMEM((PEER_HEADS, PEER_KEYS, tl), F32),
                        pltpu.VMEM((et, tl), F32),
                        pltpu.VMEM((2, et, tl), BF16),
                        pltpu.VMEM((D, tl), F32)],
        compiler_params=_cparams(("parallel", "arbitrary")),
        name="peer_mix",
    )(x, st, stats, u, vt, g2, b2)


def _pack_w_in(w):
    sizes = (512, 512, 128, 256, 64, 4, 256, 256, 256, 4, 4)
    offs = np.concatenate([[0], np.cumsum(sizes)])
    seg = lambda n: w[:, int(offs[n]):int(offs[n + 1])]
    a_in, q_b, c_b, qi, ki, wi, xc, v_m, o_m, i_m, f_m = (seg(n) for n in range(len(sizes)))
    z = lambda n: jnp.zeros((w.shape[0], n), w.dtype)
    main = jnp.concatenate([a_in, q_b, xc, v_m, o_m, c_b], axis=1)
    idx = jnp.concatenate([qi, ki, wi, z(4), i_m, z(4), f_m, z(IDX_W - 256 - SM_F - 4)], axis=1)
    return main, idx


def kernel(x, mem, ln_in_g, ln_in_b, rel_bias, w_in, conv_a_w, conv_a_b, norm_a_g, norm_a_b,
           kv_norm_g, w_uk, w_uv, conv_m_w, conv_m_b, w_qm, w_km, b_i, b_f, norm_m_g, w_out,
           ln1_g, ln1_b, w_cq, w_ckv, w_co, ln2_g, ln2_b, w_pq, sub_k1, sub_k2, peer_u, peer_v,
           ln3_g, ln3_b):
    B, L, D = x.shape
    T = B * L
    depth = w_in.shape[0]
    alpha = (2.0 * depth) ** 0.25
    xs = _entry_ln(x.reshape(T, D), ln_in_g, ln_in_b)
    mem2 = mem.reshape(-1, D)
    for l in range(depth):
        w_main, w_idx = _pack_w_in(w_in[l])
        h_main = _matmul(xs, w_main, 1, 512, MAIN_W, "proj_main")
        h_idx = _matmul(xs, w_idx, 3, 512, IDX_W, "proj_idx")
        y_a = _conv_group(h_main, conv_a_w[l], conv_a_b[l], norm_a_g[l], norm_a_b[l], B, L)
        ckv, kp = _dsa_prep(h_main, h_idx, kv_norm_g[l])
        y_b = _dsa_attention(h_main, h_idx, ckv, kp, w_uk[l], w_uv[l], rel_bias, B, L)
        q_m, k_m = _mlstm_prep(h_main, conv_m_w[l], conv_m_b[l], w_qm[l], w_km[l], B, L)
        y_c = _mlstm(q_m, k_m, h_main, h_idx, b_i[l], b_f[l], norm_m_g[l], B, L)
        xs = _mix_out(y_a, y_b, y_c, xs, w_out[l], ln1_g[l], ln1_b[l], alpha)

        kv = _matmul(mem2, w_ckv[l], 1, mem2.shape[0], 512, "xattn_kv")
        xs = _xattn(xs, kv, w_cq[l], w_co[l], ln2_g[l], ln2_b[l], alpha, B, L)

        st = _peer_scores(xs, w_pq[l], sub_k1[l], sub_k2[l])
        stats = _peer_thresholds(st)
        xs = _peer_mix(xs, st, stats, peer_u[l], peer_v[l], ln3_g[l], ln3_b[l], alpha)
    return xs.reshape(B, L, D)
```

```python
import functools
import math

import numpy as np
import jax
import jax.numpy as jnp
from jax import lax
from jax.experimental import pallas as pl
from jax.experimental.pallas import tpu as pltpu

F32 = jnp.float32
BF16 = jnp.bfloat16
I32 = jnp.int32

D_MODEL = 1024
CONV_CH = 256
CONV_WIDTH = 31
DSA_HEADS = 8
DSA_HEAD_DIM = 64
DSA_WIDTH = 512
KV_RANK = 128
IDX_HEADS = 4
IDX_DIM = 64
DSA_TOPK = 256
MLSTM_HEADS = 4
MLSTM_HEAD_DIM = 64
MLSTM_WIDTH = 256
MLSTM_CONV = 4
MLSTM_CHUNK = 64
REL_BUCKETS = 32
REL_MAX_DIST = 128
XATTN_HEADS = 4
XATTN_HEAD_DIM = 256
PEER_HEADS = 8
PEER_KEYS = 128
PEER_QDIM = 256
PEER_TOPK = 16
LN_EPS = 1e-5

LANES = 128
SUBLANES = 8
VMEM_LIMIT = 56 * 1024 * 1024

NEG_INF = float("-inf")
LOG2_E = 1.4426950408889634

MAIN_W = 1920
IDX_W = 384
SM_WI = 64
SM_I = 72
SM_F = 80


def _cparams(sem):
    return pltpu.CompilerParams(dimension_semantics=sem, vmem_limit_bytes=VMEM_LIMIT)


def _dot(a, b):
    return jnp.dot(a, b, preferred_element_type=F32)


def _dot_t(a, b):
    return lax.dot_general(a, b, (((1,), (1,)), ((), ())), preferred_element_type=F32)


def _dot_tl(a, b):
    return lax.dot_general(a, b, (((0,), (0,)), ((), ())), preferred_element_type=F32)


def _split(a):
    hi = a.astype(BF16)
    lo = (a - hi.astype(F32)).astype(BF16)
    return hi, lo


def _dot3(a, b_hi, b_lo, dot=_dot):
    a_hi, a_lo = _split(a)
    return dot(a_hi, b_hi) + dot(a_lo, b_hi) + dot(a_hi, b_lo)


def _layer_norm(x, g, b):
    mu = jnp.mean(x, axis=-1, keepdims=True)
    xc = x - mu
    var = jnp.mean(xc * xc, axis=-1, keepdims=True)
    return xc * lax.rsqrt(var + LN_EPS) * g + b


def _wsplit(w):
    hi = w.astype(BF16)
    lo = (w - hi.astype(F32)).astype(BF16)
    return hi, lo


def _ln_kernel(x_ref, g_ref, b_ref, o_ref):
    o_ref[...] = _layer_norm(x_ref[...], g_ref[...], b_ref[...])


def _entry_ln(x, g, b, tm=512):
    T, D = x.shape
    return pl.pallas_call(
        _ln_kernel,
        grid=(T // tm,),
        in_specs=[pl.BlockSpec((tm, D), lambda i: (i, 0)),
                  pl.BlockSpec((1, D), lambda i: (0, 0)),
                  pl.BlockSpec((1, D), lambda i: (0, 0))],
        out_specs=pl.BlockSpec((tm, D), lambda i: (i, 0)),
        out_shape=jax.ShapeDtypeStruct((T, D), F32),
        compiler_params=_cparams(("parallel",)),
        name="entry_ln",
    )(x, g.reshape(1, D), b.reshape(1, D))


def _mm1_kernel(x_ref, w_ref, o_ref):
    o_ref[...] = _dot(x_ref[...].astype(BF16), w_ref[...])


def _mm3_kernel(x_ref, wh_ref, wl_ref, o_ref):
    o_ref[...] = _dot3(x_ref[...], wh_ref[...], wl_ref[...])


def _matmul(x, w, passes, tm, tn, name):
    T, K = x.shape
    N = w.shape[1]
    x_spec = pl.BlockSpec((tm, K), lambda j, i: (i, 0))
    w_spec = pl.BlockSpec((K, tn), lambda j, i: (0, j))
    if passes == 1:
        kern, ws, w_specs = _mm1_kernel, (w.astype(BF16),), [w_spec]
    else:
        kern, ws, w_specs = _mm3_kernel, _wsplit(w), [w_spec, w_spec]
    return pl.pallas_call(
        kern,
        grid=(N // tn, T // tm),
        in_specs=[x_spec] + w_specs,
        out_specs=pl.BlockSpec((tm, tn), lambda j, i: (i, j)),
        out_shape=jax.ShapeDtypeStruct((T, N), F32),
        compiler_params=_cparams(("parallel", "parallel")),
        name=name,
    )(x, *ws)


CONV_HALO = 32
CONV_ROWS = 64


def _conv_kernel(cur_ref, halo_ref, w_ref, b_ref, g_ref, bb_ref, o_ref, ubuf, *, tl):
    i = pl.program_id(1)
    cur = cur_ref[...]
    ubuf[CONV_HALO:CONV_HALO + tl, :] = cur[:, :CONV_CH] * jax.nn.sigmoid(cur[:, CONV_CH:])
    hal = halo_ref[...]
    uh = hal[:, :CONV_CH] * jax.nn.sigmoid(hal[:, CONV_CH:])
    ubuf[0:CONV_HALO, :] = jnp.where(i > 0, uh, 0.0)
    base = CONV_HALO - (CONV_WIDTH - 1)
    for c in range(tl // CONV_ROWS):
        r0 = c * CONV_ROWS
        acc = jnp.broadcast_to(b_ref[...], (CONV_ROWS, CONV_CH))
        for k in range(CONV_WIDTH):
            acc = acc + w_ref[k:k + 1, :] * ubuf[r0 + base + k:r0 + base + k + CONV_ROWS, :]
        y = _layer_norm(acc, g_ref[...], bb_ref[...])
        o_ref[r0:r0 + CONV_ROWS, :] = y * jax.nn.sigmoid(y)


def _conv_group(h_main, conv_w, conv_b, ln_g, ln_b, B, L, tl=512):
    T = B * L
    nl = L // tl
    hb = tl // CONV_HALO
    return pl.pallas_call(
        functools.partial(_conv_kernel, tl=tl),
        grid=(B, nl),
        in_specs=[
            pl.BlockSpec((tl, 2 * CONV_CH), lambda b, i: (b * nl + i, 0)),
            pl.BlockSpec((CONV_HALO, 2 * CONV_CH),
                         lambda b, i: (jnp.maximum((b * nl + i) * hb - 1, 0), 0)),
            pl.BlockSpec((CONV_WIDTH, CONV_CH), lambda b, i: (0, 0)),
            pl.BlockSpec((1, CONV_CH), lambda b, i: (0, 0)),
            pl.BlockSpec((1, CONV_CH), lambda b, i: (0, 0)),
            pl.BlockSpec((1, CONV_CH), lambda b, i: (0, 0)),
        ],
        out_specs=pl.BlockSpec((tl, CONV_CH), lambda b, i: (b * nl + i, 0)),
        out_shape=jax.ShapeDtypeStruct((T, CONV_CH), F32),
        scratch_shapes=[pltpu.VMEM((CONV_HALO + tl, CONV_CH), F32)],
        compiler_params=_cparams(("parallel", "parallel")),
        name="conv_group",
    )(h_main, h_main, conv_w, conv_b.reshape(1, -1), ln_g.reshape(1, -1), ln_b.reshape(1, -1))


def _dsa_prep_kernel(c_ref, sm_ref, g_ref, ckv_ref, kp_ref):
    c = c_ref[...]
    ms = jnp.mean(c * c, axis=-1, keepdims=True)
    ckv = (c * lax.rsqrt(ms + LN_EPS) * g_ref[...]).astype(BF16)
    ckv_ref[...] = jnp.concatenate([ckv, jnp.ones_like(ckv)], axis=-1)
    k_hi, k_lo = _split(sm_ref[...][:, :IDX_DIM])
    kp_ref[...] = jnp.concatenate([k_hi, k_hi, k_lo, jnp.zeros_like(k_hi)], axis=-1)


def _dsa_prep(h_main, h_idx, kv_g, tm=512):
    T = h_main.shape[0]
    return pl.pallas_call(
        _dsa_prep_kernel,
        grid=(T // tm,),
        in_specs=[pl.BlockSpec((tm, KV_RANK), lambda i: (i, 14)),
                  pl.BlockSpec((tm, LANES), lambda i: (i, 2)),
                  pl.BlockSpec((1, KV_RANK), lambda i: (0, 0))],
        out_specs=[pl.BlockSpec((tm, 2 * KV_RANK), lambda i: (i, 0)),
                   pl.BlockSpec((tm, 4 * IDX_DIM), lambda i: (i, 0))],
        out_shape=[jax.ShapeDtypeStruct((T, 2 * KV_RANK), BF16),
                   jax.ShapeDtypeStruct((T, 4 * IDX_DIM), BF16)],
        compiler_params=_cparams(("parallel",)),
        name="dsa_prep",
    )(h_main, h_idx, kv_g.reshape(1, -1))


TQ = 128
SCORE_COLS = 512


I16 = jnp.int16
I16_MIN = -(2 ** 15)


def _dsa_kernel(qb_ref, qi_ref, sm_ref, kp_ref, ckv_ref, wuk_ref, wuv_ref, bias_ref, tri_ref, o_ref,
                hi_ref, lo_ref, selb_ref, ql_ref, s_ref, m_ref, al_ref, acc_ref, *, k_sel):
    qt = pl.program_id(1)
    nb = SCORE_COLS // LANES
    c_diag = ((qt + 1) * TQ - 1) // SCORE_COLS
    n_chunks = c_diag + 1
    n_pairs = (n_chunks + 1) // 2
    row_t = qt * TQ + lax.broadcasted_iota(I32, (TQ, LANES), 0)
    lane_c = lax.broadcasted_iota(I32, (TQ, LANES), 1)

    qi = qi_ref[...]
    sm = sm_ref[...]
    qp = []
    for h in range(IDX_HEADS):
        q_hi, q_lo = _split(qi[:, h * IDX_DIM:(h + 1) * IDX_DIM])
        qp.append(jnp.concatenate([q_hi, q_lo, q_hi, jnp.zeros_like(q_hi)], axis=-1))
    qp = jnp.concatenate(qp, axis=0)
    w_fold = (IDX_DIM ** -0.5) * (IDX_HEADS ** -0.5)
    ws = [jnp.broadcast_to(sm[:, SM_WI + h:SM_WI + h + 1] * w_fold, (TQ, LANES))
          for h in range(IDX_HEADS)]

    def score_pair(cp, carry):
        for half in range(2):
            c0 = pl.multiple_of((2 * cp + half) * SCORE_COLS, SCORE_COLS)
            d = _dot_t(qp, kp_ref[pl.ds(c0, SCORE_COLS), :])
            for a in range(nb):
                off = c0 + a * LANES
                s = jnp.zeros((TQ, LANES), F32)
                for h in range(IDX_HEADS):
                    s = s + jnp.maximum(d[h * TQ:(h + 1) * TQ, a * LANES:(a + 1) * LANES], 0.0) * ws[h]
                s = jnp.where(off + lane_c <= row_t, s + 0.0, NEG_INF)
                bits = lax.bitcast_convert_type(s, I32)
                key = bits ^ ((bits >> 31) & 0x7FFFFFFF)
                blk = (2 * cp + half) * nb + a
                hi_ref[blk] = (key >> 16).astype(I16)
                lo_ref[blk] = ((key & 0xFFFF) + I16_MIN).astype(I16)
        return carry

    lax.fori_loop(0, n_pairs, score_pair, 0)

    def count16(ref, pred):
        def body(c, acc):
            for a in range(2 * nb):
                acc = acc + jnp.where(pred(ref[c * 2 * nb + a]), jnp.int16(1), jnp.int16(0))
            return acc
        acc = lax.fori_loop(0, n_pairs, body, jnp.zeros((TQ, LANES), I16))
        tot = jnp.sum(acc.astype(I32).astype(F32), axis=-1, keepdims=True)
        return jnp.broadcast_to(tot, (TQ, LANES))

    def search16(ref, k_need):
        c_nonneg = count16(ref, lambda blk: blk >= jnp.int16(0))
        ok0 = c_nonneg >= k_need
        th0 = jnp.where(ok0, 0, I16_MIN).astype(I32)
        above0 = jnp.where(ok0, 0.0, c_nonneg)

        def bit_step(it, carry):
            th, above = carry
            cand = th | (jnp.int32(1) << (14 - it))
            cand16 = cand.astype(I16)
            cnt = count16(ref, lambda blk: blk >= cand16)
            ok = cnt >= k_need
            return jnp.where(ok, cand, th), jnp.where(ok, above, cnt)

        return lax.fori_loop(0, 15, bit_step, (th0, above0))

    k_full = jnp.full((TQ, LANES), float(k_sel), F32)
    th_hi, above_hi = search16(hi_ref, k_full)
    th_hi16 = th_hi.astype(I16)
    k_lo = k_full - above_hi

    def bucket_chunk(c, carry):
        for a in range(nb):
            blk = c * nb + a
            lo_ref[blk] = jnp.where(hi_ref[blk] == th_hi16, lo_ref[blk], jnp.int16(I16_MIN))
        return carry

    lax.fori_loop(0, 2 * n_pairs, bucket_chunk, 0)
    th_lo, above_lo = search16(lo_ref, k_lo)
    need = k_lo - above_lo
    th_lo16 = th_lo.astype(I16)
    one16, zero16 = jnp.int16(1), jnp.int16(0)

    def flags(blk):
        hi, lom = hi_ref[blk], lo_ref[blk]
        in_bucket = hi == th_hi16
        above = jnp.where(hi > th_hi16, one16, jnp.where(lom > th_lo16, one16, zero16))
        tied = jnp.where(in_bucket, jnp.where(lom == th_lo16, one16, zero16), zero16)
        return above.astype(I32).astype(F32), tied.astype(I32).astype(F32)

    def sel_pair(cp, carry):
        pcs, fl = [], []
        for half in range(2):
            fl.append([flags((2 * cp + half) * nb + a) for a in range(nb)])
            tie = jnp.concatenate([t for _, t in fl[half]], axis=1).astype(BF16)
            pcs.append(_dot(tie, tri_ref[...]))
        for half in range(2):
            for a in range(nb):
                blk = (2 * cp + half) * nb + a
                above, tied = fl[half][a]
                rank = pcs[half][:, a * LANES:(a + 1) * LANES] + carry
                tied_in = jnp.where(tied > 0.0, jnp.where(rank <= need, 0.0, NEG_INF), NEG_INF)
                sb = jnp.where(above > 0.0, 0.0, tied_in)
                selb_ref[blk] = jnp.where(blk * LANES + lane_c <= row_t, sb, NEG_INF)
            carry = carry + pcs[half][:, SCORE_COLS:SCORE_COLS + LANES]
        return carry

    lax.fori_loop(0, n_pairs, sel_pair, jnp.zeros((TQ, LANES), F32))

    qb = qb_ref[...]
    scale = (DSA_HEAD_DIM ** -0.5) * LOG2_E
    for h in range(DSA_HEADS):
        qh = qb[:, h * DSA_HEAD_DIM:(h + 1) * DSA_HEAD_DIM].astype(BF16)
        ql_ref[h * TQ:(h + 1) * TQ, :] = (_dot(qh, wuk_ref[h]) * scale).astype(BF16)

    m_ref[...] = jnp.full(m_ref.shape, -1e30, F32)
    acc_ref[...] = jnp.zeros(acc_ref.shape, F32)

    def attend(c, near):
        c0 = pl.multiple_of(c * SCORE_COLS, SCORE_COLS)
        ckx = ckv_ref[pl.ds(c0, SCORE_COLS), :]
        ck = ckx[:, :KV_RANK]
        lg = _dot_t(ql_ref[...], ck)
        for h in range(DSA_HEADS):
            blk_max = None
            for a in range(nb):
                s = lg[h * TQ:(h + 1) * TQ, a * LANES:(a + 1) * LANES] + selb_ref[c * nb + a]
                if near:
                    q_blocks = TQ // LANES
                    s = s + jnp.concatenate(
                        [bias_ref[h, jnp.clip(c * nb + a - (qt * q_blocks + r) + 2, 0, 3)]
                         for r in range(q_blocks)], axis=0)
                s_ref[h, a] = s
                blk_max = s if blk_max is None else jnp.maximum(blk_max, s)
            m_old = m_ref[h]
            m_new = jnp.maximum(m_old, jnp.broadcast_to(jnp.max(blk_max, axis=-1, keepdims=True),
                                                        (TQ, LANES)))
            al_ref[h] = jnp.exp2(m_old - m_new)
            m_ref[h] = m_new
        for h in range(DSA_HEADS):
            m_new = m_ref[h]
            alpha = al_ref[h]
            pb = jnp.concatenate([jnp.exp2(s_ref[h, a] - m_new).astype(BF16) for a in range(nb)], axis=1)
            acc_ref[h] = jnp.concatenate([alpha, alpha], axis=1) * acc_ref[h] + _dot(pb, ckx)

    def far(c, carry):
        attend(c, False)
        return carry

    lax.fori_loop(0, jnp.maximum(c_diag - 1, 0), far, 0)

    @pl.when(c_diag > 0)
    def _():
        attend(c_diag - 1, True)

    attend(c_diag, True)

    out = jnp.zeros((TQ, DSA_WIDTH), F32)
    for h in range(DSA_HEADS):
        o_lat = acc_ref[h, :, :KV_RANK] / acc_ref[h, :, KV_RANK:]
        out = out + _dot(o_lat.astype(BF16), wuv_ref[h])
    o_ref[...] = out


def _rel_bucket_table(n):
    max_exact = REL_BUCKETS // 2
    d = np.arange(n)
    df = np.maximum(d, 1).astype(np.float32)
    large = max_exact + (np.log(df / np.float32(max_exact)) / np.float32(math.log(REL_MAX_DIST / max_exact))
                         * np.float32(REL_BUCKETS - max_exact)).astype(np.int32)
    large = np.minimum(large, REL_BUCKETS - 1)
    return np.where(d < max_exact, d, large)


def _dsa_attention(h_main, h_idx, ckv, kp, w_uk, w_uv, rel_bias, B, L):
    T = B * L
    nq = L // TQ
    k_sel = min(DSA_TOPK, L // 4)
    assert SCORE_COLS == 4 * LANES and SCORE_COLS >= k_sel
    blk = LANES
    dist = blk + np.arange(blk)[:, None] - np.arange(2 * blk)[None, :]
    assert _rel_bucket_table(REL_MAX_DIST * 4)[blk:].min() == REL_BUCKETS - 1
    bucket = _rel_bucket_table(2 * blk + 1)[np.maximum(dist, 0)]
    rb = rel_bias.astype(F32)
    onehot = (jnp.asarray(bucket, I32)[:, :, None] == jnp.arange(REL_BUCKETS, dtype=I32)).astype(F32)
    near = jnp.einsum("qkb,bh->hqk", onehot, rb - rb[REL_BUCKETS - 1], precision=lax.Precision.HIGHEST)
    bias_near = jnp.pad(near * LOG2_E, ((0, 0), (0, 0), (blk, blk)))
    bias_near = bias_near.reshape(DSA_HEADS, blk, 4, blk).transpose(0, 2, 1, 3)
    assert L % (2 * SCORE_COLS) == 0 and TQ % LANES == 0 and SCORE_COLS % TQ == 0
    once = pl.Buffered(1)
    u = np.arange(SCORE_COLS)[:, None]
    v = np.arange(SCORE_COLS + LANES)[None, :]
    tri = jnp.asarray((u <= v) | (v >= SCORE_COLS), BF16)
    wuv_band = jnp.zeros((DSA_HEADS, KV_RANK, DSA_WIDTH), F32)
    for h in range(DSA_HEADS):
        wuv_band = wuv_band.at[h, :, h * DSA_HEAD_DIM:(h + 1) * DSA_HEAD_DIM].set(w_uv[h])
    kern = functools.partial(_dsa_kernel, k_sel=k_sel)
    lpad = -(-L // (2 * SCORE_COLS)) * (2 * SCORE_COLS)
    return pl.pallas_call(
        kern,
        grid=(B, nq),
        in_specs=[
            pl.BlockSpec((TQ, DSA_WIDTH), lambda b, i: (b * nq + i, 1)),
            pl.BlockSpec((TQ, IDX_HEADS * IDX_DIM), lambda b, i: (b * nq + i, 0)),
            pl.BlockSpec((TQ, LANES), lambda b, i: (b * nq + i, 2)),
            pl.BlockSpec((L, 4 * IDX_DIM), lambda b, i: (b, 0), pipeline_mode=once),
            pl.BlockSpec((L, 2 * KV_RANK), lambda b, i: (b, 0), pipeline_mode=once),
            pl.BlockSpec((DSA_HEADS, DSA_HEAD_DIM, KV_RANK), lambda b, i: (0, 0, 0)),
            pl.BlockSpec((DSA_HEADS, KV_RANK, DSA_WIDTH), lambda b, i: (0, 0, 0)),
            pl.BlockSpec((DSA_HEADS, 4, blk, blk), lambda b, i: (0, 0, 0, 0), pipeline_mode=once),
            pl.BlockSpec((SCORE_COLS, SCORE_COLS + LANES), lambda b, i: (0, 0)),
        ],
        out_specs=pl.BlockSpec((TQ, DSA_WIDTH), lambda b, i: (b * nq + i, 0)),
        out_shape=jax.ShapeDtypeStruct((T, DSA_WIDTH), F32),
        scratch_shapes=[
            pltpu.VMEM((lpad // LANES, TQ, LANES), I16),
            pltpu.VMEM((lpad // LANES, TQ, LANES), I16),
            pltpu.VMEM((lpad // LANES, TQ, LANES), F32),
            pltpu.VMEM((DSA_HEADS * TQ, KV_RANK), BF16),
            pltpu.VMEM((DSA_HEADS, SCORE_COLS // LANES, TQ, LANES), F32),
            pltpu.VMEM((DSA_HEADS, TQ, LANES), F32),
            pltpu.VMEM((DSA_HEADS, TQ, LANES), F32),
            pltpu.VMEM((DSA_HEADS, TQ, 2 * KV_RANK), F32),
        ],
        compiler_params=_cparams(("parallel", "arbitrary")),
        name="dsa_attention",
    )(h_main, h_idx, h_idx, kp, ckv, w_uk.astype(BF16), wuv_band.astype(BF16), bias_near, tri)


def _mlstm_prep_kernel(cur_ref, halo_ref, w_ref, b_ref, wqh_ref, wql_ref, wkh_ref, wkl_ref,
                       q_ref, k_ref, xbuf, *, tl):
    i = pl.program_id(1)
    xbuf[SUBLANES:SUBLANES + tl, :] = cur_ref[...]
    xbuf[0:SUBLANES, :] = jnp.where(i > 0, halo_ref[...], 0.0)
    base = SUBLANES - (MLSTM_CONV - 1)
    acc = jnp.broadcast_to(b_ref[...], (tl, MLSTM_WIDTH))
    for k in range(MLSTM_CONV):
        acc = acc + w_ref[k:k + 1, :] * xbuf[base + k:base + k + tl, :]
    xc = acc * jax.nn.sigmoid(acc)
    q_ref[...] = _dot3(xc, wqh_ref[...], wql_ref[...]) * (MLSTM_HEAD_DIM ** -0.5)
    k_ref[...] = _dot3(xc, wkh_ref[...], wkl_ref[...])


def _block_diag(w):
    h, d, e = w.shape
    out = jnp.zeros((h * d, h * e), F32)
    for i in range(h):
        out = out.at[i * d:(i + 1) * d, i * e:(i + 1) * e].set(w[i])
    return out


def _mlstm_prep(h_main, conv_w, conv_b, w_qm, w_km, B, L, tl=512):
    T = B * L
    nl = L // tl
    hb = tl // SUBLANES
    wq = _wsplit(_block_diag(w_qm))
    wk = _wsplit(_block_diag(w_km))
    full = lambda shape: pl.BlockSpec(shape, lambda b, i: (0,) * len(shape))
    return pl.pallas_call(
        functools.partial(_mlstm_prep_kernel, tl=tl),
        grid=(B, nl),
        in_specs=[
            pl.BlockSpec((tl, MLSTM_WIDTH), lambda b, i: (b * nl + i, 4)),
            pl.BlockSpec((SUBLANES, MLSTM_WIDTH),
                         lambda b, i: (jnp.maximum((b * nl + i) * hb - 1, 0), 4)),
            full((MLSTM_CONV, MLSTM_WIDTH)), full((1, MLSTM_WIDTH)),
            full((MLSTM_WIDTH, MLSTM_WIDTH)), full((MLSTM_WIDTH, MLSTM_WIDTH)),
            full((MLSTM_WIDTH, MLSTM_WIDTH)), full((MLSTM_WIDTH, MLSTM_WIDTH)),
        ],
        out_specs=[pl.BlockSpec((tl, MLSTM_WIDTH), lambda b, i: (b * nl + i, 0)),
                   pl.BlockSpec((tl, MLSTM_WIDTH), lambda b, i: (b * nl + i, 0))],
        out_shape=[jax.ShapeDtypeStruct((T, MLSTM_WIDTH), F32),
                   jax.ShapeDtypeStruct((T, MLSTM_WIDTH), F32)],
        scratch_shapes=[pltpu.VMEM((SUBLANES + tl, MLSTM_WIDTH), F32)],
        compiler_params=_cparams(("parallel", "parallel")),
        name="mlstm_prep",
    )(h_main, h_main, conv_w, conv_b.reshape(1, -1), wq[0], wq[1], wk[0], wk[1])


ML_ROWS = 128


def _mlstm_kernel(q_ref, k_ref, v_ref, o_ref, sm_ref, bi_ref, bf_ref, g_ref, out_ref,
                  cm_ref, n_ref, m_ref):
    C = MLSTM_CHUNK
    dh = MLSTM_HEAD_DIM

    @pl.when(pl.program_id(1) == 0)
    def _():
        cm_ref[...] = jnp.zeros(cm_ref.shape, F32)
        n_ref[...] = jnp.zeros(n_ref.shape, F32)
        m_ref[...] = jnp.zeros(m_ref.shape, F32)

    sm_t = sm_ref[...].T
    ig = sm_t[SM_I:SM_I + SUBLANES] + bi_ref[...]
    fg = jax.nn.log_sigmoid(sm_t[SM_F:SM_F + SUBLANES] + bf_ref[...])
    lane = lax.broadcasted_iota(I32, (SUBLANES, ML_ROWS), 1) & (C - 1)
    bcum = fg
    s = 1
    while s < C:
        bcum = bcum + jnp.where(lane >= s, pltpu.roll(bcum, s, axis=1), 0.0)
        s *= 2
    cols = jnp.concatenate([bcum, ig, jnp.zeros((LANES - 2 * SUBLANES, ML_ROWS), F32)], axis=0).T
    tri = (lax.broadcasted_iota(I32, (C, C), 1) <= lax.broadcasted_iota(I32, (C, C), 0))

    q_all, k_all, v_all, o_all = q_ref[...], k_ref[...], v_ref[...], o_ref[...]
    g_all = g_ref[...]
    for c in range(ML_ROWS // C):
        r0 = c * C
        for h in range(MLSTM_HEADS):
            hs = slice(h * dh, (h + 1) * dh)
            qj = q_all[r0:r0 + C, hs]
            kj = k_all[r0:r0 + C, hs]
            vj = v_all[r0:r0 + C, hs]
            b_row = bcum[h:h + 1, r0:r0 + C]
            i_row = ig[h:h + 1, r0:r0 + C]
            b_col = cols[r0:r0 + C, h:h + 1]
            i_col = cols[r0:r0 + C, SUBLANES + h:SUBLANES + h + 1]
            m_prev = m_ref[h:h + 1, 0:1]
            n_prev = n_ref[h:h + 1, :]
            cm_prev = cm_ref[h]

            dm = jnp.where(tri, b_col - b_row + i_row, NEG_INF)
            inter = b_col + m_prev
            m_row = jnp.maximum(inter, jnp.max(dm, axis=-1, keepdims=True))
            w_inter = jnp.exp(inter - m_row)
            qb, kb, vb = qj.astype(BF16), kj.astype(BF16), vj.astype(BF16)
            sw = _dot_t(qb, kb) * jnp.exp(dm - m_row)
            num = _dot(sw.astype(BF16), vb) + w_inter * _dot(qb, cm_prev.astype(BF16))
            den = (jnp.sum(sw, axis=-1, keepdims=True)
                   + w_inter * jnp.sum(qj * n_prev, axis=-1, keepdims=True))
            hh = num / jnp.maximum(jnp.abs(den), jnp.exp(-m_row))

            b_last = b_row[:, C - 1:C]
            g_row = b_last - b_row + i_row
            g_col = b_last - b_col + i_col
            m_new = jnp.maximum(b_last + m_prev, jnp.max(g_row, axis=-1, keepdims=True))
            decay = jnp.exp(b_last + m_prev - m_new)
            kw = kj * jnp.exp(g_col - m_new)
            cm_ref[h] = decay * cm_prev + _dot_tl(kw.astype(BF16), vb)
            n_ref[h:h + 1, :] = decay * n_prev + jnp.sum(kw, axis=0, keepdims=True)
            m_ref[h:h + 1, :] = jnp.broadcast_to(m_new, (1, LANES))

            mu = jnp.mean(hh, axis=-1, keepdims=True)
            hc = hh - mu
            var = jnp.mean(hc * hc, axis=-1, keepdims=True)
            hn = hc * lax.rsqrt(var + LN_EPS) * g_all[:, hs]
            out_ref[r0:r0 + C, hs] = jax.nn.sigmoid(o_all[r0:r0 + C, hs]) * hn


def _mlstm(q, k, h_main, h_idx, b_i, b_f, norm_g, B, L):
    T = B * L
    nl = L // ML_ROWS
    pad8 = lambda v: jnp.pad(v.astype(F32), (0, SUBLANES - MLSTM_HEADS)).reshape(SUBLANES, 1)
    row = lambda col: pl.BlockSpec((ML_ROWS, MLSTM_WIDTH), lambda b, i: (b * nl + i, col))
    return pl.pallas_call(
        _mlstm_kernel,
        grid=(B, nl),
        in_specs=[row(0), row(0), row(5), row(6),
                  pl.BlockSpec((ML_ROWS, LANES), lambda b, i: (b * nl + i, 2)),
                  pl.BlockSpec((SUBLANES, 1), lambda b, i: (0, 0)),
                  pl.BlockSpec((SUBLANES, 1), lambda b, i: (0, 0)),
                  pl.BlockSpec((1, MLSTM_WIDTH), lambda b, i: (0, 0))],
        out_specs=row(0),
        out_shape=jax.ShapeDtypeStruct((T, MLSTM_WIDTH), F32),
        scratch_shapes=[pltpu.VMEM((MLSTM_HEADS, MLSTM_HEAD_DIM, MLSTM_HEAD_DIM), F32),
                        pltpu.VMEM((SUBLANES, MLSTM_HEAD_DIM), F32),
                        pltpu.VMEM((SUBLANES, LANES), F32)],
        compiler_params=_cparams(("parallel", "arbitrary")),
        name="mlstm_scan",
    )(q, k, h_main, h_main, h_idx, pad8(b_i), pad8(b_f), norm_g.reshape(1, -1))


def _mix_out_kernel(ya_ref, yb_ref, yc_ref, x_ref, wa_ref, wb_ref, wc_ref, g_ref, b_ref, o_ref, *,
                    alpha):
    y = (_dot(ya_ref[...].astype(BF16), wa_ref[...])
         + _dot(yb_ref[...].astype(BF16), wb_ref[...])
         + _dot(yc_ref[...].astype(BF16), wc_ref[...]))
    o_ref[...] = _layer_norm(alpha * x_ref[...] + y, g_ref[...], b_ref[...])


def _mix_out(y_a, y_b, y_c, x, w_out, g, b, alpha, tm=512):
    T, D = x.shape
    w = w_out.astype(BF16)
    wa, wb, wc = w[:CONV_CH], w[CONV_CH:CONV_CH + DSA_WIDTH], w[CONV_CH + DSA_WIDTH:]
    rows = lambda width: pl.BlockSpec((tm, width), lambda i: (i, 0))
    full = lambda a: pl.BlockSpec(a.shape, lambda i: (0, 0))
    g2, b2 = g.reshape(1, D), b.reshape(1, D)
    return pl.pallas_call(
        functools.partial(_mix_out_kernel, alpha=alpha),
        grid=(T // tm,),
        in_specs=[rows(CONV_CH), rows(DSA_WIDTH), rows(MLSTM_WIDTH), rows(D),
                  full(wa), full(wb), full(wc), full(g2), full(b2)],
        out_specs=rows(D),
        out_shape=jax.ShapeDtypeStruct((T, D), F32),
        compiler_params=_cparams(("parallel",)),
        name="mix_out",
    )(y_a, y_b, y_c, x, wa, wb, wc, g2, b2)


def _xattn_kernel(x_ref, kv_ref, wq_ref, wo_ref, g_ref, b_ref, o_ref, *, alpha):
    x = x_ref[...]
    q = _dot(x.astype(BF16), wq_ref[...])
    kv = kv_ref[...]
    scale = XATTN_HEAD_DIM ** -0.5
    outs = []
    for h in range(XATTN_HEADS):
        hs = slice(h * XATTN_HEAD_DIM, (h + 1) * XATTN_HEAD_DIM)
        kh = kv[:, hs].astype(BF16)
        vh = kv[:, D_MODEL + h * XATTN_HEAD_DIM:D_MODEL + (h + 1) * XATTN_HEAD_DIM].astype(BF16)
        lg = _dot_t(q[:, hs].astype(BF16), kh) * scale
        lg = lg - jnp.max(lg, axis=-1, keepdims=True)
        p = jnp.exp(lg)
        p = p / jnp.sum(p, axis=-1, keepdims=True)
        outs.append(_dot(p.astype(BF16), vh))
    o = jnp.concatenate(outs, axis=-1)
    y = _dot(o.astype(BF16), wo_ref[...])
    o_ref[...] = _layer_norm(alpha * x + y, g_ref[...], b_ref[...])


def _xattn(x, kv, w_q, w_o, g, b, alpha, B, L, tm=512):
    T, D = x.shape
    nl = L // tm
    M = kv.shape[0] // B
    full = lambda a: pl.BlockSpec(a.shape, lambda bb, i: (0, 0))
    g2, b2 = g.reshape(1, D), b.reshape(1, D)
    wq, wo = w_q.astype(BF16), w_o.astype(BF16)
    return pl.pallas_call(
        functools.partial(_xattn_kernel, alpha=alpha),
        grid=(B, nl),
        in_specs=[pl.BlockSpec((tm, D), lambda bb, i: (bb * nl + i, 0)),
                  pl.BlockSpec((M, 2 * D), lambda bb, i: (bb, 0)),
                  full(wq), full(wo), full(g2), full(b2)],
        out_specs=pl.BlockSpec((tm, D), lambda bb, i: (bb * nl + i, 0)),
        out_shape=jax.ShapeDtypeStruct((T, D), F32),
        compiler_params=_cparams(("parallel", "parallel")),
        name="xattn",
    )(x, kv, wq, wo, g2, b2)


def _peer_score_kernel(x_ref, w_ref, k1h_ref, k1l_ref, k2h_ref, k2l_ref, st_ref):
    q = _dot(x_ref[...].astype(BF16), w_ref[...])
    half = PEER_QDIM // 2
    for h in range(PEER_HEADS):
        for part, (kh_ref, kl_ref) in enumerate(((k1h_ref, k1l_ref), (k2h_ref, k2l_ref))):
            c0 = h * PEER_QDIM + part * half
            q_hi, q_lo = _split(q[:, c0:c0 + half])
            kh, kl = kh_ref[...], kl_ref[...]
            st_ref[2 * h + part] = _dot_t(kh, q_hi) + _dot_t(kl, q_hi) + _dot_t(kh, q_lo)


def _peer_scores(x, w_pq, sub_k1, sub_k2, tm=512):
    T, D = x.shape
    w = w_pq.astype(BF16)
    k1h, k1l = _wsplit(sub_k1)
    k2h, k2l = _wsplit(sub_k2)
    full = lambda a: pl.BlockSpec(a.shape, lambda i: (0, 0))
    return pl.pallas_call(
        _peer_score_kernel,
        grid=(T // tm,),
        in_specs=[pl.BlockSpec((tm, D), lambda i: (i, 0)), full(w),
                  full(k1h), full(k1l), full(k2h), full(k2l)],
        out_specs=pl.BlockSpec((2 * PEER_HEADS, PEER_KEYS, tm), lambda i: (0, 0, i)),
        out_shape=jax.ShapeDtypeStruct((2 * PEER_HEADS, PEER_KEYS, T), F32),
        compiler_params=_cparams(("parallel",)),
        name="peer_scores",
    )(x, w, k1h, k1l, k2h, k2l)


PEER_NTOP = PEER_TOPK + 1
PEER_PAIR_ROWS = tuple(PEER_NTOP // (k + 1) for k in range(PEER_NTOP))
PEER_CAND_ROWS = -(-sum(PEER_PAIR_ROWS) // SUBLANES) * SUBLANES


def _peer_thr_kernel(st_ref, stats_ref, v2_ref, cand_ref):
    def top_rows(x):
        rows = []
        for _ in range(PEER_NTOP):
            m = jnp.max(x, axis=0, keepdims=True)
            rows.append(m)
            x = jnp.where(x == m, NEG_INF, x)
        return rows

    v1 = top_rows(st_ref[0])
    v2 = top_rows(st_ref[1])
    for k in range(PEER_NTOP):
        v2_ref[k:k + 1, :] = v2[k]
    r = 0
    for k, n in enumerate(PEER_PAIR_ROWS):
        cand_ref[r:r + n, :] = v1[k] + v2_ref[0:n, :]
        r += n
    cand_ref[r:PEER_CAND_ROWS, :] = jnp.full((PEER_CAND_ROWS - r, cand_ref.shape[1]), NEG_INF, F32)
    cand = cand_ref[...]
    x = cand
    for _ in range(PEER_TOPK - 1):
        m = jnp.max(x, axis=0, keepdims=True)
        x = jnp.where(x == m, NEG_INF, x)
    thr = jnp.max(x, axis=0, keepdims=True)
    nxt = jnp.max(jnp.where(x == thr, NEG_INF, x), axis=0, keepdims=True)
    top = v1[0] + v2[0]
    z = jnp.sum(jnp.where(cand >= thr, jnp.exp(cand - top), 0.0), axis=0, keepdims=True)
    cut = jnp.where(nxt > NEG_INF, 0.5 * thr + 0.5 * nxt, thr)
    pad = jnp.zeros((SUBLANES - 4, thr.shape[1]), F32)
    stats_ref[0] = jnp.concatenate([cut, v1[0], v2[0], 1.0 / z, pad], axis=0)


def _peer_thresholds(st, tm=1024):
    T = st.shape[2]
    return pl.pallas_call(
        _peer_thr_kernel,
        grid=(PEER_HEADS, T // tm),
        in_specs=[pl.BlockSpec((2, PEER_KEYS, tm), lambda h, i: (h, 0, i))],
        out_specs=pl.BlockSpec((1, SUBLANES, tm), lambda h, i: (h, 0, i)),
        out_shape=jax.ShapeDtypeStruct((PEER_HEADS, SUBLANES, T), F32),
        scratch_shapes=[pltpu.VMEM((3 * SUBLANES, tm), F32),
                        pltpu.VMEM((PEER_CAND_ROWS, tm), F32)],
        compiler_params=_cparams(("parallel", "parallel")),
        name="peer_thresholds",
    )(st)


PEER_TL = 512
PEER_ET = 512
PEER_SUB = 64
GELU_FOLD = 2.0 ** -0.5

def _peer_mix_kernel(x_ref, st_ref, stats_ref, u_ref, vt_ref, g_ref, b_ref, o_ref,
                     xb_ref, d1_ref, e1_ref, e2_ref, gate_ref, w_ref, acc_ref, *, alpha):
    j = pl.program_id(1)
    n_tiles = pl.num_programs(1) - 1
    sub_tiles = PEER_ET // PEER_KEYS
    slot = j % 2

    @pl.when(j == 0)
    def _():
        xb_ref[...] = x_ref[...].astype(BF16)
        for h in range(PEER_HEADS):
            st = stats_ref[h]
            s1 = st_ref[2 * h]
            d1_ref[h] = st[0:1] - s1
            e1_ref[h] = jnp.exp(s1 - st[1:2]) * (st[3:4] * GELU_FOLD)
            e2_ref[h] = jnp.exp(st_ref[2 * h + 1] - st[2:3])
        acc_ref[...] = jnp.zeros(acc_ref.shape, F32)
        w_ref[...] = jnp.zeros(w_ref.shape, BF16)

    def gate_block(a, lb):
        i1 = jnp.minimum(j, n_tiles - 1) * sub_tiles + a
        ls = slice(lb * LANES, (lb + 1) * LANES)
        d_rows = [d1_ref[h, pl.ds(i1, 1), :][:, ls] for h in range(PEER_HEADS)]
        c_rows = [e1_ref[h, pl.ds(i1, 1), :][:, ls] for h in range(PEER_HEADS)]
        for k0 in range(0, PEER_KEYS, PEER_SUB):
            ks = slice(k0, k0 + PEER_SUB)
            gate = None
            for h in range(PEER_HEADS):
                picked = st_ref[2 * h + 1, ks, ls] >= d_rows[h]
                term = jnp.where(picked, e2_ref[h, ks, ls] * c_rows[h], 0.0)
                gate = term if gate is None else gate + term
            gate_ref[a * PEER_KEYS + k0:a * PEER_KEYS + k0 + PEER_SUB, ls] = gate

    for a in range(sub_tiles):
        for lb in range(PEER_TL // LANES):
            gate_block(a, lb)
    acc_ref[...] += _dot(vt_ref[...], w_ref[1 - slot])
    su = _dot_t(u_ref[...], xb_ref[...])
    act = su + su * lax.erf(su)
    w_ref[slot] = (gate_ref[...] * act).astype(BF16)

    @pl.when(j == n_tiles)
    def _():
        y = acc_ref[...].T
        o_ref[...] = _layer_norm(alpha * x_ref[...] + y, g_ref[...], b_ref[...])


def _peer_mix(x, st, stats, peer_u, peer_v, g, b, alpha):
    T, D = x.shape
    E = peer_u.shape[0]
    u = (peer_u * GELU_FOLD).astype(BF16)
    vt = peer_v.astype(BF16).T
    g2, b2 = g.reshape(1, D), b.reshape(1, D)
    tl, et = PEER_TL, PEER_ET
    n_e = E // et
    return pl.pallas_call(
        functools.partial(_peer_mix_kernel, alpha=alpha),
        grid=(T // tl, n_e + 1),
        in_specs=[pl.BlockSpec((tl, D), lambda i, j: (i, 0)),
                  pl.BlockSpec((2 * PEER_HEADS, PEER_KEYS, tl), lambda i, j: (0, 0, i)),
                  pl.BlockSpec((PEER_HEADS, SUBLANES, tl), lambda i, j: (0, 0, i)),
                  pl.BlockSpec((et, D), lambda i, j: (jnp.minimum(j, n_e - 1), 0)),
                  pl.BlockSpec((D, et), lambda i, j: (0, jnp.maximum(j - 1, 0))),
                  pl.BlockSpec((1, D), lambda i, j: (0, 0)),
                  pl.BlockSpec((1, D), lambda i, j: (0, 0))],
        out_specs=pl.BlockSpec((tl, D), lambda i, j: (i, 0)),
        out_shape=jax.ShapeDtypeStruct((T, D), F32),
        scratch_shapes=[pltpu.VMEM((tl, D), BF16),
                        pltpu.VMEM((PEER_HEADS, PEER_KEYS, tl), F32),
                        pltpu.VMEM((PEER_HEADS, PEER_KEYS, tl), F32),
                        pltpu.VMEM((PEER_HEADS, PEER_KEYS, tl), F32),
                        pltpu.VMEM((et, tl), F32),
                        pltpu.VMEM((2, et, tl), BF16),
                        pltpu.VMEM((D, tl), F32)],
        compiler_params=_cparams(("parallel", "arbitrary")),
        name="peer_mix",
    )(x, st, stats, u, vt, g2, b2)


def _pack_w_in(w):
    sizes = (512, 512, 128, 256, 64, 4, 256, 256, 256, 4, 4)
    offs = np.concatenate([[0], np.cumsum(sizes)])
    seg = lambda n: w[:, int(offs[n]):int(offs[n + 1])]
    a_in, q_b, c_b, qi, ki, wi, xc, v_m, o_m, i_m, f_m = (seg(n) for n in range(len(sizes)))
    z = lambda n: jnp.zeros((w.shape[0], n), w.dtype)
    main = jnp.concatenate([a_in, q_b, xc, v_m, o_m, c_b], axis=1)
    idx = jnp.concatenate([qi, ki, wi, z(4), i_m, z(4), f_m, z(IDX_W - 256 - SM_F - 4)], axis=1)
    return main, idx


def kernel(x, mem, ln_in_g, ln_in_b, rel_bias, w_in, conv_a_w, conv_a_b, norm_a_g, norm_a_b,
           kv_norm_g, w_uk, w_uv, conv_m_w, conv_m_b, w_qm, w_km, b_i, b_f, norm_m_g, w_out,
           ln1_g, ln1_b, w_cq, w_ckv, w_co, ln2_g, ln2_b, w_pq, sub_k1, sub_k2, peer_u, peer_v,
           ln3_g, ln3_b):
    B, L, D = x.shape
    T = B * L
    depth = w_in.shape[0]
    alpha = (2.0 * depth) ** 0.25
    xs = _entry_ln(x.reshape(T, D), ln_in_g, ln_in_b)
    mem2 = mem.reshape(-1, D)
    for l in range(depth):
        w_main, w_idx = _pack_w_in(w_in[l])
        h_main = _matmul(xs, w_main, 1, 512, MAIN_W, "proj_main")
        h_idx = _matmul(xs, w_idx, 3, 512, IDX_W, "proj_idx")
        y_a = _conv_group(h_main, conv_a_w[l], conv_a_b[l], norm_a_g[l], norm_a_b[l], B, L)
        ckv, kp = _dsa_prep(h_main, h_idx, kv_norm_g[l])
        y_b = _dsa_attention(h_main, h_idx, ckv, kp, w_uk[l], w_uv[l], rel_bias, B, L)
        q_m, k_m = _mlstm_prep(h_main, conv_m_w[l], conv_m_b[l], w_qm[l], w_km[l], B, L)
        y_c = _mlstm(q_m, k_m, h_main, h_idx, b_i[l], b_f[l], norm_m_g[l], B, L)
        xs = _mix_out(y_a, y_b, y_c, xs, w_out[l], ln1_g[l], ln1_b[l], alpha)

        kv = _matmul(mem2, w_ckv[l], 1, mem2.shape[0], 512, "xattn_kv")
        xs = _xattn(xs, kv, w_cq[l], w_co[l], ln2_g[l], ln2_b[l], alpha, B, L)

        st = _peer_scores(xs, w_pq[l], sub_k1[l], sub_k2[l])
        stats = _peer_thresholds(st)
        xs = _peer_mix(xs, st, stats, peer_u[l], peer_v[l], ln3_g[l], ln3_b[l], alpha)
    return xs.reshape(B, L, D)
```

```python
import functools
import math

import numpy as np
import jax
import jax.numpy as jnp
from jax import lax
from jax.experimental import pallas as pl
from jax.experimental.pallas import tpu as pltpu

F32 = jnp.float32
BF16 = jnp.bfloat16
I32 = jnp.int32

D_MODEL = 1024
CONV_CH = 256
CONV_WIDTH = 31
DSA_HEADS = 8
DSA_HEAD_DIM = 64
DSA_WIDTH = 512
KV_RANK = 128
IDX_HEADS = 4
IDX_DIM = 64
DSA_TOPK = 256
MLSTM_HEADS = 4
MLSTM_HEAD_DIM = 64
MLSTM_WIDTH = 256
MLSTM_CONV = 4
MLSTM_CHUNK = 64
REL_BUCKETS = 32
REL_MAX_DIST = 128
XATTN_HEADS = 4
XATTN_HEAD_DIM = 256
PEER_HEADS = 8
PEER_KEYS = 128
PEER_QDIM = 256
PEER_TOPK = 16
LN_EPS = 1e-5

LANES = 128
SUBLANES = 8
VMEM_LIMIT = 56 * 1024 * 1024

NEG_INF = float("-inf")
LOG2_E = 1.4426950408889634

MAIN_W = 1920
IDX_W = 384
SM_WI = 64
SM_I = 72
SM_F = 80


def _cparams(sem):
    return pltpu.CompilerParams(dimension_semantics=sem, vmem_limit_bytes=VMEM_LIMIT)


def _dot(a, b):
    return jnp.dot(a, b, preferred_element_type=F32)


def _dot_t(a, b):
    return lax.dot_general(a, b, (((1,), (1,)), ((), ())), preferred_element_type=F32)


def _dot_tl(a, b):
    return lax.dot_general(a, b, (((0,), (0,)), ((), ())), preferred_element_type=F32)


def _split(a):
    hi = a.astype(BF16)
    lo = (a - hi.astype(F32)).astype(BF16)
    return hi, lo


def _dot3(a, b_hi, b_lo, dot=_dot):
    a_hi, a_lo = _split(a)
    return dot(a_hi, b_hi) + dot(a_lo, b_hi) + dot(a_hi, b_lo)


def _layer_norm(x, g, b):
    mu = jnp.mean(x, axis=-1, keepdims=True)
    xc = x - mu
    var = jnp.mean(xc * xc, axis=-1, keepdims=True)
    return xc * lax.rsqrt(var + LN_EPS) * g + b


def _wsplit(w):
    hi = w.astype(BF16)
    lo = (w - hi.astype(F32)).astype(BF16)
    return hi, lo


def _ln_kernel(x_ref, g_ref, b_ref, o_ref):
    o_ref[...] = _layer_norm(x_ref[...], g_ref[...], b_ref[...])


def _entry_ln(x, g, b, tm=512):
    T, D = x.shape
    return pl.pallas_call(
        _ln_kernel,
        grid=(T // tm,),
        in_specs=[pl.BlockSpec((tm, D), lambda i: (i, 0)),
                  pl.BlockSpec((1, D), lambda i: (0, 0)),
                  pl.BlockSpec((1, D), lambda i: (0, 0))],
        out_specs=pl.BlockSpec((tm, D), lambda i: (i, 0)),
        out_shape=jax.ShapeDtypeStruct((T, D), F32),
        compiler_params=_cparams(("parallel",)),
        name="entry_ln",
    )(x, g.reshape(1, D), b.reshape(1, D))


def _mm1_kernel(x_ref, w_ref, o_ref):
    o_ref[...] = _dot(x_ref[...].astype(BF16), w_ref[...])


def _mm3_kernel(x_ref, wh_ref, wl_ref, o_ref):
    o_ref[...] = _dot3(x_ref[...], wh_ref[...], wl_ref[...])


def _matmul(x, w, passes, tm, tn, name):
    T, K = x.shape
    N = w.shape[1]
    x_spec = pl.BlockSpec((tm, K), lambda j, i: (i, 0))
    w_spec = pl.BlockSpec((K, tn), lambda j, i: (0, j))
    if passes == 1:
        kern, ws, w_specs = _mm1_kernel, (w.astype(BF16),), [w_spec]
    else:
        kern, ws, w_specs = _mm3_kernel, _wsplit(w), [w_spec, w_spec]
    return pl.pallas_call(
        kern,
        grid=(N // tn, T // tm),
        in_specs=[x_spec] + w_specs,
        out_specs=pl.BlockSpec((tm, tn), lambda j, i: (i, j)),
        out_shape=jax.ShapeDtypeStruct((T, N), F32),
        compiler_params=_cparams(("parallel", "parallel")),
        name=name,
    )(x, *ws)


CONV_HALO = 32
CONV_ROWS = 64


def _conv_kernel(cur_ref, halo_ref, w_ref, b_ref, g_ref, bb_ref, o_ref, ubuf, *, tl):
    i = pl.program_id(1)
    cur = cur_ref[...]
    ubuf[CONV_HALO:CONV_HALO + tl, :] = cur[:, :CONV_CH] * jax.nn.sigmoid(cur[:, CONV_CH:])
    hal = halo_ref[...]
    uh = hal[:, :CONV_CH] * jax.nn.sigmoid(hal[:, CONV_CH:])
    ubuf[0:CONV_HALO, :] = jnp.where(i > 0, uh, 0.0)
    base = CONV_HALO - (CONV_WIDTH - 1)
    for c in range(tl // CONV_ROWS):
        r0 = c * CONV_ROWS
        acc = jnp.broadcast_to(b_ref[...], (CONV_ROWS, CONV_CH))
        for k in range(CONV_WIDTH):
            acc = acc + w_ref[k:k + 1, :] * ubuf[r0 + base + k:r0 + base + k + CONV_ROWS, :]
        y = _layer_norm(acc, g_ref[...], bb_ref[...])
        o_ref[r0:r0 + CONV_ROWS, :] = y * jax.nn.sigmoid(y)


def _conv_group(h_main, conv_w, conv_b, ln_g, ln_b, B, L, tl=512):
    T = B * L
    nl = L // tl
    hb = tl // CONV_HALO
    return pl.pallas_call(
        functools.partial(_conv_kernel, tl=tl),
        grid=(B, nl),
        in_specs=[
            pl.BlockSpec((tl, 2 * CONV_CH), lambda b, i: (b * nl + i, 0)),
            pl.BlockSpec((CONV_HALO, 2 * CONV_CH),
                         lambda b, i: (jnp.maximum((b * nl + i) * hb - 1, 0), 0)),
            pl.BlockSpec((CONV_WIDTH, CONV_CH), lambda b, i: (0, 0)),
            pl.BlockSpec((1, CONV_CH), lambda b, i: (0, 0)),
            pl.BlockSpec((1, CONV_CH), lambda b, i: (0, 0)),
            pl.BlockSpec((1, CONV_CH), lambda b, i: (0, 0)),
        ],
        out_specs=pl.BlockSpec((tl, CONV_CH), lambda b, i: (b * nl + i, 0)),
        out_shape=jax.ShapeDtypeStruct((T, CONV_CH), F32),
        scratch_shapes=[pltpu.VMEM((CONV_HALO + tl, CONV_CH), F32)],
        compiler_params=_cparams(("parallel", "parallel")),
        name="conv_group",
    )(h_main, h_main, conv_w, conv_b.reshape(1, -1), ln_g.reshape(1, -1), ln_b.reshape(1, -1))


def _dsa_prep_kernel(c_ref, sm_ref, g_ref, ckv_ref, kp_ref):
    c = c_ref[...]
    ms = jnp.mean(c * c, axis=-1, keepdims=True)
    ckv = (c * lax.rsqrt(ms + LN_EPS) * g_ref[...]).astype(BF16)
    ckv_ref[...] = jnp.concatenate([ckv, jnp.ones_like(ckv)], axis=-1)
    k_hi, k_lo = _split(sm_ref[...][:, :IDX_DIM])
    kp_ref[...] = jnp.concatenate([k_hi, k_hi, k_lo, jnp.zeros_like(k_hi)], axis=-1)


def _dsa_prep(h_main, h_idx, kv_g, tm=512):
    T = h_main.shape[0]
    return pl.pallas_call(
        _dsa_prep_kernel,
        grid=(T // tm,),
        in_specs=[pl.BlockSpec((tm, KV_RANK), lambda i: (i, 14)),
                  pl.BlockSpec((tm, LANES), lambda i: (i, 2)),
                  pl.BlockSpec((1, KV_RANK), lambda i: (0, 0))],
        out_specs=[pl.BlockSpec((tm, 2 * KV_RANK), lambda i: (i, 0)),
                   pl.BlockSpec((tm, 4 * IDX_DIM), lambda i: (i, 0))],
        out_shape=[jax.ShapeDtypeStruct((T, 2 * KV_RANK), BF16),
                   jax.ShapeDtypeStruct((T, 4 * IDX_DIM), BF16)],
        compiler_params=_cparams(("parallel",)),
        name="dsa_prep",
    )(h_main, h_idx, kv_g.reshape(1, -1))


TQ = 128
SCORE_COLS = 512


I16 = jnp.int16
I16_MIN = -(2 ** 15)


def _dsa_kernel(qb_ref, qi_ref, sm_ref, kp_ref, ckv_ref, wuk_ref, wuv_ref, bias_ref, tri_ref, o_ref,
                hi_ref, lo_ref, selb_ref, ql_ref, s_ref, m_ref, al_ref, acc_ref, *, k_sel):
    qt = pl.program_id(1)
    nb = SCORE_COLS // LANES
    c_diag = ((qt + 1) * TQ - 1) // SCORE_COLS
    n_chunks = c_diag + 1
    n_pairs = (n_chunks + 1) // 2
    row_t = qt * TQ + lax.broadcasted_iota(I32, (TQ, LANES), 0)
    lane_c = lax.broadcasted_iota(I32, (TQ, LANES), 1)

    qi = qi_ref[...]
    sm = sm_ref[...]
    qp = []
    for h in range(IDX_HEADS):
        q_hi, q_lo = _split(qi[:, h * IDX_DIM:(h + 1) * IDX_DIM])
        qp.append(jnp.concatenate([q_hi, q_lo, q_hi, jnp.zeros_like(q_hi)], axis=-1))
    qp = jnp.concatenate(qp, axis=0)
    w_fold = (IDX_DIM ** -0.5) * (IDX_HEADS ** -0.5)
    ws = [jnp.broadcast_to(sm[:, SM_WI + h:SM_WI + h + 1] * w_fold, (TQ, LANES))
          for h in range(IDX_HEADS)]

    def score_pair(cp, carry):
        for half in range(2):
            c0 = pl.multiple_of((2 * cp + half) * SCORE_COLS, SCORE_COLS)
            d = _dot_t(qp, kp_ref[pl.ds(c0, SCORE_COLS), :])
            for a in range(nb):
                off = c0 + a * LANES
                s = jnp.zeros((TQ, LANES), F32)
                for h in range(IDX_HEADS):
                    s = s + jnp.maximum(d[h * TQ:(h + 1) * TQ, a * LANES:(a + 1) * LANES], 0.0) * ws[h]
                s = jnp.where(off + lane_c <= row_t, s + 0.0, NEG_INF)
                bits = lax.bitcast_convert_type(s, I32)
                key = bits ^ ((bits >> 31) & 0x7FFFFFFF)
                blk = (2 * cp + half) * nb + a
                hi_ref[blk] = (key >> 16).astype(I16)
                lo_ref[blk] = ((key & 0xFFFF) + I16_MIN).astype(I16)
        return carry

    lax.fori_loop(0, n_pairs, score_pair, 0)

    def count16(ref, pred):
        def body(c, acc):
            for a in range(2 * nb):
                acc = acc + jnp.where(pred(ref[c * 2 * nb + a]), jnp.int16(1), jnp.int16(0))
            return acc
        acc = lax.fori_loop(0, n_pairs, body, jnp.zeros((TQ, LANES), I16))
        tot = jnp.sum(acc.astype(I32).astype(F32), axis=-1, keepdims=True)
        return jnp.broadcast_to(tot, (TQ, LANES))

    def search16(ref, k_need):
        c_nonneg = count16(ref, lambda blk: blk >= jnp.int16(0))
        ok0 = c_nonneg >= k_need
        th0 = jnp.where(ok0, 0, I16_MIN).astype(I32)
        above0 = jnp.where(ok0, 0.0, c_nonneg)

        def bit_step(it, carry):
            th, above = carry
            cand = th | (jnp.int32(1) << (14 - it))
            cand16 = cand.astype(I16)
            cnt = count16(ref, lambda blk: blk >= cand16)
            ok = cnt >= k_need
            return jnp.where(ok, cand, th), jnp.where(ok, above, cnt)

        return lax.fori_loop(0, 15, bit_step, (th0, above0))

    k_full = jnp.full((TQ, LANES), float(k_sel), F32)
    th_hi, above_hi = search16(hi_ref, k_full)
    th_hi16 = th_hi.astype(I16)
    k_lo = k_full - above_hi

    def bucket_chunk(c, carry):
        for a in range(nb):
            blk = c * nb + a
            lo_ref[blk] = jnp.where(hi_ref[blk] == th_hi16, lo_ref[blk], jnp.int16(I16_MIN))
        return carry

    lax.fori_loop(0, 2 * n_pairs, bucket_chunk, 0)
    th_lo, above_lo = search16(lo_ref, k_lo)
    need = k_lo - above_lo
    th_lo16 = th_lo.astype(I16)
    one16, zero16 = jnp.int16(1), jnp.int16(0)

    def flags(blk):
        hi, lom = hi_ref[blk], lo_ref[blk]
        in_bucket = hi == th_hi16
        above = jnp.where(hi > th_hi16, one16, jnp.where(lom > th_lo16, one16, zero16))
        tied = jnp.where(in_bucket, jnp.where(lom == th_lo16, one16, zero16), zero16)
        return above.astype(I32).astype(F32), tied.astype(I32).astype(F32)

    def sel_pair(cp, carry):
        pcs, fl = [], []
        for half in range(2):
            fl.append([flags((2 * cp + half) * nb + a) for a in range(nb)])
            tie = jnp.concatenate([t for _, t in fl[half]], axis=1).astype(BF16)
            pcs.append(_dot(tie, tri_ref[...]))
        for half in range(2):
            for a in range(nb):
                blk = (2 * cp + half) * nb + a
                above, tied = fl[half][a]
                rank = pcs[half][:, a * LANES:(a + 1) * LANES] + carry
                tied_in = jnp.where(tied > 0.0, jnp.where(rank <= need, 0.0, NEG_INF), NEG_INF)
                sb = jnp.where(above > 0.0, 0.0, tied_in)
                selb_ref[blk] = jnp.where(blk * LANES + lane_c <= row_t, sb, NEG_INF)
            carry = carry + pcs[half][:, SCORE_COLS:SCORE_COLS + LANES]
        return carry

    lax.fori_loop(0, n_pairs, sel_pair, jnp.zeros((TQ, LANES), F32))

    qb = qb_ref[...]
    scale = (DSA_HEAD_DIM ** -0.5) * LOG2_E
    for h in range(DSA_HEADS):
        qh = qb[:, h * DSA_HEAD_DIM:(h + 1) * DSA_HEAD_DIM].astype(BF16)
        ql_ref[h * TQ:(h + 1) * TQ, :] = (_dot(qh, wuk_ref[h]) * scale).astype(BF16)

    m_ref[...] = jnp.full(m_ref.shape, -1e30, F32)
    acc_ref[...] = jnp.zeros(acc_ref.shape, F32)

    def attend(c, near):
        c0 = pl.multiple_of(c * SCORE_COLS, SCORE_COLS)
        ckx = ckv_ref[pl.ds(c0, SCORE_COLS), :]
        ck = ckx[:, :KV_RANK]
        lg = _dot_t(ql_ref[...], ck)
        for h in range(DSA_HEADS):
            blk_max = None
            for a in range(nb):
                s = lg[h * TQ:(h + 1) * TQ, a * LANES:(a + 1) * LANES] + selb_ref[c * nb + a]
                if near:
                    q_blocks = TQ // LANES
                    s = s + jnp.concatenate(
                        [bias_ref[h, jnp.clip(c * nb + a - (qt * q_blocks + r) + 2, 0, 3)]
                         for r in range(q_blocks)], axis=0)
                s_ref[h, a] = s
                blk_max = s if blk_max is None else jnp.maximum(blk_max, s)
            m_old = m_ref[h]
            m_new = jnp.maximum(m_old, jnp.broadcast_to(jnp.max(blk_max, axis=-1, keepdims=True),
                                                        (TQ, LANES)))
            al_ref[h] = jnp.exp2(m_old - m_new)
            m_ref[h] = m_new
        for h in range(DSA_HEADS):
            m_new = m_ref[h]
            alpha = al_ref[h]
            pb = jnp.concatenate([jnp.exp2(s_ref[h, a] - m_new).astype(BF16) for a in range(nb)], axis=1)
            acc_ref[h] = jnp.concatenate([alpha, alpha], axis=1) * acc_ref[h] + _dot(pb, ckx)

    def far(c, carry):
        attend(c, False)
        return carry

    lax.fori_loop(0, jnp.maximum(c_diag - 1, 0), far, 0)

    @pl.when(c_diag > 0)
    def _():
        attend(c_diag - 1, True)

    attend(c_diag, True)

    out = jnp.zeros((TQ, DSA_WIDTH), F32)
    for h in range(DSA_HEADS):
        o_lat = acc_ref[h, :, :KV_RANK] / acc_ref[h, :, KV_RANK:]
        out = out + _dot(o_lat.astype(BF16), wuv_ref[h])
    o_ref[...] = out


def _rel_bucket_table(n):
    max_exact = REL_BUCKETS // 2
    d = np.arange(n)
    df = np.maximum(d, 1).astype(np.float32)
    large = max_exact + (np.log(df / np.float32(max_exact)) / np.float32(math.log(REL_MAX_DIST / max_exact))
                         * np.float32(REL_BUCKETS - max_exact)).astype(np.int32)
    large = np.minimum(large, REL_BUCKETS - 1)
    return np.where(d < max_exact, d, large)


def _dsa_attention(h_main, h_idx, ckv, kp, w_uk, w_uv, rel_bias, B, L):
    T = B * L
    nq = L // TQ
    k_sel = min(DSA_TOPK, L // 4)
    assert SCORE_COLS == 4 * LANES and SCORE_COLS >= k_sel
    blk = LANES
    dist = blk + np.arange(blk)[:, None] - np.arange(2 * blk)[None, :]
    assert _rel_bucket_table(REL_MAX_DIST * 4)[blk:].min() == REL_BUCKETS - 1
    bucket = _rel_bucket_table(2 * blk + 1)[np.maximum(dist, 0)]
    rb = rel_bias.astype(F32)
    onehot = (jnp.asarray(bucket, I32)[:, :, None] == jnp.arange(REL_BUCKETS, dtype=I32)).astype(F32)
    near = jnp.einsum("qkb,bh->hqk", onehot, rb - rb[REL_BUCKETS - 1], precision=lax.Precision.HIGHEST)
    bias_near = jnp.pad(near * LOG2_E, ((0, 0), (0, 0), (blk, blk)))
    bias_near = bias_near.reshape(DSA_HEADS, blk, 4, blk).transpose(0, 2, 1, 3)
    assert L % (2 * SCORE_COLS) == 0 and TQ % LANES == 0 and SCORE_COLS % TQ == 0
    once = pl.Buffered(1)
    u = np.arange(SCORE_COLS)[:, None]
    v = np.arange(SCORE_COLS + LANES)[None, :]
    tri = jnp.asarray((u <= v) | (v >= SCORE_COLS), BF16)
    wuv_band = jnp.zeros((DSA_HEADS, KV_RANK, DSA_WIDTH), F32)
    for h in range(DSA_HEADS):
        wuv_band = wuv_band.at[h, :, h * DSA_HEAD_DIM:(h + 1) * DSA_HEAD_DIM].set(w_uv[h])
    kern = functools.partial(_dsa_kernel, k_sel=k_sel)
    lpad = -(-L // (2 * SCORE_COLS)) * (2 * SCORE_COLS)
    return pl.pallas_call(
        kern,
        grid=(B, nq),
        in_specs=[
            pl.BlockSpec((TQ, DSA_WIDTH), lambda b, i: (b * nq + i, 1)),
            pl.BlockSpec((TQ, IDX_HEADS * IDX_DIM), lambda b, i: (b * nq + i, 0)),
            pl.BlockSpec((TQ, LANES), lambda b, i: (b * nq + i, 2)),
            pl.BlockSpec((L, 4 * IDX_DIM), lambda b, i: (b, 0), pipeline_mode=once),
            pl.BlockSpec((L, 2 * KV_RANK), lambda b, i: (b, 0), pipeline_mode=once),
            pl.BlockSpec((DSA_HEADS, DSA_HEAD_DIM, KV_RANK), lambda b, i: (0, 0, 0)),
            pl.BlockSpec((DSA_HEADS, KV_RANK, DSA_WIDTH), lambda b, i: (0, 0, 0)),
            pl.BlockSpec((DSA_HEADS, 4, blk, blk), lambda b, i: (0, 0, 0, 0), pipeline_mode=once),
            pl.BlockSpec((SCORE_COLS, SCORE_COLS + LANES), lambda b, i: (0, 0)),
        ],
        out_specs=pl.BlockSpec((TQ, DSA_WIDTH), lambda b, i: (b * nq + i, 0)),
        out_shape=jax.ShapeDtypeStruct((T, DSA_WIDTH), F32),
        scratch_shapes=[
            pltpu.VMEM((lpad // LANES, TQ, LANES), I16),
            pltpu.VMEM((lpad // LANES, TQ, LANES), I16),
            pltpu.VMEM((lpad // LANES, TQ, LANES), F32),
            pltpu.VMEM((DSA_HEADS * TQ, KV_RANK), BF16),
            pltpu.VMEM((DSA_HEADS, SCORE_COLS // LANES, TQ, LANES), F32),
            pltpu.VMEM((DSA_HEADS, TQ, LANES), F32),
            pltpu.VMEM((DSA_HEADS, TQ, LANES), F32),
            pltpu.VMEM((DSA_HEADS, TQ, 2 * KV_RANK), F32),
        ],
        compiler_params=_cparams(("parallel", "arbitrary")),
        name="dsa_attention",
    )(h_main, h_idx, h_idx, kp, ckv, w_uk.astype(BF16), wuv_band.astype(BF16), bias_near, tri)


def _mlstm_prep_kernel(cur_ref, halo_ref, w_ref, b_ref, wqh_ref, wql_ref, wkh_ref, wkl_ref,
                       q_ref, k_ref, xbuf, *, tl):
    i = pl.program_id(1)
    xbuf[SUBLANES:SUBLANES + tl, :] = cur_ref[...]
    xbuf[0:SUBLANES, :] = jnp.where(i > 0, halo_ref[...], 0.0)
    base = SUBLANES - (MLSTM_CONV - 1)
    acc = jnp.broadcast_to(b_ref[...], (tl, MLSTM_WIDTH))
    for k in range(MLSTM_CONV):
        acc = acc + w_ref[k:k + 1, :] * xbuf[base + k:base + k + tl, :]
    xc = acc * jax.nn.sigmoid(acc)
    q_ref[...] = _dot3(xc, wqh_ref[...], wql_ref[...]) * (MLSTM_HEAD_DIM ** -0.5)
    k_ref[...] = _dot3(xc, wkh_ref[...], wkl_ref[...])


def _block_diag(w):
    h, d, e = w.shape
    out = jnp.zeros((h * d, h * e), F32)
    for i in range(h):
        out = out.at[i * d:(i + 1) * d, i * e:(i + 1) * e].set(w[i])
    return out


def _mlstm_prep(h_main, conv_w, conv_b, w_qm, w_km, B, L, tl=512):
    T = B * L
    nl = L // tl
    hb = tl // SUBLANES
    wq = _wsplit(_block_diag(w_qm))
    wk = _wsplit(_block_diag(w_km))
    full = lambda shape: pl.BlockSpec(shape, lambda b, i: (0,) * len(shape))
    return pl.pallas_call(
        functools.partial(_mlstm_prep_kernel, tl=tl),
        grid=(B, nl),
        in_specs=[
            pl.BlockSpec((tl, MLSTM_WIDTH), lambda b, i: (b * nl + i, 4)),
            pl.BlockSpec((SUBLANES, MLSTM_WIDTH),
                         lambda b, i: (jnp.maximum((b * nl + i) * hb - 1, 0), 4)),
            full((MLSTM_CONV, MLSTM_WIDTH)), full((1, MLSTM_WIDTH)),
            full((MLSTM_WIDTH, MLSTM_WIDTH)), full((MLSTM_WIDTH, MLSTM_WIDTH)),
            full((MLSTM_WIDTH, MLSTM_WIDTH)), full((MLSTM_WIDTH, MLSTM_WIDTH)),
        ],
        out_specs=[pl.BlockSpec((tl, MLSTM_WIDTH), lambda b, i: (b * nl + i, 0)),
                   pl.BlockSpec((tl, MLSTM_WIDTH), lambda b, i: (b * nl + i, 0))],
        out_shape=[jax.ShapeDtypeStruct((T, MLSTM_WIDTH), F32),
                   jax.ShapeDtypeStruct((T, MLSTM_WIDTH), F32)],
        scratch_shapes=[pltpu.VMEM((SUBLANES + tl, MLSTM_WIDTH), F32)],
        compiler_params=_cparams(("parallel", "parallel")),
        name="mlstm_prep",
    )(h_main, h_main, conv_w, conv_b.reshape(1, -1), wq[0], wq[1], wk[0], wk[1])


ML_ROWS = 128


def _mlstm_kernel(q_ref, k_ref, v_ref, o_ref, sm_ref, bi_ref, bf_ref, g_ref, out_ref,
                  cm_ref, n_ref, m_ref):
    C = MLSTM_CHUNK
    dh = MLSTM_HEAD_DIM

    @pl.when(pl.program_id(1) == 0)
    def _():
        cm_ref[...] = jnp.zeros(cm_ref.shape, F32)
        n_ref[...] = jnp.zeros(n_ref.shape, F32)
        m_ref[...] = jnp.zeros(m_ref.shape, F32)

    sm_t = sm_ref[...].T
    ig = sm_t[SM_I:SM_I + SUBLANES] + bi_ref[...]
    fg = jax.nn.log_sigmoid(sm_t[SM_F:SM_F + SUBLANES] + bf_ref[...])
    lane = lax.broadcasted_iota(I32, (SUBLANES, ML_ROWS), 1) & (C - 1)
    bcum = fg
    s = 1
    while s < C:
        bcum = bcum + jnp.where(lane >= s, pltpu.roll(bcum, s, axis=1), 0.0)
        s *= 2
    cols = jnp.concatenate([bcum, ig, jnp.zeros((LANES - 2 * SUBLANES, ML_ROWS), F32)], axis=0).T
    tri = (lax.broadcasted_iota(I32, (C, C), 1) <= lax.broadcasted_iota(I32, (C, C), 0))

    q_all, k_all, v_all, o_all = q_ref[...], k_ref[...], v_ref[...], o_ref[...]
    g_all = g_ref[...]
    for c in range(ML_ROWS // C):
        r0 = c * C
        for h in range(MLSTM_HEADS):
            hs = slice(h * dh, (h + 1) * dh)
            qj = q_all[r0:r0 + C, hs]
            kj = k_all[r0:r0 + C, hs]
            vj = v_all[r0:r0 + C, hs]
            b_row = bcum[h:h + 1, r0:r0 + C]
            i_row = ig[h:h + 1, r0:r0 + C]
            b_col = cols[r0:r0 + C, h:h + 1]
            i_col = cols[r0:r0 + C, SUBLANES + h:SUBLANES + h + 1]
            m_prev = m_ref[h:h + 1, 0:1]
            n_prev = n_ref[h:h + 1, :]
            cm_prev = cm_ref[h]

            dm = jnp.where(tri, b_col - b_row + i_row, NEG_INF)
            inter = b_col + m_prev
            m_row = jnp.maximum(inter, jnp.max(dm, axis=-1, keepdims=True))
            w_inter = jnp.exp(inter - m_row)
            qb, kb, vb = qj.astype(BF16), kj.astype(BF16), vj.astype(BF16)
            sw = _dot_t(qb, kb) * jnp.exp(dm - m_row)
            num = _dot(sw.astype(BF16), vb) + w_inter * _dot(qb, cm_prev.astype(BF16))
            den = (jnp.sum(sw, axis=-1, keepdims=True)
                   + w_inter * jnp.sum(qj * n_prev, axis=-1, keepdims=True))
            hh = num / jnp.maximum(jnp.abs(den), jnp.exp(-m_row))

            b_last = b_row[:, C - 1:C]
            g_row = b_last - b_row + i_row
            g_col = b_last - b_col + i_col
            m_new = jnp.maximum(b_last + m_prev, jnp.max(g_row, axis=-1, keepdims=True))
            decay = jnp.exp(b_last + m_prev - m_new)
            kw = kj * jnp.exp(g_col - m_new)
            cm_ref[h] = decay * cm_prev + _dot_tl(kw.astype(BF16), vb)
            n_ref[h:h + 1, :] = decay * n_prev + jnp.sum(kw, axis=0, keepdims=True)
            m_ref[h:h + 1, :] = jnp.broadcast_to(m_new, (1, LANES))

            mu = jnp.mean(hh, axis=-1, keepdims=True)
            hc = hh - mu
            var = jnp.mean(hc * hc, axis=-1, keepdims=True)
            hn = hc * lax.rsqrt(var + LN_EPS) * g_all[:, hs]
            out_ref[r0:r0 + C, hs] = jax.nn.sigmoid(o_all[r0:r0 + C, hs]) * hn


def _mlstm(q, k, h_main, h_idx, b_i, b_f, norm_g, B, L):
    T = B * L
    nl = L // ML_ROWS
    pad8 = lambda v: jnp.pad(v.astype(F32), (0, SUBLANES - MLSTM_HEADS)).reshape(SUBLANES, 1)
    row = lambda col: pl.BlockSpec((ML_ROWS, MLSTM_WIDTH), lambda b, i: (b * nl + i, col))
    return pl.pallas_call(
        _mlstm_kernel,
        grid=(B, nl),
        in_specs=[row(0), row(0), row(5), row(6),
                  pl.BlockSpec((ML_ROWS, LANES), lambda b, i: (b * nl + i, 2)),
                  pl.BlockSpec((SUBLANES, 1), lambda b, i: (0, 0)),
                  pl.BlockSpec((SUBLANES, 1), lambda b, i: (0, 0)),
                  pl.BlockSpec((1, MLSTM_WIDTH), lambda b, i: (0, 0))],
        out_specs=row(0),
        out_shape=jax.ShapeDtypeStruct((T, MLSTM_WIDTH), F32),
        scratch_shapes=[pltpu.VMEM((MLSTM_HEADS, MLSTM_HEAD_DIM, MLSTM_HEAD_DIM), F32),
                        pltpu.VMEM((SUBLANES, MLSTM_HEAD_DIM), F32),
                        pltpu.VMEM((SUBLANES, LANES), F32)],
        compiler_params=_cparams(("parallel", "arbitrary")),
        name="mlstm_scan",
    )(q, k, h_main, h_main, h_idx, pad8(b_i), pad8(b_f), norm_g.reshape(1, -1))


def _mix_out_kernel(ya_ref, yb_ref, yc_ref, x_ref, wa_ref, wb_ref, wc_ref, g_ref, b_ref, o_ref, *,
                    alpha):
    y = (_dot(ya_ref[...].astype(BF16), wa_ref[...])
         + _dot(yb_ref[...].astype(BF16), wb_ref[...])
         + _dot(yc_ref[...].astype(BF16), wc_ref[...]))
    o_ref[...] = _layer_norm(alpha * x_ref[...] + y, g_ref[...], b_ref[...])


def _mix_out(y_a, y_b, y_c, x, w_out, g, b, alpha, tm=512):
    T, D = x.shape
    w = w_out.astype(BF16)
    wa, wb, wc = w[:CONV_CH], w[CONV_CH:CONV_CH + DSA_WIDTH], w[CONV_CH + DSA_WIDTH:]
    rows = lambda width: pl.BlockSpec((tm, width), lambda i: (i, 0))
    full = lambda a: pl.BlockSpec(a.shape, lambda i: (0, 0))
    g2, b2 = g.reshape(1, D), b.reshape(1, D)
    return pl.pallas_call(
        functools.partial(_mix_out_kernel, alpha=alpha),
        grid=(T // tm,),
        in_specs=[rows(CONV_CH), rows(DSA_WIDTH), rows(MLSTM_WIDTH), rows(D),
                  full(wa), full(wb), full(wc), full(g2), full(b2)],
        out_specs=rows(D),
        out_shape=jax.ShapeDtypeStruct((T, D), F32),
        compiler_params=_cparams(("parallel",)),
        name="mix_out",
    )(y_a, y_b, y_c, x, wa, wb, wc, g2, b2)


def _xattn_kernel(x_ref, kv_ref, wq_ref, wo_ref, g_ref, b_ref, o_ref, *, alpha):
    x = x_ref[...]
    q = _dot(x.astype(BF16), wq_ref[...])
    kv = kv_ref[...]
    scale = XATTN_HEAD_DIM ** -0.5
    outs = []
    for h in range(XATTN_HEADS):
        hs = slice(h * XATTN_HEAD_DIM, (h + 1) * XATTN_HEAD_DIM)
        kh = kv[:, hs].astype(BF16)
        vh = kv[:, D_MODEL + h * XATTN_HEAD_DIM:D_MODEL + (h + 1) * XATTN_HEAD_DIM].astype(BF16)
        lg = _dot_t(q[:, hs].astype(BF16), kh) * scale
        lg = lg - jnp.max(lg, axis=-1, keepdims=True)
        p = jnp.exp(lg)
        p = p / jnp.sum(p, axis=-1, keepdims=True)
        outs.append(_dot(p.astype(BF16), vh))
    o = jnp.concatenate(outs, axis=-1)
    y = _dot(o.astype(BF16), wo_ref[...])
    o_ref[...] = _layer_norm(alpha * x + y, g_ref[...], b_ref[...])


def _xattn(x, kv, w_q, w_o, g, b, alpha, B, L, tm=512):
    T, D = x.shape
    nl = L // tm
    M = kv.shape[0] // B
    full = lambda a: pl.BlockSpec(a.shape, lambda bb, i: (0, 0))
    g2, b2 = g.reshape(1, D), b.reshape(1, D)
    wq, wo = w_q.astype(BF16), w_o.astype(BF16)
    return pl.pallas_call(
        functools.partial(_xattn_kernel, alpha=alpha),
        grid=(B, nl),
        in_specs=[pl.BlockSpec((tm, D), lambda bb, i: (bb * nl + i, 0)),
                  pl.BlockSpec((M, 2 * D), lambda bb, i: (bb, 0)),
                  full(wq), full(wo), full(g2), full(b2)],
        out_specs=pl.BlockSpec((tm, D), lambda bb, i: (bb * nl + i, 0)),
        out_shape=jax.ShapeDtypeStruct((T, D), F32),
        compiler_params=_cparams(("parallel", "parallel")),
        name="xattn",
    )(x, kv, wq, wo, g2, b2)


def _peer_score_kernel(x_ref, w_ref, k1h_ref, k1l_ref, k2h_ref, k2l_ref, st_ref):
    q = _dot(x_ref[...].astype(BF16), w_ref[...])
    half = PEER_QDIM // 2
    for h in range(PEER_HEADS):
        for part, (kh_ref, kl_ref) in enumerate(((k1h_ref, k1l_ref), (k2h_ref, k2l_ref))):
            c0 = h * PEER_QDIM + part * half
            q_hi, q_lo = _split(q[:, c0:c0 + half])
            kh, kl = kh_ref[...], kl_ref[...]
            st_ref[2 * h + part] = _dot_t(kh, q_hi) + _dot_t(kl, q_hi) + _dot_t(kh, q_lo)


def _peer_scores(x, w_pq, sub_k1, sub_k2, tm=512):
    T, D = x.shape
    w = w_pq.astype(BF16)
    k1h, k1l = _wsplit(sub_k1)
    k2h, k2l = _wsplit(sub_k2)
    full = lambda a: pl.BlockSpec(a.shape, lambda i: (0, 0))
    return pl.pallas_call(
        _peer_score_kernel,
        grid=(T // tm,),
        in_specs=[pl.BlockSpec((tm, D), lambda i: (i, 0)), full(w),
                  full(k1h), full(k1l), full(k2h), full(k2l)],
        out_specs=pl.BlockSpec((2 * PEER_HEADS, PEER_KEYS, tm), lambda i: (0, 0, i)),
        out_shape=jax.ShapeDtypeStruct((2 * PEER_HEADS, PEER_KEYS, T), F32),
        compiler_params=_cparams(("parallel",)),
        name="peer_scores",
    )(x, w, k1h, k1l, k2h, k2l)


PEER_NTOP = PEER_TOPK + 1
PEER_PAIR_ROWS = tuple(PEER_NTOP // (k + 1) for k in range(PEER_NTOP))
PEER_CAND_ROWS = -(-sum(PEER_PAIR_ROWS) // SUBLANES) * SUBLANES


def _peer_thr_kernel(st_ref, stats_ref, v2_ref, cand_ref):
    def top_rows(x):
        rows = []
        for _ in range(PEER_NTOP):
            m = jnp.max(x, axis=0, keepdims=True)
            rows.append(m)
            x = jnp.where(x == m, NEG_INF, x)
        return rows

    v1 = top_rows(st_ref[0])
    v2 = top_rows(st_ref[1])
    for k in range(PEER_NTOP):
        v2_ref[k:k + 1, :] = v2[k]
    r = 0
    for k, n in enumerate(PEER_PAIR_ROWS):
        cand_ref[r:r + n, :] = v1[k] + v2_ref[0:n, :]
        r += n
    cand_ref[r:PEER_CAND_ROWS, :] = jnp.full((PEER_CAND_ROWS - r, cand_ref.shape[1]), NEG_INF, F32)
    cand = cand_ref[...]
    x = cand
    for _ in range(PEER_TOPK - 1):
        m = jnp.max(x, axis=0, keepdims=True)
        x = jnp.where(x == m, NEG_INF, x)
    thr = jnp.max(x, axis=0, keepdims=True)
    nxt = jnp.max(jnp.where(x == thr, NEG_INF, x), axis=0, keepdims=True)
    top = v1[0] + v2[0]
    z = jnp.sum(jnp.where(cand >= thr, jnp.exp(cand - top), 0.0), axis=0, keepdims=True)
    cut = jnp.where(nxt > NEG_INF, 0.5 * thr + 0.5 * nxt, thr)
    pad = jnp.zeros((SUBLANES - 4, thr.shape[1]), F32)
    stats_ref[0] = jnp.concatenate([cut, v1[0], v2[0], 1.0 / z, pad], axis=0)


def _peer_thresholds(st, tm=1024):
    T = st.shape[2]
    return pl.pallas_call(
        _peer_thr_kernel,
        grid=(PEER_HEADS, T // tm),
        in_specs=[pl.BlockSpec((2, PEER_KEYS, tm), lambda h, i: (h, 0, i))],
        out_specs=pl.BlockSpec((1, SUBLANES, tm), lambda h, i: (h, 0, i)),
        out_shape=jax.ShapeDtypeStruct((PEER_HEADS, SUBLANES, T), F32),
        scratch_shapes=[pltpu.VMEM((3 * SUBLANES, tm), F32),
                        pltpu.VMEM((PEER_CAND_ROWS, tm), F32)],
        compiler_params=_cparams(("parallel", "parallel")),
        name="peer_thresholds",
    )(st)


PEER_TL = 512
PEER_ET = 512
PEER_SUB = 64
GELU_FOLD = 2.0 ** -0.5
PEER_PIECE = 256

def _peer_mix_kernel(x_ref, st_ref, stats_ref, u_ref, vt_ref, g_ref, b_ref, o_ref,
                     xb_ref, d1_ref, e1_ref, e2_ref, gate_ref, w_ref, acc_ref, *, alpha):
    j = pl.program_id(1)
    n_tiles = pl.num_programs(1) - 1
    sub_tiles = PEER_ET // PEER_KEYS
    slot = j % 2

    @pl.when(j == 0)
    def _():
        xb_ref[...] = x_ref[...].astype(BF16)
        for h in range(PEER_HEADS):
            st = stats_ref[h]
            s1 = st_ref[2 * h]
            d1_ref[h] = st[0:1] - s1
            e1_ref[h] = jnp.exp(s1 - st[1:2]) * (st[3:4] * GELU_FOLD)
            e2_ref[h] = jnp.exp(st_ref[2 * h + 1] - st[2:3])
        acc_ref[...] = jnp.zeros(acc_ref.shape, F32)
        w_ref[...] = jnp.zeros(w_ref.shape, BF16)

    def gate_block(a, lb):
        i1 = jnp.minimum(j, n_tiles - 1) * sub_tiles + a
        ls = slice(lb * LANES, (lb + 1) * LANES)
        d_rows = [d1_ref[h, pl.ds(i1, 1), :][:, ls] for h in range(PEER_HEADS)]
        c_rows = [e1_ref[h, pl.ds(i1, 1), :][:, ls] for h in range(PEER_HEADS)]
        for k0 in range(0, PEER_KEYS, PEER_SUB):
            ks = slice(k0, k0 + PEER_SUB)
            gate = None
            for h in range(PEER_HEADS):
                picked = st_ref[2 * h + 1, ks, ls] >= d_rows[h]
                term = jnp.where(picked, e2_ref[h, ks, ls] * c_rows[h], 0.0)
                gate = term if gate is None else gate + term
            gate_ref[a * PEER_KEYS + k0:a * PEER_KEYS + k0 + PEER_SUB, ls] = gate

    def derived_zero(a, lb):
        g = gate_ref[a * PEER_KEYS:a * PEER_KEYS + SUBLANES, lb * LANES:(lb + 1) * LANES]
        u = lax.bitcast_convert_type(g, jnp.uint32)
        z = lax.bitcast_convert_type((u >> 16) >> 16, F32).astype(BF16)
        return jnp.concatenate([z, z], axis=0)

    blocks = [(a, lb) for a in range(sub_tiles) for lb in range(PEER_TL // LANES)]
    pieces = ([("value", k) for k in range(PEER_ET // PEER_PIECE)]
              + [("score", k) for k in range(D_MODEL // PEER_PIECE)])
    base, extra = divmod(len(blocks), len(pieces))
    starts = np.cumsum([0] + [base + (n < extra) for n in range(len(pieces))])
    bf16_rows = 2 * SUBLANES
    yv = None
    su = None
    for n, (kind, k) in enumerate(pieces):
        k0 = k * PEER_PIECE
        ks = slice(k0, k0 + PEER_PIECE)
        if n > 0:
            z = derived_zero(*blocks[int(starts[n]) - 1])
            if kind == "value":
                w_ref[1 - slot, k0:k0 + bf16_rows, 0:LANES] = w_ref[1 - slot, k0:k0 + bf16_rows, 0:LANES] + z
            else:
                xb_ref[0:bf16_rows, k0:k0 + LANES] = xb_ref[0:bf16_rows, k0:k0 + LANES] + z
        if kind == "value":
            piece = _dot(vt_ref[:, ks], w_ref[1 - slot, ks, :])
            yv = piece if yv is None else yv + piece
        else:
            piece = _dot_t(u_ref[:, ks], xb_ref[:, ks])
            su = piece if su is None else su + piece
        for a, lb in blocks[int(starts[n]):int(starts[n + 1])]:
            gate_block(a, lb)
    acc_ref[...] += yv
    act = su + su * lax.erf(su)
    w_ref[slot] = (gate_ref[...] * act).astype(BF16)

    @pl.when(j == n_tiles)
    def _():
        y = acc_ref[...].T
        o_ref[...] = _layer_norm(alpha * x_ref[...] + y, g_ref[...], b_ref[...])


def _peer_mix(x, st, stats, peer_u, peer_v, g, b, alpha):
    T, D = x.shape
    E = peer_u.shape[0]
    u = (peer_u * GELU_FOLD).astype(BF16)
    vt = peer_v.astype(BF16).T
    g2, b2 = g.reshape(1, D), b.reshape(1, D)
    tl, et = PEER_TL, PEER_ET
    n_e = E // et
    return pl.pallas_call(
        functools.partial(_peer_mix_kernel, alpha=alpha),
        grid=(T // tl, n_e + 1),
        in_specs=[pl.BlockSpec((tl, D), lambda i, j: (i, 0)),
                  pl.BlockSpec((2 * PEER_HEADS, PEER_KEYS, tl), lambda i, j: (0, 0, i)),
                  pl.BlockSpec((PEER_HEADS, SUBLANES, tl), lambda i, j: (0, 0, i)),
                  pl.BlockSpec((et, D), lambda i, j: (jnp.minimum(j, n_e - 1), 0)),
                  pl.BlockSpec((D, et), lambda i, j: (0, jnp.maximum(j - 1, 0))),
                  pl.BlockSpec((1, D), lambda i, j: (0, 0)),
                  pl.BlockSpec((1, D), lambda i, j: (0, 0))],
        out_specs=pl.BlockSpec((tl, D), lambda i, j: (i, 0)),
        out_shape=jax.ShapeDtypeStruct((T, D), F32),
        scratch_shapes=[pltpu.VMEM((tl, D), BF16),
                        pltpu.VMEM((PEER_HEADS, PEER_KEYS, tl), F32),
                        pltpu.VMEM((PEER_HEADS, PEER_KEYS, tl), F32),
                        pltpu.VMEM((PEER_HEADS, PEER_KEYS, tl), F32),
                        pltpu.VMEM((et, tl), F32),
                        pltpu.VMEM((2, et, tl), BF16),
                        pltpu.VMEM((D, tl), F32)],
        compiler_params=_cparams(("parallel", "arbitrary")),
        name="peer_mix",
    )(x, st, stats, u, vt, g2, b2)


def _pack_w_in(w):
    sizes = (512, 512, 128, 256, 64, 4, 256, 256, 256, 4, 4)
    offs = np.concatenate([[0], np.cumsum(sizes)])
    seg = lambda n: w[:, int(offs[n]):int(offs[n + 1])]
    a_in, q_b, c_b, qi, ki, wi, xc, v_m, o_m, i_m, f_m = (seg(n) for n in range(len(sizes)))
    z = lambda n: jnp.zeros((w.shape[0], n), w.dtype)
    main = jnp.concatenate([a_in, q_b, xc, v_m, o_m, c_b], axis=1)
    idx = jnp.concatenate([qi, ki, wi, z(4), i_m, z(4), f_m, z(IDX_W - 256 - SM_F - 4)], axis=1)
    return main, idx


def kernel(x, mem, ln_in_g, ln_in_b, rel_bias, w_in, conv_a_w, conv_a_b, norm_a_g, norm_a_b,
           kv_norm_g, w_uk, w_uv, conv_m_w, conv_m_b, w_qm, w_km, b_i, b_f, norm_m_g, w_out,
           ln1_g, ln1_b, w_cq, w_ckv, w_co, ln2_g, ln2_b, w_pq, sub_k1, sub_k2, peer_u, peer_v,
           ln3_g, ln3_b):
    B, L, D = x.shape
    T = B * L
    depth = w_in.shape[0]
    alpha = (2.0 * depth) ** 0.25
    xs = _entry_ln(x.reshape(T, D), ln_in_g, ln_in_b)
    mem2 = mem.reshape(-1, D)
    for l in range(depth):
        w_main, w_idx = _pack_w_in(w_in[l])
        h_main = _matmul(xs, w_main, 1, 512, MAIN_W, "proj_main")
        h_idx = _matmul(xs, w_idx, 3, 512, IDX_W, "proj_idx")
        y_a = _conv_group(h_main, conv_a_w[l], conv_a_b[l], norm_a_g[l], norm_a_b[l], B, L)
        ckv, kp = _dsa_prep(h_main, h_idx, kv_norm_g[l])
        y_b = _dsa_attention(h_main, h_idx, ckv, kp, w_uk[l], w_uv[l], rel_bias, B, L)
        q_m, k_m = _mlstm_prep(h_main, conv_m_w[l], conv_m_b[l], w_qm[l], w_km[l], B, L)
        y_c = _mlstm(q_m, k_m, h_main, h_idx, b_i[l], b_f[l], norm_m_g[l], B, L)
        xs = _mix_out(y_a, y_b, y_c, xs, w_out[l], ln1_g[l], ln1_b[l], alpha)

        kv = _matmul(mem2, w_ckv[l], 1, mem2.shape[0], 512, "xattn_kv")
        xs = _xattn(xs, kv, w_cq[l], w_co[l], ln2_g[l], ln2_b[l], alpha, B, L)

        st = _peer_scores(xs, w_pq[l], sub_k1[l], sub_k2[l])
        stats = _peer_thresholds(st)
        xs = _peer_mix(xs, st, stats, peer_u[l], peer_v[l], ln3_g[l], ln3_b[l], alpha)
    return xs.reshape(B, L, D)
```
